```python
import jax
import jax.numpy as jnp
from jax import lax
import numpy as np

D_MODEL = 1024
BATCH = 2
SEQ = 8192
DEPTH = 2

HEAD_DIM = 64
BLOCK = 128
NORM_EPS = 1e-6
FOX_HEADS = 4
FORGET_BIAS_MEAN = 2.0
SWA_Q_HEADS = 4
SWA_KV_HEADS = 2
SWA_GROUP = SWA_Q_HEADS // SWA_KV_HEADS
SWA_WINDOW = 128
MLA_HEADS = 4
MLA_Q_LORA = 256
MLA_KV_LORA = 128
MLA_NOPE_DIM = 64
MLA_ROPE_DIM = 32
MLA_V_DIM = 64
ROPE_THETA = 10000.0
DIL_HEADS = 4
DIL_PATTERNS = ((128, 1), (512, 4), (2048, 16))
MIX_WIDTH = (FOX_HEADS + SWA_Q_HEADS + DIL_HEADS) * HEAD_DIM + MLA_HEADS * MLA_V_DIM
IN_SIZES = (
    FOX_HEADS * HEAD_DIM, FOX_HEADS * HEAD_DIM, FOX_HEADS * HEAD_DIM, FOX_HEADS,
    SWA_Q_HEADS * HEAD_DIM, SWA_KV_HEADS * HEAD_DIM, SWA_KV_HEADS * HEAD_DIM,
    MLA_Q_LORA, MLA_KV_LORA, MLA_ROPE_DIM,
    DIL_HEADS * HEAD_DIM, DIL_HEADS * HEAD_DIM, DIL_HEADS * HEAD_DIM,
)
N_IN = sum(IN_SIZES)
FFN_DIM = 3584
N_EXPERTS = 8
TOP_K = 2
N_DENSE = (DEPTH + 1) // 2
N_MOE = DEPTH // 2

kernel_name = "hybrid_parallel_heads_fox_swa_mla_dilated_moe"


def rms_norm(x, g):
    xf = x.astype(jnp.float32)
    y = xf * lax.rsqrt(jnp.mean(xf * xf, axis=-1, keepdims=True) + NORM_EPS)
    return (y * g.astype(jnp.float32)).astype(x.dtype)


def to_heads(t, n_heads):
    B, S, _ = t.shape
    return t.reshape(B, S, n_heads, -1).transpose(0, 2, 1, 3)


def from_heads(t):
    B, H, S, d = t.shape
    return t.transpose(0, 2, 1, 3).reshape(B, S, H * d)


def alibi_slopes():
    n = SWA_Q_HEADS + DIL_HEADS
    return 2.0 ** (-8.0 * jnp.arange(1, n + 1, dtype=jnp.float32) / n)


def rope(t, positions):
    half = t.shape[-1] // 2
    inv_freq = ROPE_THETA ** (-jnp.arange(half, dtype=jnp.float32) / half)
    ang = positions[:, None, :, None].astype(jnp.float32) * inv_freq
    cos, sin = jnp.cos(ang), jnp.sin(ang)
    t1 = t[..., :half].astype(jnp.float32)
    t2 = t[..., half:].astype(jnp.float32)
    return jnp.concatenate([t1 * cos - t2 * sin, t2 * cos + t1 * sin], axis=-1).astype(t.dtype)


def causal_dense_attention(q, k, v, scale, log_decay=None):
    B, H, S, dq = q.shape
    nb = S // BLOCK
    key_pos = jnp.arange(S)
    xs = {"i": jnp.arange(nb), "q": jnp.moveaxis(q.reshape(B, H, nb, BLOCK, dq), 2, 0)}
    if log_decay is not None:
        xs["d"] = jnp.moveaxis(log_decay.reshape(B, H, nb, BLOCK), 2, 0)

    def one_block(blk):
        s = jnp.einsum("bhqd,bhkd->bhqk", blk["q"], k).astype(jnp.float32) * scale
        if "d" in blk:
            s = s + blk["d"][..., :, None] - log_decay[..., None, :]
        q_pos = blk["i"] * BLOCK + jnp.arange(BLOCK)
        s = jnp.where(key_pos[None, :] <= q_pos[:, None], s, -jnp.inf)
        p = jax.nn.softmax(s, axis=-1)
        return jnp.einsum("bhqk,bhkd->bhqd", p.astype(v.dtype), v)

    out = lax.map(one_block, xs)
    return jnp.moveaxis(out, 0, 2).reshape(B, H, S, v.shape[-1])


def banded_attention(q, k, v, span, slopes, dist_unit, scale):
    B, K, G, L, dh = q.shape
    dv = v.shape[-1]
    nb = -(-L // BLOCK)
    pad = nb * BLOCK - L
    q = jnp.pad(q, ((0, 0), (0, 0), (0, 0), (0, pad), (0, 0))).reshape(B, K, G, nb, BLOCK, dh)
    kv_pad = ((0, 0), (0, 0), (BLOCK, pad), (0, 0))
    k = jnp.pad(k, kv_pad).reshape(B, K, nb + 1, BLOCK, dh)
    v = jnp.pad(v, kv_pad).reshape(B, K, nb + 1, BLOCK, dv)
    k_band = jnp.concatenate([k[:, :, :-1], k[:, :, 1:]], axis=3)
    v_band = jnp.concatenate([v[:, :, :-1], v[:, :, 1:]], axis=3)
    s = jnp.einsum("bkgnqd,bknjd->bkgnqj", q, k_band).astype(jnp.float32) * scale
    dist = BLOCK + jnp.arange(BLOCK)[:, None] - jnp.arange(2 * BLOCK)[None, :]
    key_idx = (jnp.arange(nb)[:, None] - 1) * BLOCK + jnp.arange(2 * BLOCK)[None, :]
    valid = ((dist >= 0) & (dist <= span))[None] & (key_idx >= 0)[:, None, :]
    s = s - slopes.astype(jnp.float32)[None, :, :, None, None, None] * (dist * dist_unit).astype(jnp.float32)
    s = jnp.where(valid, s, -jnp.inf)
    m = jnp.max(s, axis=-1, keepdims=True)
    p = jnp.exp(s - m)
    l = jnp.sum(p, axis=-1, keepdims=True)
    o = jnp.einsum("bkgnqj,bknjd->bkgnqd", (p / l).astype(v.dtype), v_band)
    lse = (m + jnp.log(l))[..., 0]
    o = o.reshape(B, K, G, nb * BLOCK, dv)[:, :, :, :L]
    lse = lse.reshape(B, K, G, nb * BLOCK)[:, :, :, :L]
    return o, lse


def fox_mixer(q, k, v, f_logit, b_forget):
    qh, kh, vh = to_heads(q, FOX_HEADS), to_heads(k, FOX_HEADS), to_heads(v, FOX_HEADS)
    log_f = jax.nn.log_sigmoid(f_logit.astype(jnp.float32) + b_forget.astype(jnp.float32))
    cum = jnp.cumsum(jnp.transpose(log_f, (0, 2, 1)), axis=-1)
    return from_heads(causal_dense_attention(qh, kh, vh, HEAD_DIM ** -0.5, cum))


def swa_sink_mixer(q, k, v, sink, slopes):
    B, S, _ = q.shape
    qh = q.reshape(B, S, SWA_KV_HEADS, SWA_GROUP, HEAD_DIM).transpose(0, 2, 3, 1, 4)
    kh, vh = to_heads(k, SWA_KV_HEADS), to_heads(v, SWA_KV_HEADS)
    o, lse = banded_attention(qh, kh, vh, SWA_WINDOW - 1,
                              slopes.reshape(SWA_KV_HEADS, SWA_GROUP), 1, HEAD_DIM ** -0.5)
    sink = sink.astype(jnp.float32).reshape(SWA_KV_HEADS, SWA_GROUP)[None, :, :, None]
    o = o * jax.nn.sigmoid(lse - sink)[..., None].astype(o.dtype)
    return o.transpose(0, 3, 1, 2, 4).reshape(B, S, SWA_Q_HEADS * HEAD_DIM)


def mla_mixer(c_q, c_kv, k_rope, positions, q_norm, w_q_up, kv_norm, w_kv_up):
    B, S, _ = c_q.shape
    q = to_heads(rms_norm(c_q, q_norm) @ w_q_up, MLA_HEADS)
    kv = to_heads(rms_norm(c_kv, kv_norm) @ w_kv_up, MLA_HEADS)
    q = jnp.concatenate([q[..., :MLA_NOPE_DIM], rope(q[..., MLA_NOPE_DIM:], positions)], axis=-1)
    k_r = rope(k_rope[:, None], positions)
    k = jnp.concatenate([kv[..., :MLA_NOPE_DIM],
                         jnp.broadcast_to(k_r, (B, MLA_HEADS, S, MLA_ROPE_DIM))], axis=-1)
    o = causal_dense_attention(q, k, kv[..., MLA_NOPE_DIM:], (MLA_NOPE_DIM + MLA_ROPE_DIM) ** -0.5)
    return from_heads(o)


def by_residue(t, dil):
    B, H, S, d = t.shape
    return t.reshape(B, H, S // dil, dil, d).transpose(0, 1, 3, 2, 4).reshape(B, H * dil, S // dil, d)


def dilated_mixer(q, k, v, slopes):
    B, S, _ = q.shape
    qh, kh, vh = to_heads(q, DIL_HEADS), to_heads(k, DIL_HEADS), to_heads(v, DIL_HEADS)
    outs, lses = [], []
    for window, dil in DIL_PATTERNS:
        L = S // dil
        o, lse = banded_attention(by_residue(qh, dil)[:, :, None], by_residue(kh, dil), by_residue(vh, dil),
                                  window // dil, jnp.repeat(slopes, dil)[:, None], dil, HEAD_DIM ** -0.5)
        outs.append(o[:, :, 0].reshape(B, DIL_HEADS, dil, L, HEAD_DIM).transpose(0, 1, 3, 2, 4)
                    .reshape(B, DIL_HEADS, S, HEAD_DIM))
        lses.append(lse[:, :, 0].reshape(B, DIL_HEADS, dil, L).transpose(0, 1, 3, 2).reshape(B, DIL_HEADS, S))
    w = jax.nn.softmax(jnp.stack(lses), axis=0)
    o = jnp.sum(w[..., None].astype(qh.dtype) * jnp.stack(outs), axis=0)
    return from_heads(o)


def swiglu(t, w_gate, w_up, w_down):
    return (jax.nn.silu(t @ w_gate) * (t @ w_up)) @ w_down


def moe_swiglu(h, router, w_gate, w_up, w_down):
    B, S, D = h.shape
    t = h.reshape(B * S, D)
    logits = (t @ router).astype(jnp.float32)
    top_val, top_idx = lax.top_k(logits, TOP_K)
    gates = jax.nn.softmax(top_val, axis=-1)
    combine = jnp.sum(jax.nn.one_hot(top_idx, N_EXPERTS, dtype=jnp.float32) * gates[..., None], axis=1)
    out = jnp.zeros_like(t)
    for e in range(N_EXPERTS):
        out = out + combine[:, e:e + 1].astype(t.dtype) * swiglu(t, w_gate[e], w_up[e], w_down[e])
    return out.reshape(B, S, D)


def setup_inputs(seed: int = 0) -> dict:
    key = jax.random.key(seed)
    ks = jax.random.split(key, 24)
    f32 = jnp.float32

    def nrm(k, shape, fan_in):
        return jax.random.normal(k, shape, f32) * fan_in ** -0.5

    def gain(k, shape):
        return 1.0 + 0.02 * jax.random.normal(k, shape, f32)

    x = jax.random.normal(ks[0], (BATCH, SEQ, D_MODEL), f32)
    start = jax.random.randint(ks[1], (BATCH, 1), 0, 1024, dtype=jnp.int32)
    positions = start + jnp.arange(SEQ, dtype=jnp.int32)[None, :]
    return {
        "x": x,
        "positions": positions,
        "attn_norm": gain(ks[2], (DEPTH, D_MODEL)),
        "w_in": nrm(ks[3], (DEPTH, D_MODEL, N_IN), D_MODEL),
        "b_forget": FORGET_BIAS_MEAN + 0.1 * jax.random.normal(ks[4], (DEPTH, FOX_HEADS), f32),
        "mla_q_norm": gain(ks[5], (DEPTH, MLA_Q_LORA)),
        "w_q_up": nrm(ks[6], (DEPTH, MLA_Q_LORA, MLA_HEADS * (MLA_NOPE_DIM + MLA_ROPE_DIM)), MLA_Q_LORA),
        "mla_kv_norm": gain(ks[7], (DEPTH, MLA_KV_LORA)),
        "w_kv_up": nrm(ks[8], (DEPTH, MLA_KV_LORA, MLA_HEADS * (MLA_NOPE_DIM + MLA_V_DIM)), MLA_KV_LORA),
        "sinks": 0.5 * jax.random.normal(ks[9], (DEPTH, SWA_Q_HEADS), f32),
        "w_out": nrm(ks[10], (DEPTH, MIX_WIDTH, D_MODEL), MIX_WIDTH),
        "ffn_norm": gain(ks[11], (DEPTH, D_MODEL)),
        "dense_w_gate": nrm(ks[12], (N_DENSE, D_MODEL, FFN_DIM), D_MODEL),
        "dense_w_up": nrm(ks[13], (N_DENSE, D_MODEL, FFN_DIM), D_MODEL),
        "dense_w_down": nrm(ks[14], (N_DENSE, FFN_DIM, D_MODEL), FFN_DIM),
        "router": nrm(ks[15], (N_MOE, D_MODEL, N_EXPERTS), D_MODEL),
        "moe_w_gate": nrm(ks[16], (N_MOE, N_EXPERTS, D_MODEL, FFN_DIM), D_MODEL),
        "moe_w_up": nrm(ks[17], (N_MOE, N_EXPERTS, D_MODEL, FFN_DIM), D_MODEL),
        "moe_w_down": nrm(ks[18], (N_MOE, N_EXPERTS, FFN_DIM, D_MODEL), FFN_DIM),
        "final_norm": gain(ks[19], (D_MODEL,)),
    }


def reference(x, positions, attn_norm, w_in, b_forget, mla_q_norm, w_q_up, mla_kv_norm, w_kv_up, sinks,
              w_out, ffn_norm, dense_w_gate, dense_w_up, dense_w_down, router, moe_w_gate, moe_w_up,
              moe_w_down, final_norm):
    slopes = alibi_slopes()
    splits = np.cumsum(IN_SIZES)[:-1].tolist()
    for layer in range(DEPTH):
        h = rms_norm(x, attn_norm[layer])
        (fq, fk, fv, ff, sq, sk, sv, cq, ckv, kr, dq, dk, dv) = jnp.split(h @ w_in[layer], splits, axis=-1)
        y_a = fox_mixer(fq, fk, fv, ff, b_forget[layer])
        y_b = swa_sink_mixer(sq, sk, sv, sinks[layer], slopes[:SWA_Q_HEADS])
        y_c = mla_mixer(cq, ckv, kr, positions, mla_q_norm[layer], w_q_up[layer],
                        mla_kv_norm[layer], w_kv_up[layer])
        y_d = dilated_mixer(dq, dk, dv, slopes[SWA_Q_HEADS:])
        mixed = jnp.concatenate([y_a, y_b, y_c, y_d], axis=-1)
        x = x + mixed @ w_out[layer]
        h = rms_norm(x, ffn_norm[layer])
        j = layer // 2
        if layer % 2 == 0:
            x = x + swiglu(h, dense_w_gate[j], dense_w_up[j], dense_w_down[j])
        else:
            x = x + moe_swiglu(h, router[j], moe_w_gate[j], moe_w_up[j], moe_w_down[j])
    return rms_norm(x, final_norm)
```

```python
import functools

import numpy as np
import jax
import jax.numpy as jnp
from jax import lax
from jax.experimental import pallas as pl
from jax.experimental.pallas import tpu as pltpu

D_MODEL = 1024
BATCH = 2
SEQ = 8192
DEPTH = 2
TOKENS = BATCH * SEQ
HEAD_DIM = 64
BAND = 128
NORM_EPS = 1e-6
FOX_HEADS = 4
SWA_Q_HEADS = 4
SWA_KV_HEADS = 2
SWA_WINDOW = 128
MLA_HEADS = 4
MLA_Q_LORA = 256
MLA_KV_LORA = 128
MLA_NOPE_DIM = 64
MLA_ROPE_DIM = 32
MLA_V_DIM = 64
ROPE_THETA = 10000.0
DIL_HEADS = 4
DIL_PATTERNS = ((128, 1), (512, 4), (2048, 16))
DIL_BLOCK = BAND * max(d for _, d in DIL_PATTERNS)
FFN_DIM = 3584
N_EXPERTS = 8
LANES = 128
NEG = -1e30
VMEM_LIMIT = 56 * 1024 * 1024

_OFF = np.cumsum([0, 256, 256, 256, 4, 256, 128, 128, 256, 128, 32, 256, 256, 256])
(_A_Q, _A_K, _A_V, _A_F, _B_Q, _B_K, _B_V, _C_Q, _C_KV, _C_KR, _D_Q, _D_K, _D_V) = _OFF[:13].tolist()
N_PROJ_BLOCKS = 21

BF16 = jnp.bfloat16
F32 = jnp.float32


def _alibi_slopes():
    n = SWA_Q_HEADS + DIL_HEADS
    return [2.0 ** (-8.0 * i / n) for i in range(1, n + 1)]


def _proj_columns():
    idx = np.zeros((N_PROJ_BLOCKS * LANES,), np.int32)
    sgn = np.zeros((N_PROJ_BLOCKS * LANES,), np.float32)

    def put(dst, src, n, sign=1.0):
        idx[dst:dst + n] = np.arange(src, src + n)
        sgn[dst:dst + n] = sign

    put(0, _A_Q, 256); put(256, _A_K, 256); put(512, _A_V, 256)
    for blk, heads in ((6, (0, 2)), (7, (1, 3))):
        for half, h in enumerate(heads):
            put(blk * LANES + half * HEAD_DIM, _B_Q + h * HEAD_DIM, HEAD_DIM)
    put(8 * LANES, _B_K, 128); put(9 * LANES, _B_V, 128)
    put(10 * LANES, _C_Q, 256); put(12 * LANES, _C_KV, 128)
    half = MLA_ROPE_DIM // 2
    put(13 * LANES + MLA_NOPE_DIM, _C_KR, MLA_ROPE_DIM)
    put(14 * LANES + MLA_NOPE_DIM, _C_KR + half, half, -1.0)
    put(14 * LANES + MLA_NOPE_DIM + half, _C_KR, half)
    put(15 * LANES, _D_Q, 256); put(17 * LANES, _D_K, 256); put(19 * LANES, _D_V, 256)
    return idx, sgn


def _mla_q_columns():
    idx = np.zeros((8 * LANES,), np.int32)
    sgn = np.zeros((8 * LANES,), np.float32)
    half = MLA_ROPE_DIM // 2
    dq = MLA_NOPE_DIM + MLA_ROPE_DIM
    for h in range(MLA_HEADS):
        a = h * LANES
        idx[a:a + dq] = np.arange(h * dq, (h + 1) * dq); sgn[a:a + dq] = 1.0
        b = (MLA_HEADS + h) * LANES + MLA_NOPE_DIM
        r = h * dq + MLA_NOPE_DIM
        idx[b:b + half] = np.arange(r + half, r + 2 * half); sgn[b:b + half] = -1.0
        idx[b + half:b + 2 * half] = np.arange(r, r + half); sgn[b + half:b + 2 * half] = 1.0
    return idx, sgn


def _mla_kv_columns():
    idx = np.zeros((6 * LANES,), np.int32)
    sgn = np.zeros((6 * LANES,), np.float32)
    dkv = MLA_NOPE_DIM + MLA_V_DIM
    for h in range(MLA_HEADS):
        idx[h * LANES:h * LANES + MLA_NOPE_DIM] = np.arange(h * dkv, h * dkv + MLA_NOPE_DIM)
        sgn[h * LANES:h * LANES + MLA_NOPE_DIM] = 1.0
        b = MLA_HEADS * LANES + h * MLA_V_DIM
        idx[b:b + MLA_V_DIM] = np.arange(h * dkv + MLA_NOPE_DIM, (h + 1) * dkv)
        sgn[b:b + MLA_V_DIM] = 1.0
    return idx, sgn


def _mix_rows():
    rows = np.arange(4 * 256)
    b = 256
    perm = np.concatenate([np.arange(b + h * HEAD_DIM, b + (h + 1) * HEAD_DIM) for h in (0, 2, 1, 3)])
    rows[b:b + 256] = perm
    return rows


def _rms(x, g):
    return x * lax.rsqrt(jnp.mean(x * x, axis=-1, keepdims=True) + NORM_EPS) * g


def _dot_nt(a, b):
    return lax.dot_general(a, b, (((1,), (1,)), ((), ())), preferred_element_type=F32)


def _lane_tile(x, width):
    return x if width == LANES else jnp.concatenate([x] * (width // LANES), axis=1)


def _params(*sem):
    return pltpu.CompilerParams(dimension_semantics=sem, vmem_limit_bytes=VMEM_LIMIT)


def _rope_table_kernel(pos_ref, invf_ref, cos_ref, sin_ref):
    ang = pos_ref[...].astype(F32) * invf_ref[...]
    cos_ref[...] = jnp.cos(ang)
    sin_ref[...] = jnp.sin(ang)


def _rope_tables(positions):
    tm = 2048
    half = MLA_ROPE_DIM // 2
    invf = np.zeros((1, LANES), np.float32)
    f = (ROPE_THETA ** (-np.arange(half, dtype=np.float32) / np.float32(half))).astype(np.float32)
    invf[0, MLA_NOPE_DIM:MLA_NOPE_DIM + half] = f
    invf[0, MLA_NOPE_DIM + half:MLA_NOPE_DIM + 2 * half] = f
    return pl.pallas_call(
        _rope_table_kernel,
        grid=(TOKENS // tm,),
        in_specs=[pl.BlockSpec((tm, 1), lambda i: (i, 0)), pl.BlockSpec((1, LANES), lambda i: (0, 0))],
        out_specs=[pl.BlockSpec((tm, LANES), lambda i: (i, 0))] * 2,
        out_shape=[jax.ShapeDtypeStruct((TOKENS, LANES), F32)] * 2,
        compiler_params=_params("parallel"),
        name="rope_tables",
    )(positions.reshape(TOKENS, 1), jnp.asarray(invf))


def _proj_kernel(x_ref, g_ref, w_ref, wf_ref, qn_ref, wq_ref, kvn_ref, wkv_ref, cos_ref, sin_ref,
                 pa_ref, ft_ref, pb_ref, qc_ref, kc_ref, vc_ref, pd_ref):
    hb = _rms(x_ref[...], g_ref[...]).astype(BF16)
    res = jnp.dot(hb, w_ref[...], preferred_element_type=F32)

    def blk(j, n=1):
        return res[:, j * LANES:(j + n) * LANES]

    qscale = HEAD_DIM ** -0.5
    for j in range(2):
        pa_ref[j] = (blk(j) * qscale).astype(BF16)
    for j in range(2, 6):
        pa_ref[j] = blk(j).astype(BF16)
    ft_ref[...] = _dot_nt(wf_ref[...], hb)
    for j in range(2):
        pb_ref[j] = (blk(6 + j) * qscale).astype(BF16)
    pb_ref[2] = blk(8).astype(BF16)
    pb_ref[3] = blk(9).astype(BF16)

    cos = cos_ref[...]
    sin = sin_ref[...]
    cq = _rms(blk(10, 2), qn_ref[...]).astype(BF16)
    qab = jnp.dot(cq, wq_ref[...], preferred_element_type=F32)
    mla_scale = (MLA_NOPE_DIM + MLA_ROPE_DIM) ** -0.5
    for h in range(MLA_HEADS):
        qa = qab[:, h * LANES:(h + 1) * LANES]
        qb = qab[:, (MLA_HEADS + h) * LANES:(MLA_HEADS + h + 1) * LANES]
        qc_ref[h] = ((qa * cos + qb * sin) * mla_scale).astype(BF16)
    ckv = _rms(blk(12), kvn_ref[...]).astype(BF16)
    kv = jnp.dot(ckv, wkv_ref[...], preferred_element_type=F32)
    k_rot = blk(13) * cos + blk(14) * sin
    for h in range(MLA_HEADS):
        kc_ref[h] = (kv[:, h * LANES:(h + 1) * LANES] + k_rot).astype(BF16)
    for j in range(2):
        vc_ref[j] = kv[:, (MLA_HEADS + j) * LANES:(MLA_HEADS + j + 1) * LANES].astype(BF16)

    for j in range(2):
        pd_ref[j] = blk(15 + j) * qscale
    for j in range(2, 6):
        pd_ref[j] = blk(15 + j)


def _project(x, g, w, wf, qn, wq, kvn, wkv, cos, sin):
    tm = 512
    full = lambda shape: pl.BlockSpec(shape, lambda i: (0,) * len(shape))
    out_blk = lambda n: pl.BlockSpec((n, tm, LANES), lambda i: (0, i, 0))
    out_sds = lambda n, dt: jax.ShapeDtypeStruct((n, TOKENS, LANES), dt)
    return pl.pallas_call(
        _proj_kernel,
        grid=(TOKENS // tm,),
        in_specs=[pl.BlockSpec((tm, D_MODEL), lambda i: (i, 0)), full((1, D_MODEL)),
                  full((D_MODEL, N_PROJ_BLOCKS * LANES)), full((8, D_MODEL)),
                  full((1, MLA_Q_LORA)), full((MLA_Q_LORA, 8 * LANES)),
                  full((1, MLA_KV_LORA)), full((MLA_KV_LORA, 6 * LANES)),
                  pl.BlockSpec((tm, LANES), lambda i: (i, 0)), pl.BlockSpec((tm, LANES), lambda i: (i, 0))],
        out_specs=[out_blk(6), pl.BlockSpec((8, tm), lambda i: (0, i)), out_blk(4), out_blk(4), out_blk(4),
                   out_blk(2), out_blk(6)],
        out_shape=[out_sds(6, BF16), jax.ShapeDtypeStruct((8, TOKENS), F32), out_sds(4, BF16), out_sds(4, BF16),
                   out_sds(4, BF16), out_sds(2, BF16), out_sds(6, F32)],
        compiler_params=_params("parallel"),
        name="in_proj",
    )(x, g, w, wf, qn, wq, kvn, wkv, cos, sin)


def _cum_kernel(ft_ref, b_ref, out_ref, carry_ref, *, cs):
    @pl.when(pl.program_id(1) == 0)
    def _():
        carry_ref[...] = jnp.zeros_like(carry_ref)

    lf = jax.nn.log_sigmoid(ft_ref[...] + b_ref[...])
    row = lax.broadcasted_iota(jnp.int32, (cs, cs), 0)
    col = lax.broadcasted_iota(jnp.int32, (cs, cs), 1)
    tri = jnp.where(row <= col, 1.0, 0.0).astype(BF16)
    hi = lf.astype(BF16)
    r1 = lf - hi.astype(F32)
    mid = r1.astype(BF16)
    lo = (r1 - mid.astype(F32)).astype(BF16)
    c = (jnp.dot(hi, tri, preferred_element_type=F32) + jnp.dot(mid, tri, preferred_element_type=F32)
         + jnp.dot(lo, tri, preferred_element_type=F32))
    carry = carry_ref[...]
    out_ref[...] = -(c + _lane_tile(carry, cs))
    carry_ref[...] = carry + jnp.broadcast_to(c[:, cs - 1:cs], carry.shape)


def _neg_cum_log_forget(ft, b_forget):
    cs = 512
    b8 = jnp.zeros((8, 1), F32).at[:FOX_HEADS, 0].set(b_forget.astype(F32))
    return pl.pallas_call(
        functools.partial(_cum_kernel, cs=cs),
        grid=(BATCH, SEQ // cs),
        in_specs=[pl.BlockSpec((8, cs), lambda b, i: (0, b * (SEQ // cs) + i)),
                  pl.BlockSpec((8, 1), lambda b, i: (0, 0))],
        out_specs=pl.BlockSpec((8, cs), lambda b, i: (0, b * (SEQ // cs) + i)),
        out_shape=jax.ShapeDtypeStruct((8, TOKENS), F32),
        scratch_shapes=[pltpu.VMEM((8, LANES), F32)],
        compiler_params=_params("parallel", "arbitrary"),
        name="fox_cumsum",
    )(ft, b8)


def _flash_kernel(*refs, tq, has_bias, mask_q):
    if has_bias:
        q0_ref, q1_ref, k0_ref, k1_ref, v_ref, nb_ref, o_ref, acc_ref, m_ref, l_ref = refs
    else:
        q0_ref, q1_ref, k0_ref, k1_ref, v_ref, o_ref, acc_ref, m_ref, l_ref = refs
        nb_ref = None
    pair = pl.program_id(1)
    i = pl.program_id(2)
    lane = lax.broadcasted_iota(jnp.int32, (tq, LANES), 1)
    q0 = q0_ref[0]
    q1 = q1_ref[0]
    if mask_q:
        q0 = jnp.where(lane < HEAD_DIM, q0, jnp.zeros_like(q0))
        q1 = jnp.where(lane >= HEAD_DIM, q1, jnp.zeros_like(q1))
    qs = (q0, q1)
    k_refs = (k0_ref, k1_ref)
    m_ref[...] = jnp.full(m_ref.shape, NEG, F32)
    l_ref[...] = jnp.zeros(l_ref.shape, F32)
    acc_ref[...] = jnp.zeros(acc_ref.shape, F32)

    def step(kb, diagonal):
        ks = pl.multiple_of(kb * tq, tq)
        v = v_ref[0, pl.ds(ks, tq), :]
        for h in range(2):
            k = k_refs[h][0, pl.ds(ks, tq), :]
            s = _dot_nt(qs[h], k)
            if has_bias:
                s = s + nb_ref[pl.ds(2 * pair + h, 1), pl.ds(ks, tq)]
            if diagonal:
                r = lax.broadcasted_iota(jnp.int32, (tq, tq), 0)
                c = lax.broadcasted_iota(jnp.int32, (tq, tq), 1)
                s = jnp.where(c <= r, s, NEG)
            m_old = m_ref[h]
            m_new = jnp.maximum(m_old, jnp.max(s, axis=-1, keepdims=True))
            alpha = jnp.exp(m_old - m_new)
            p = jnp.exp(s - _lane_tile(m_new, tq))
            l_ref[h] = alpha * l_ref[h] + jnp.sum(p, axis=-1, keepdims=True)
            acc_ref[h] = alpha * acc_ref[h] + jnp.dot(p.astype(BF16), v, preferred_element_type=F32)
            m_ref[h] = m_new

    def body(kb, carry):
        step(kb, False)
        return carry

    lax.fori_loop(0, i, body, 0)
    step(i, True)
    out = jnp.where(lane < HEAD_DIM, acc_ref[0] / l_ref[0], acc_ref[1] / l_ref[1])
    o_ref[0] = out.astype(o_ref.dtype)


def _flash(q_arr, q_blocks, k_arr, k_blocks, v_arr, v_block0, neg_cum, mask_q):
    tq = 512
    nq = SEQ // tq
    qb0, qb1 = q_blocks
    kb0, kb1 = k_blocks
    q_spec = lambda f: pl.BlockSpec((1, tq, LANES), lambda b, j, i: (f(j), b * nq + i, 0))
    kv_spec = lambda f: pl.BlockSpec((1, SEQ, LANES), lambda b, j, i: (f(j), b, 0))
    in_specs = [q_spec(qb0), q_spec(qb1), kv_spec(kb0), kv_spec(kb1), kv_spec(lambda j: v_block0 + j)]
    args = [q_arr, q_arr, k_arr, k_arr, v_arr]
    if neg_cum is not None:
        in_specs.append(pl.BlockSpec((8, SEQ), lambda b, j, i: (0, b)))
        args.append(neg_cum)
    return pl.pallas_call(
        functools.partial(_flash_kernel, tq=tq, has_bias=neg_cum is not None, mask_q=mask_q),
        grid=(BATCH, 2, nq),
        in_specs=in_specs,
        out_specs=pl.BlockSpec((1, tq, LANES), lambda b, j, i: (j, b * nq + i, 0)),
        out_shape=jax.ShapeDtypeStruct((2, TOKENS, LANES), BF16),
        scratch_shapes=[pltpu.VMEM((2, tq, LANES), F32)] * 3,
        compiler_params=_params("parallel", "parallel", "parallel"),
        name="fox_attention" if neg_cum is not None else "mla_attention",
    )(*args)


def _band_geometry(span):
    qi = lax.broadcasted_iota(jnp.int32, (BAND, 2 * BAND), 0)
    kj = lax.broadcasted_iota(jnp.int32, (BAND, 2 * BAND), 1)
    dist = BAND + qi - kj
    return dist.astype(F32), (dist >= 0) & (dist <= span), kj


def _band_tile(q, kk, vv, bias, valid):
    s = jnp.where(valid, _dot_nt(q, kk) + bias, NEG)
    m = jnp.max(s, axis=-1, keepdims=True)
    p = jnp.exp(s - m)
    l = jnp.sum(p, axis=-1, keepdims=True)
    o = jnp.dot(p.astype(BF16), vv, preferred_element_type=F32) / l
    return o, m + jnp.log(l)


def _swa_kernel(q_ref, k_ref, kp_ref, v_ref, vp_ref, sink_ref, o_ref, kk_ref, vv_ref, *, tb, slopes):
    n = pl.program_id(1)
    kk_ref[0:BAND] = kp_ref[0]
    kk_ref[BAND:] = k_ref[0]
    vv_ref[0:BAND] = vp_ref[0]
    vv_ref[BAND:] = v_ref[0]
    dist, in_span, kj = _band_geometry(SWA_WINDOW - 1)
    lane = lax.broadcasted_iota(jnp.int32, (BAND, LANES), 1)
    low = lane < HEAD_DIM
    for c in range(tb // BAND):
        kk = kk_ref[c * BAND:(c + 2) * BAND]
        vv = vv_ref[c * BAND:(c + 2) * BAND]
        valid = in_span & (kj >= jnp.where(n == 0, BAND, 0)) if c == 0 else in_span
        for jb in range(2):
            q = q_ref[jb, c * BAND:(c + 1) * BAND, :]
            outs = []
            for half in range(2):
                head = jb + 2 * half
                qm = jnp.where(low if half == 0 else ~low, q, jnp.zeros_like(q))
                o, lse = _band_tile(qm, kk, vv, dist * (-slopes[head]), valid)
                outs.append(o * jax.nn.sigmoid(lse - sink_ref[jb:jb + 1, :]))
            o_ref[jb, c * BAND:(c + 1) * BAND, :] = jnp.where(low, outs[0], outs[1]).astype(o_ref.dtype)


def _swa(pb, sinks):
    tb = 512
    nb = SEQ // tb
    r = tb // BAND
    s = sinks.astype(F32)
    sink_lanes = jnp.stack([jnp.concatenate([jnp.full((HEAD_DIM,), s[jb]), jnp.full((HEAD_DIM,), s[jb + 2])])
                            for jb in range(2)])
    cur = lambda blk: pl.BlockSpec((1, tb, LANES), lambda b, n: (blk, b * nb + n, 0))
    prev = lambda blk: pl.BlockSpec((1, BAND, LANES), lambda b, n: (blk, jnp.maximum((b * nb + n) * r - 1, 0), 0))
    return pl.pallas_call(
        functools.partial(_swa_kernel, tb=tb, slopes=_alibi_slopes()[:SWA_Q_HEADS]),
        grid=(BATCH, nb),
        in_specs=[pl.BlockSpec((2, tb, LANES), lambda b, n: (0, b * nb + n, 0)), cur(2), prev(2), cur(3), prev(3),
                  pl.BlockSpec((2, LANES), lambda b, n: (0, 0))],
        out_specs=pl.BlockSpec((2, tb, LANES), lambda b, n: (0, b * nb + n, 0)),
        out_shape=jax.ShapeDtypeStruct((2, TOKENS, LANES), BF16),
        scratch_shapes=[pltpu.VMEM((tb + BAND, LANES), BF16)] * 2,
        compiler_params=_params("parallel", "parallel"),
        name="swa_attention",
    )(pb, pb, pb, pb, pb, sink_lanes)


def _dil_kernel(q_ref, k_ref, kp_ref, v_ref, vp_ref, o_ref, kk_ref, vv_ref, po_ref, pl_ref, *, slopes):
    pair = pl.program_id(1)
    n = pl.program_id(2)
    kk_ref[0:DIL_BLOCK] = kp_ref[0]
    kk_ref[DIL_BLOCK:] = k_ref[0]
    vv_ref[0:DIL_BLOCK] = vp_ref[0]
    vv_ref[DIL_BLOCK:] = v_ref[0]
    lane = lax.broadcasted_iota(jnp.int32, (BAND, LANES), 1)
    low = lane < HEAD_DIM
    for p, (window, dil) in enumerate(DIL_PATTERNS):
        dist, in_span, kj = _band_geometry(window // dil)
        units = DIL_BLOCK // BAND

        def unit(u, carry, p=p, dil=dil, dist=dist, in_span=in_span, kj=kj):
            c = u // dil
            start = c * (BAND * dil) + u % dil
            if dil == 1:
                rows = pl.ds(pl.multiple_of(start, BAND), BAND)
                krows = pl.ds(pl.multiple_of(DIL_BLOCK + start - BAND, BAND), 2 * BAND)
            else:
                rows = pl.ds(start, BAND, stride=dil)
                krows = pl.ds(DIL_BLOCK + start - BAND * dil, 2 * BAND, stride=dil)
            q = q_ref[0, rows, :].astype(BF16)
            kk = kk_ref[krows, :].astype(BF16)
            vv = vv_ref[krows, :].astype(BF16)
            valid = in_span & (kj >= jnp.where((n == 0) & (c == 0), BAND, 0))
            outs, lses = [], []
            for half in range(2):
                slope = jnp.where(pair == 0, slopes[half], slopes[2 + half])
                qm = jnp.where(low if half == 0 else ~low, q, jnp.zeros_like(q))
                o, lse = _band_tile(qm, kk, vv, dist * (-slope * dil), valid)
                outs.append(o)
                lses.append(jnp.broadcast_to(lse, (BAND, LANES)))
            po_ref[p, rows, :] = jnp.where(low, outs[0], outs[1])
            pl_ref[p, rows, :] = jnp.where(low, lses[0], lses[1])
            return carry

        lax.fori_loop(0, units, unit, 0)
    chunk = 256

    def merge(t, carry):
        rows = pl.ds(pl.multiple_of(t * chunk, chunk), chunk)
        lse = pl_ref[:, rows, :]
        w = jnp.exp(lse - jnp.max(lse, axis=0, keepdims=True))
        o_ref[0, rows, :] = (jnp.sum(w * po_ref[:, rows, :], axis=0) / jnp.sum(w, axis=0)).astype(o_ref.dtype)
        return carry

    lax.fori_loop(0, DIL_BLOCK // chunk, merge, 0)


def _dilated(pd):
    nb = SEQ // DIL_BLOCK
    cur = lambda off: pl.BlockSpec((1, DIL_BLOCK, LANES), lambda b, j, n: (off + j, b * nb + n, 0))
    prev = lambda off: pl.BlockSpec((1, DIL_BLOCK, LANES),
                                    lambda b, j, n: (off + j, b * nb + jnp.maximum(n - 1, 0), 0))
    return pl.pallas_call(
        functools.partial(_dil_kernel, slopes=_alibi_slopes()[SWA_Q_HEADS:]),
        grid=(BATCH, 2, nb),
        in_specs=[cur(0), cur(2), prev(2), cur(4), prev(4)],
        out_specs=pl.BlockSpec((1, DIL_BLOCK, LANES), lambda b, j, n: (j, b * nb + n, 0)),
        out_shape=jax.ShapeDtypeStruct((2, TOKENS, LANES), BF16),
        scratch_shapes=[pltpu.VMEM((2 * DIL_BLOCK, LANES), F32)] * 2
                       + [pltpu.VMEM((len(DIL_PATTERNS), DIL_BLOCK, LANES), F32)] * 2,
        compiler_params=_params("parallel", "parallel", "parallel"),
        name="dilated_attention",
    )(pd, pd, pd, pd, pd)


def _out_kernel(x_ref, ya_ref, yb_ref, yc_ref, yd_ref, w_ref, o_ref):
    mixed = jnp.concatenate([y[j] for y in (ya_ref, yb_ref, yc_ref, yd_ref) for j in range(2)], axis=1)
    o_ref[...] = x_ref[...] + jnp.dot(mixed, w_ref[...], preferred_element_type=F32)


def _out_proj(x, ys, w):
    tm = 512
    y_spec = pl.BlockSpec((2, tm, LANES), lambda i: (0, i, 0))
    return pl.pallas_call(
        _out_kernel,
        grid=(TOKENS // tm,),
        in_specs=[pl.BlockSpec((tm, D_MODEL), lambda i: (i, 0))] + [y_spec] * 4
                 + [pl.BlockSpec((D_MODEL, D_MODEL), lambda i: (0, 0))],
        out_specs=pl.BlockSpec((tm, D_MODEL), lambda i: (i, 0)),
        out_shape=jax.ShapeDtypeStruct((TOKENS, D_MODEL), F32),
        compiler_params=_params("parallel"),
        name="out_proj",
    )(x, *ys, w)


def _ffn_kernel(*refs, moe):
    if moe:
        x_ref, g_ref, wg_ref, wu_ref, wd_ref, router_ref, fg_ref, o_ref, h_ref, acc_ref, comb_ref = refs
    else:
        x_ref, g_ref, wg_ref, wu_ref, wd_ref, o_ref, h_ref, acc_ref = refs
    e = pl.program_id(1)
    f = pl.program_id(2)

    @pl.when((e == 0) & (f == 0))
    def _():
        x = x_ref[...]
        h = _rms(x, g_ref[...])
        h_ref[...] = h.astype(BF16)
        acc_ref[...] = x
        if moe:
            lane = lax.broadcasted_iota(jnp.int32, (x.shape[0], LANES), 1).astype(F32)
            logits = jnp.dot(h, router_ref[...], preferred_element_type=F32, precision=lax.Precision.HIGHEST)
            logits = jnp.where(lane < N_EXPERTS, logits, NEG)
            m1 = jnp.max(logits, axis=-1, keepdims=True)
            i1 = jnp.min(jnp.where(logits == m1, lane, float(LANES)), axis=-1, keepdims=True)
            rest = jnp.where(lane == i1, NEG, logits)
            m2 = jnp.max(rest, axis=-1, keepdims=True)
            i2 = jnp.min(jnp.where(rest == m2, lane, float(LANES)), axis=-1, keepdims=True)
            t = jnp.exp(m2 - m1)
            comb_ref[...] = jnp.where(lane == i1, 1.0 / (1.0 + t), 0.0) + jnp.where(lane == i2, t / (1.0 + t), 0.0)

    h = h_ref[...]
    gate = jnp.dot(h, wg_ref[0], preferred_element_type=F32)
    up = jnp.dot(h, wu_ref[0], preferred_element_type=F32)
    act = (gate * jax.nn.sigmoid(gate) * up).astype(BF16)
    y = jnp.dot(act, wd_ref[0], preferred_element_type=F32)
    if moe:
        comb = comb_ref[...]
        lane = lax.broadcasted_iota(jnp.int32, comb.shape, 1)
        y = y * jnp.sum(jnp.where(lane == e, comb, 0.0), axis=-1, keepdims=True)
    acc_ref[...] += y

    @pl.when((e == pl.num_programs(1) - 1) & (f == pl.num_programs(2) - 1))
    def _():
        if moe:
            o_ref[...] = _rms(acc_ref[...], fg_ref[...])
        else:
            o_ref[...] = acc_ref[...]


def _ffn(x, g, wg, wu, wd, router=None, final_g=None):
    moe = router is not None
    tm, tf = 1024, 512
    n_e = wg.shape[0]
    in_specs = [pl.BlockSpec((tm, D_MODEL), lambda i, e, f: (i, 0)),
                pl.BlockSpec((1, D_MODEL), lambda i, e, f: (0, 0)),
                pl.BlockSpec((1, D_MODEL, tf), lambda i, e, f: (e, 0, f)),
                pl.BlockSpec((1, D_MODEL, tf), lambda i, e, f: (e, 0, f)),
                pl.BlockSpec((1, tf, D_MODEL), lambda i, e, f: (e, f, 0))]
    args = [x, g, wg, wu, wd]
    scratch = [pltpu.VMEM((tm, D_MODEL), BF16), pltpu.VMEM((tm, D_MODEL), F32)]
    if moe:
        in_specs += [pl.BlockSpec((D_MODEL, LANES), lambda i, e, f: (0, 0)),
                     pl.BlockSpec((1, D_MODEL), lambda i, e, f: (0, 0))]
        args += [router, final_g]
        scratch.append(pltpu.VMEM((tm, LANES), F32))
    return pl.pallas_call(
        functools.partial(_ffn_kernel, moe=moe),
        grid=(TOKENS // tm, n_e, FFN_DIM // tf),
        in_specs=in_specs,
        out_specs=pl.BlockSpec((tm, D_MODEL), lambda i, e, f: (i, 0)),
        out_shape=jax.ShapeDtypeStruct((TOKENS, D_MODEL), F32),
        scratch_shapes=scratch,
        compiler_params=_params("parallel", "arbitrary", "arbitrary"),
        name="moe_ffn" if moe else "dense_ffn",
    )(*args)


def kernel(x, positions, attn_norm, w_in, b_forget, mla_q_norm, w_q_up, mla_kv_norm, w_kv_up, sinks, w_out, ffn_norm, dense_w_gate, dense_w_up, dense_w_down, router, moe_w_gate, moe_w_up, moe_w_down, final_norm):
    assert x.shape == (BATCH, SEQ, D_MODEL) and positions.shape == (BATCH, SEQ)
    p_idx, p_sgn = _proj_columns()
    q_idx, q_sgn = _mla_q_columns()
    kv_idx, kv_sgn = _mla_kv_columns()
    mix_rows = _mix_rows()
    cos, sin = _rope_tables(positions)
    xt = x.reshape(TOKENS, D_MODEL).astype(F32)
    for layer in range(DEPTH):
        w = (w_in[layer][:, p_idx] * p_sgn).astype(BF16)
        wf = jnp.zeros((8, D_MODEL), F32).at[:FOX_HEADS].set(w_in[layer][:, _A_F:_A_F + FOX_HEADS].T).astype(BF16)
        wq = (w_q_up[layer][:, q_idx] * q_sgn).astype(BF16)
        wkv = (w_kv_up[layer][:, kv_idx] * kv_sgn).astype(BF16)
        pa, ft, pb, qc, kc, vc, pd = _project(
            xt, attn_norm[layer].reshape(1, D_MODEL), w, wf, mla_q_norm[layer].reshape(1, MLA_Q_LORA), wq,
            mla_kv_norm[layer].reshape(1, MLA_KV_LORA), wkv, cos, sin)
        neg_cum = _neg_cum_log_forget(ft, b_forget[layer])
        same = lambda j: j
        y_a = _flash(pa, (same, same), pa, (lambda j: 2 + j, lambda j: 2 + j), pa, 4, neg_cum, mask_q=True)
        y_b = _swa(pb, sinks[layer])
        y_c = _flash(qc, (lambda j: 2 * j, lambda j: 2 * j + 1), kc, (lambda j: 2 * j, lambda j: 2 * j + 1), vc, 0,
                     None, mask_q=False)
        y_d = _dilated(pd)
        xt = _out_proj(xt, (y_a, y_b, y_c, y_d), w_out[layer][mix_rows].astype(BF16))
        j = layer // 2
        g = ffn_norm[layer].reshape(1, D_MODEL)
        if layer % 2 == 0:
            xt = _ffn(xt, g, dense_w_gate[j][None].astype(BF16), dense_w_up[j][None].astype(BF16),
                      dense_w_down[j][None].astype(BF16))
        else:
            assert layer == DEPTH - 1
            router_pad = jnp.zeros((D_MODEL, LANES), F32).at[:, :N_EXPERTS].set(router[j])
            xt = _ffn(xt, g, moe_w_gate[j].astype(BF16), moe_w_up[j].astype(BF16), moe_w_down[j].astype(BF16),
                      router_pad, final_norm.reshape(1, D_MODEL))
    return xt.reshape(BATCH, SEQ, D_MODEL)
```

```python
import functools

import numpy as np
import jax
import jax.numpy as jnp
from jax import lax
from jax.experimental import pallas as pl
from jax.experimental.pallas import tpu as pltpu

D_MODEL = 1024
BATCH = 2
SEQ = 8192
DEPTH = 2
TOKENS = BATCH * SEQ
HEAD_DIM = 64
BAND = 128
NORM_EPS = 1e-6
FOX_HEADS = 4
SWA_Q_HEADS = 4
SWA_KV_HEADS = 2
SWA_WINDOW = 128
MLA_HEADS = 4
MLA_Q_LORA = 256
MLA_KV_LORA = 128
MLA_NOPE_DIM = 64
MLA_ROPE_DIM = 32
MLA_V_DIM = 64
ROPE_THETA = 10000.0
DIL_HEADS = 4
DIL_PATTERNS = ((128, 1), (512, 4), (2048, 16))
DIL_BLOCK = BAND * max(d for _, d in DIL_PATTERNS)
FFN_DIM = 3584
N_EXPERTS = 8
LANES = 128
NEG = -1e30
VMEM_LIMIT = 56 * 1024 * 1024

_OFF = np.cumsum([0, 256, 256, 256, 4, 256, 128, 128, 256, 128, 32, 256, 256, 256])
(_A_Q, _A_K, _A_V, _A_F, _B_Q, _B_K, _B_V, _C_Q, _C_KV, _C_KR, _D_Q, _D_K, _D_V) = _OFF[:13].tolist()
N_PROJ_BLOCKS = 21

BF16 = jnp.bfloat16
F32 = jnp.float32


def _alibi_slopes():
    n = SWA_Q_HEADS + DIL_HEADS
    return [2.0 ** (-8.0 * i / n) for i in range(1, n + 1)]


def _proj_columns():
    idx = np.zeros((N_PROJ_BLOCKS * LANES,), np.int32)
    sgn = np.zeros((N_PROJ_BLOCKS * LANES,), np.float32)

    def put(dst, src, n, sign=1.0):
        idx[dst:dst + n] = np.arange(src, src + n)
        sgn[dst:dst + n] = sign

    put(0, _A_Q, 256); put(256, _A_K, 256); put(512, _A_V, 256)
    for blk, heads in ((6, (0, 2)), (7, (1, 3))):
        for half, h in enumerate(heads):
            put(blk * LANES + half * HEAD_DIM, _B_Q + h * HEAD_DIM, HEAD_DIM)
    put(8 * LANES, _B_K, 128); put(9 * LANES, _B_V, 128)
    put(10 * LANES, _C_Q, 256); put(12 * LANES, _C_KV, 128)
    half = MLA_ROPE_DIM // 2
    put(13 * LANES + MLA_NOPE_DIM, _C_KR, MLA_ROPE_DIM)
    put(14 * LANES + MLA_NOPE_DIM, _C_KR + half, half, -1.0)
    put(14 * LANES + MLA_NOPE_DIM + half, _C_KR, half)
    put(15 * LANES, _D_Q, 256); put(17 * LANES, _D_K, 256); put(19 * LANES, _D_V, 256)
    return idx, sgn


def _mla_q_columns():
    idx = np.zeros((8 * LANES,), np.int32)
    sgn = np.zeros((8 * LANES,), np.float32)
    half = MLA_ROPE_DIM // 2
    dq = MLA_NOPE_DIM + MLA_ROPE_DIM
    for h in range(MLA_HEADS):
        a = h * LANES
        idx[a:a + dq] = np.arange(h * dq, (h + 1) * dq); sgn[a:a + dq] = 1.0
        b = (MLA_HEADS + h) * LANES + MLA_NOPE_DIM
        r = h * dq + MLA_NOPE_DIM
        idx[b:b + half] = np.arange(r + half, r + 2 * half); sgn[b:b + half] = -1.0
        idx[b + half:b + 2 * half] = np.arange(r, r + half); sgn[b + half:b + 2 * half] = 1.0
    return idx, sgn


def _mla_kv_columns():
    idx = np.zeros((6 * LANES,), np.int32)
    sgn = np.zeros((6 * LANES,), np.float32)
    dkv = MLA_NOPE_DIM + MLA_V_DIM
    for h in range(MLA_HEADS):
        idx[h * LANES:h * LANES + MLA_NOPE_DIM] = np.arange(h * dkv, h * dkv + MLA_NOPE_DIM)
        sgn[h * LANES:h * LANES + MLA_NOPE_DIM] = 1.0
        b = MLA_HEADS * LANES + h * MLA_V_DIM
        idx[b:b + MLA_V_DIM] = np.arange(h * dkv + MLA_NOPE_DIM, (h + 1) * dkv)
        sgn[b:b + MLA_V_DIM] = 1.0
    return idx, sgn


def _mix_rows():
    rows = np.arange(4 * 256)
    b = 256
    perm = np.concatenate([np.arange(b + h * HEAD_DIM, b + (h + 1) * HEAD_DIM) for h in (0, 2, 1, 3)])
    rows[b:b + 256] = perm
    return rows


def _rms(x, g):
    return x * lax.rsqrt(jnp.mean(x * x, axis=-1, keepdims=True) + NORM_EPS) * g


def _dot_nt(a, b):
    return lax.dot_general(a, b, (((1,), (1,)), ((), ())), preferred_element_type=F32)


def _lane_tile(x, width):
    return x if width == LANES else jnp.concatenate([x] * (width // LANES), axis=1)


def _params(*sem):
    return pltpu.CompilerParams(dimension_semantics=sem, vmem_limit_bytes=VMEM_LIMIT)


def _rope_table_kernel(pos_ref, invf_ref, cos_ref, sin_ref):
    ang = pos_ref[...].astype(F32) * invf_ref[...]
    cos_ref[...] = jnp.cos(ang)
    sin_ref[...] = jnp.sin(ang)


def _rope_tables(positions):
    tm = 2048
    half = MLA_ROPE_DIM // 2
    invf = np.zeros((1, LANES), np.float32)
    f = (ROPE_THETA ** (-np.arange(half, dtype=np.float32) / np.float32(half))).astype(np.float32)
    invf[0, MLA_NOPE_DIM:MLA_NOPE_DIM + half] = f
    invf[0, MLA_NOPE_DIM + half:MLA_NOPE_DIM + 2 * half] = f
    return pl.pallas_call(
        _rope_table_kernel,
        grid=(TOKENS // tm,),
        in_specs=[pl.BlockSpec((tm, 1), lambda i: (i, 0)), pl.BlockSpec((1, LANES), lambda i: (0, 0))],
        out_specs=[pl.BlockSpec((tm, LANES), lambda i: (i, 0))] * 2,
        out_shape=[jax.ShapeDtypeStruct((TOKENS, LANES), F32)] * 2,
        compiler_params=_params("parallel"),
        name="rope_tables",
    )(positions.reshape(TOKENS, 1), jnp.asarray(invf))


def _proj_kernel(x_ref, g_ref, w_ref, wf_ref, qn_ref, wq_ref, kvn_ref, wkv_ref, cos_ref, sin_ref,
                 pa_ref, ft_ref, pb_ref, qc_ref, kc_ref, vc_ref, pd_ref):
    hb = _rms(x_ref[...], g_ref[...]).astype(BF16)
    res = jnp.dot(hb, w_ref[...], preferred_element_type=F32)

    def blk(j, n=1):
        return res[:, j * LANES:(j + n) * LANES]

    qscale = HEAD_DIM ** -0.5
    for j in range(2):
        pa_ref[j] = (blk(j) * qscale).astype(BF16)
    for j in range(2, 6):
        pa_ref[j] = blk(j).astype(BF16)
    ft_ref[...] = _dot_nt(wf_ref[...], hb)
    for j in range(2):
        pb_ref[j] = (blk(6 + j) * qscale).astype(BF16)
    pb_ref[2] = blk(8).astype(BF16)
    pb_ref[3] = blk(9).astype(BF16)

    cos = cos_ref[...]
    sin = sin_ref[...]
    cq = _rms(blk(10, 2), qn_ref[...]).astype(BF16)
    qab = jnp.dot(cq, wq_ref[...], preferred_element_type=F32)
    mla_scale = (MLA_NOPE_DIM + MLA_ROPE_DIM) ** -0.5
    for h in range(MLA_HEADS):
        qa = qab[:, h * LANES:(h + 1) * LANES]
        qb = qab[:, (MLA_HEADS + h) * LANES:(MLA_HEADS + h + 1) * LANES]
        qc_ref[h] = ((qa * cos + qb * sin) * mla_scale).astype(BF16)
    ckv = _rms(blk(12), kvn_ref[...]).astype(BF16)
    kv = jnp.dot(ckv, wkv_ref[...], preferred_element_type=F32)
    k_rot = blk(13) * cos + blk(14) * sin
    for h in range(MLA_HEADS):
        kc_ref[h] = (kv[:, h * LANES:(h + 1) * LANES] + k_rot).astype(BF16)
    for j in range(2):
        vc_ref[j] = kv[:, (MLA_HEADS + j) * LANES:(MLA_HEADS + j + 1) * LANES].astype(BF16)

    for j in range(2):
        pd_ref[j] = blk(15 + j) * qscale
    for j in range(2, 6):
        pd_ref[j] = blk(15 + j)


def _project(x, g, w, wf, qn, wq, kvn, wkv, cos, sin):
    tm = 512
    full = lambda shape: pl.BlockSpec(shape, lambda i: (0,) * len(shape))
    out_blk = lambda n: pl.BlockSpec((n, tm, LANES), lambda i: (0, i, 0))
    out_sds = lambda n, dt: jax.ShapeDtypeStruct((n, TOKENS, LANES), dt)
    return pl.pallas_call(
        _proj_kernel,
        grid=(TOKENS // tm,),
        in_specs=[pl.BlockSpec((tm, D_MODEL), lambda i: (i, 0)), full((1, D_MODEL)),
                  full((D_MODEL, N_PROJ_BLOCKS * LANES)), full((8, D_MODEL)),
                  full((1, MLA_Q_LORA)), full((MLA_Q_LORA, 8 * LANES)),
                  full((1, MLA_KV_LORA)), full((MLA_KV_LORA, 6 * LANES)),
                  pl.BlockSpec((tm, LANES), lambda i: (i, 0)), pl.BlockSpec((tm, LANES), lambda i: (i, 0))],
        out_specs=[out_blk(6), pl.BlockSpec((8, tm), lambda i: (0, i)), out_blk(4), out_blk(4), out_blk(4),
                   out_blk(2), out_blk(6)],
        out_shape=[out_sds(6, BF16), jax.ShapeDtypeStruct((8, TOKENS), F32), out_sds(4, BF16), out_sds(4, BF16),
                   out_sds(4, BF16), out_sds(2, BF16), out_sds(6, F32)],
        compiler_params=_params("parallel"),
        name="in_proj",
    )(x, g, w, wf, qn, wq, kvn, wkv, cos, sin)


def _cum_kernel(ft_ref, b_ref, out_ref, carry_ref, *, cs):
    @pl.when(pl.program_id(1) == 0)
    def _():
        carry_ref[...] = jnp.zeros_like(carry_ref)

    lf = jax.nn.log_sigmoid(ft_ref[...] + b_ref[...])
    row = lax.broadcasted_iota(jnp.int32, (cs, cs), 0)
    col = lax.broadcasted_iota(jnp.int32, (cs, cs), 1)
    tri = jnp.where(row <= col, 1.0, 0.0).astype(BF16)
    hi = lf.astype(BF16)
    r1 = lf - hi.astype(F32)
    mid = r1.astype(BF16)
    lo = (r1 - mid.astype(F32)).astype(BF16)
    c = (jnp.dot(hi, tri, preferred_element_type=F32) + jnp.dot(mid, tri, preferred_element_type=F32)
         + jnp.dot(lo, tri, preferred_element_type=F32))
    carry = carry_ref[...]
    out_ref[...] = -(c + _lane_tile(carry, cs))
    carry_ref[...] = carry + jnp.broadcast_to(c[:, cs - 1:cs], carry.shape)


def _neg_cum_log_forget(ft, b_forget):
    cs = 512
    b8 = jnp.zeros((8, 1), F32).at[:FOX_HEADS, 0].set(b_forget.astype(F32))
    return pl.pallas_call(
        functools.partial(_cum_kernel, cs=cs),
        grid=(BATCH, SEQ // cs),
        in_specs=[pl.BlockSpec((8, cs), lambda b, i: (0, b * (SEQ // cs) + i)),
                  pl.BlockSpec((8, 1), lambda b, i: (0, 0))],
        out_specs=pl.BlockSpec((8, cs), lambda b, i: (0, b * (SEQ // cs) + i)),
        out_shape=jax.ShapeDtypeStruct((8, TOKENS), F32),
        scratch_shapes=[pltpu.VMEM((8, LANES), F32)],
        compiler_params=_params("parallel", "arbitrary"),
        name="fox_cumsum",
    )(ft, b8)


def _flash_kernel(*refs, tq, has_bias, mask_q):
    if has_bias:
        q0_ref, q1_ref, k0_ref, k1_ref, v_ref, nb_ref, o_ref, acc_ref, m_ref, l_ref = refs
    else:
        q0_ref, q1_ref, k0_ref, k1_ref, v_ref, o_ref, acc_ref, m_ref, l_ref = refs
        nb_ref = None
    pair = pl.program_id(1)
    i = pl.program_id(2)
    lane = lax.broadcasted_iota(jnp.int32, (tq, LANES), 1)
    q0 = q0_ref[0]
    q1 = q1_ref[0]
    if mask_q:
        q0 = jnp.where(lane < HEAD_DIM, q0, jnp.zeros_like(q0))
        q1 = jnp.where(lane >= HEAD_DIM, q1, jnp.zeros_like(q1))
    qs = (q0, q1)
    k_refs = (k0_ref, k1_ref)
    m_ref[...] = jnp.full(m_ref.shape, NEG, F32)
    l_ref[...] = jnp.zeros(l_ref.shape, F32)
    acc_ref[...] = jnp.zeros(acc_ref.shape, F32)

    def step(kb, diagonal):
        ks = pl.multiple_of(kb * tq, tq)
        v = v_ref[0, pl.ds(ks, tq), :]
        for h in range(2):
            k = k_refs[h][0, pl.ds(ks, tq), :]
            s = _dot_nt(qs[h], k)
            if has_bias:
                s = s + nb_ref[pl.ds(2 * pair + h, 1), pl.ds(ks, tq)]
            if diagonal:
                r = lax.broadcasted_iota(jnp.int32, (tq, tq), 0)
                c = lax.broadcasted_iota(jnp.int32, (tq, tq), 1)
                s = jnp.where(c <= r, s, NEG)
            m_old = m_ref[h]
            m_new = jnp.maximum(m_old, jnp.max(s, axis=-1, keepdims=True))
            alpha = jnp.exp(m_old - m_new)
            p = jnp.exp(s - _lane_tile(m_new, tq))
            l_ref[h] = alpha * l_ref[h] + jnp.sum(p, axis=-1, keepdims=True)
            acc_ref[h] = alpha * acc_ref[h] + jnp.dot(p.astype(BF16), v, preferred_element_type=F32)
            m_ref[h] = m_new

    def body(kb, carry):
        step(kb, False)
        return carry

    lax.fori_loop(0, i, body, 0)
    step(i, True)
    out = jnp.where(lane < HEAD_DIM, acc_ref[0] / l_ref[0], acc_ref[1] / l_ref[1])
    o_ref[0] = out.astype(o_ref.dtype)


def _flash(q_arr, q_blocks, k_arr, k_blocks, v_arr, v_block0, neg_cum, mask_q):
    tq = 512
    nq = SEQ // tq
    qb0, qb1 = q_blocks
    kb0, kb1 = k_blocks
    q_spec = lambda f: pl.BlockSpec((1, tq, LANES), lambda b, j, i: (f(j), b * nq + i, 0))
    kv_spec = lambda f: pl.BlockSpec((1, SEQ, LANES), lambda b, j, i: (f(j), b, 0))
    in_specs = [q_spec(qb0), q_spec(qb1), kv_spec(kb0), kv_spec(kb1), kv_spec(lambda j: v_block0 + j)]
    args = [q_arr, q_arr, k_arr, k_arr, v_arr]
    if neg_cum is not None:
        in_specs.append(pl.BlockSpec((8, SEQ), lambda b, j, i: (0, b)))
        args.append(neg_cum)
    return pl.pallas_call(
        functools.partial(_flash_kernel, tq=tq, has_bias=neg_cum is not None, mask_q=mask_q),
        grid=(BATCH, 2, nq),
        in_specs=in_specs,
        out_specs=pl.BlockSpec((1, tq, LANES), lambda b, j, i: (j, b * nq + i, 0)),
        out_shape=jax.ShapeDtypeStruct((2, TOKENS, LANES), BF16),
        scratch_shapes=[pltpu.VMEM((2, tq, LANES), F32)] * 3,
        compiler_params=_params("parallel", "parallel", "parallel"),
        name="fox_attention" if neg_cum is not None else "mla_attention",
    )(*args)


def _band_geometry(span):
    qi = lax.broadcasted_iota(jnp.int32, (BAND, 2 * BAND), 0)
    kj = lax.broadcasted_iota(jnp.int32, (BAND, 2 * BAND), 1)
    dist = BAND + qi - kj
    return dist.astype(F32), (dist >= 0) & (dist <= span), kj


def _band_tile(q, kk, vv, bias, valid):
    s = jnp.where(valid, _dot_nt(q, kk) + bias, NEG)
    m = jnp.max(s, axis=-1, keepdims=True)
    p = jnp.exp(s - m)
    l = jnp.sum(p, axis=-1, keepdims=True)
    o = jnp.dot(p.astype(BF16), vv, preferred_element_type=F32) / l
    return o, m + jnp.log(l)


def _swa_kernel(q_ref, k_ref, kp_ref, v_ref, vp_ref, sink_ref, o_ref, kk_ref, vv_ref, *, tb, slopes):
    n = pl.program_id(1)
    kk_ref[0:BAND] = kp_ref[0]
    kk_ref[BAND:] = k_ref[0]
    vv_ref[0:BAND] = vp_ref[0]
    vv_ref[BAND:] = v_ref[0]
    dist, in_span, kj = _band_geometry(SWA_WINDOW - 1)
    lane = lax.broadcasted_iota(jnp.int32, (BAND, LANES), 1)
    low = lane < HEAD_DIM
    for c in range(tb // BAND):
        kk = kk_ref[c * BAND:(c + 2) * BAND]
        vv = vv_ref[c * BAND:(c + 2) * BAND]
        valid = in_span & (kj >= jnp.where(n == 0, BAND, 0)) if c == 0 else in_span
        for jb in range(2):
            q = q_ref[jb, c * BAND:(c + 1) * BAND, :]
            outs = []
            for half in range(2):
                head = jb + 2 * half
                qm = jnp.where(low if half == 0 else ~low, q, jnp.zeros_like(q))
                o, lse = _band_tile(qm, kk, vv, dist * (-slopes[head]), valid)
                outs.append(o * jax.nn.sigmoid(lse - sink_ref[jb:jb + 1, :]))
            o_ref[jb, c * BAND:(c + 1) * BAND, :] = jnp.where(low, outs[0], outs[1]).astype(o_ref.dtype)


def _swa(pb, sinks):
    tb = 512
    nb = SEQ // tb
    r = tb // BAND
    s = sinks.astype(F32)
    sink_lanes = jnp.stack([jnp.concatenate([jnp.full((HEAD_DIM,), s[jb]), jnp.full((HEAD_DIM,), s[jb + 2])])
                            for jb in range(2)])
    cur = lambda blk: pl.BlockSpec((1, tb, LANES), lambda b, n: (blk, b * nb + n, 0))
    prev = lambda blk: pl.BlockSpec((1, BAND, LANES), lambda b, n: (blk, jnp.maximum((b * nb + n) * r - 1, 0), 0))
    return pl.pallas_call(
        functools.partial(_swa_kernel, tb=tb, slopes=_alibi_slopes()[:SWA_Q_HEADS]),
        grid=(BATCH, nb),
        in_specs=[pl.BlockSpec((2, tb, LANES), lambda b, n: (0, b * nb + n, 0)), cur(2), prev(2), cur(3), prev(3),
                  pl.BlockSpec((2, LANES), lambda b, n: (0, 0))],
        out_specs=pl.BlockSpec((2, tb, LANES), lambda b, n: (0, b * nb + n, 0)),
        out_shape=jax.ShapeDtypeStruct((2, TOKENS, LANES), BF16),
        scratch_shapes=[pltpu.VMEM((tb + BAND, LANES), BF16)] * 2,
        compiler_params=_params("parallel", "parallel"),
        name="swa_attention",
    )(pb, pb, pb, pb, pb, sink_lanes)


def _dil_kernel(q_ref, k_ref, kp_ref, v_ref, vp_ref, o_ref, kk_ref, vv_ref, po_ref, pl_ref, *, slopes):
    pair = pl.program_id(1)
    n = pl.program_id(2)
    kk_ref[0:DIL_BLOCK] = kp_ref[0]
    kk_ref[DIL_BLOCK:] = k_ref[0]
    vv_ref[0:DIL_BLOCK] = vp_ref[0]
    vv_ref[DIL_BLOCK:] = v_ref[0]
    lane = lax.broadcasted_iota(jnp.int32, (BAND, LANES), 1)
    low = lane < HEAD_DIM
    for p, (window, dil) in enumerate(DIL_PATTERNS):
        dist, in_span, kj = _band_geometry(window // dil)
        units = DIL_BLOCK // BAND

        def unit(u, carry, p=p, dil=dil, dist=dist, in_span=in_span, kj=kj):
            c = u // dil
            start = c * (BAND * dil) + u % dil
            if dil == 1:
                rows = pl.ds(pl.multiple_of(start, BAND), BAND)
                krows = pl.ds(pl.multiple_of(DIL_BLOCK + start - BAND, BAND), 2 * BAND)
            else:
                rows = pl.ds(start, BAND, stride=dil)
                krows = pl.ds(DIL_BLOCK + start - BAND * dil, 2 * BAND, stride=dil)
            q = q_ref[0, rows, :].astype(BF16)
            kk = kk_ref[krows, :].astype(BF16)
            vv = vv_ref[krows, :].astype(BF16)
            valid = in_span & (kj >= jnp.where((n == 0) & (c == 0), BAND, 0))
            outs, lses = [], []
            for half in range(2):
                slope = jnp.where(pair == 0, slopes[half], slopes[2 + half])
                qm = jnp.where(low if half == 0 else ~low, q, jnp.zeros_like(q))
                o, lse = _band_tile(qm, kk, vv, dist * (-slope * dil), valid)
                outs.append(o)
                lses.append(jnp.broadcast_to(lse, (BAND, LANES)))
            po_ref[p, rows, :] = jnp.where(low, outs[0], outs[1])
            pl_ref[p, rows, :] = jnp.where(low, lses[0], lses[1])
            return carry

        lax.fori_loop(0, units, unit, 0, unroll=4)
    chunk = 256

    def merge(t, carry):
        rows = pl.ds(pl.multiple_of(t * chunk, chunk), chunk)
        lse = pl_ref[:, rows, :]
        w = jnp.exp(lse - jnp.max(lse, axis=0, keepdims=True))
        o_ref[0, rows, :] = (jnp.sum(w * po_ref[:, rows, :], axis=0) / jnp.sum(w, axis=0)).astype(o_ref.dtype)
        return carry

    lax.fori_loop(0, DIL_BLOCK // chunk, merge, 0)


def _dilated(pd):
    nb = SEQ // DIL_BLOCK
    cur = lambda off: pl.BlockSpec((1, DIL_BLOCK, LANES), lambda b, j, n: (off + j, b * nb + n, 0))
    prev = lambda off: pl.BlockSpec((1, DIL_BLOCK, LANES),
                                    lambda b, j, n: (off + j, b * nb + jnp.maximum(n - 1, 0), 0))
    return pl.pallas_call(
        functools.partial(_dil_kernel, slopes=_alibi_slopes()[SWA_Q_HEADS:]),
        grid=(BATCH, 2, nb),
        in_specs=[cur(0), cur(2), prev(2), cur(4), prev(4)],
        out_specs=pl.BlockSpec((1, DIL_BLOCK, LANES), lambda b, j, n: (j, b * nb + n, 0)),
        out_shape=jax.ShapeDtypeStruct((2, TOKENS, LANES), BF16),
        scratch_shapes=[pltpu.VMEM((2 * DIL_BLOCK, LANES), F32)] * 2
                       + [pltpu.VMEM((len(DIL_PATTERNS), DIL_BLOCK, LANES), F32)] * 2,
        compiler_params=_params("parallel", "parallel", "parallel"),
        name="dilated_attention",
    )(pd, pd, pd, pd, pd)


def _out_kernel(x_ref, ya_ref, yb_ref, yc_ref, yd_ref, w_ref, o_ref):
    mixed = jnp.concatenate([y[j] for y in (ya_ref, yb_ref, yc_ref, yd_ref) for j in range(2)], axis=1)
    o_ref[...] = x_ref[...] + jnp.dot(mixed, w_ref[...], preferred_element_type=F32)


def _out_proj(x, ys, w):
    tm = 512
    y_spec = pl.BlockSpec((2, tm, LANES), lambda i: (0, i, 0))
    return pl.pallas_call(
        _out_kernel,
        grid=(TOKENS // tm,),
        in_specs=[pl.BlockSpec((tm, D_MODEL), lambda i: (i, 0))] + [y_spec] * 4
                 + [pl.BlockSpec((D_MODEL, D_MODEL), lambda i: (0, 0))],
        out_specs=pl.BlockSpec((tm, D_MODEL), lambda i: (i, 0)),
        out_shape=jax.ShapeDtypeStruct((TOKENS, D_MODEL), F32),
        compiler_params=_params("parallel"),
        name="out_proj",
    )(x, *ys, w)


def _swiglu(h, wg, wu, wd):
    gate = jnp.dot(h, wg, preferred_element_type=F32)
    up = jnp.dot(h, wu, preferred_element_type=F32)
    act = (gate * jax.nn.sigmoid(gate) * up).astype(BF16)
    return jnp.dot(act, wd, preferred_element_type=F32)


def _ffn_kernel(x_ref, g_ref, wg_ref, wu_ref, wd_ref, o_ref, h_ref, acc_ref):
    f = pl.program_id(1)

    @pl.when(f == 0)
    def _():
        x = x_ref[...]
        h_ref[...] = _rms(x, g_ref[...]).astype(BF16)
        acc_ref[...] = x

    acc_ref[...] += _swiglu(h_ref[...], wg_ref[...], wu_ref[...], wd_ref[...])

    @pl.when(f == pl.num_programs(1) - 1)
    def _():
        o_ref[...] = acc_ref[...]


def _ffn(x, g, wg, wu, wd):
    tm, tf = 1024, 512
    return pl.pallas_call(
        _ffn_kernel,
        grid=(TOKENS // tm, FFN_DIM // tf),
        in_specs=[pl.BlockSpec((tm, D_MODEL), lambda i, f: (i, 0)),
                  pl.BlockSpec((1, D_MODEL), lambda i, f: (0, 0)),
                  pl.BlockSpec((D_MODEL, tf), lambda i, f: (0, f)),
                  pl.BlockSpec((D_MODEL, tf), lambda i, f: (0, f)),
                  pl.BlockSpec((tf, D_MODEL), lambda i, f: (f, 0))],
        out_specs=pl.BlockSpec((tm, D_MODEL), lambda i, f: (i, 0)),
        out_shape=jax.ShapeDtypeStruct((TOKENS, D_MODEL), F32),
        scratch_shapes=[pltpu.VMEM((tm, D_MODEL), BF16), pltpu.VMEM((tm, D_MODEL), F32)],
        compiler_params=_params("parallel", "arbitrary"),
        name="dense_ffn",
    )(x, g, wg, wu, wd)


MOE_TM = 1024
MOE_STATIC_ROWS = 320
MOE_CHUNK = 64


def _route_kernel(x_ref, g_ref, router_ref, h_ref, comb_ref, pc_ref, pt_ref, cnt_ref):
    tm = x_ref.shape[0]
    h = _rms(x_ref[...], g_ref[...])
    h_ref[...] = h.astype(BF16)
    lane = lax.broadcasted_iota(jnp.int32, (tm, LANES), 1).astype(F32)
    logits = jnp.dot(h, router_ref[...], preferred_element_type=F32, precision=lax.Precision.HIGHEST)
    logits = jnp.where(lane < N_EXPERTS, logits, NEG)
    m1 = jnp.max(logits, axis=-1, keepdims=True)
    i1 = jnp.min(jnp.where(logits == m1, lane, float(LANES)), axis=-1, keepdims=True)
    rest = jnp.where(lane == i1, NEG, logits)
    m2 = jnp.max(rest, axis=-1, keepdims=True)
    i2 = jnp.min(jnp.where(rest == m2, lane, float(LANES)), axis=-1, keepdims=True)
    t = jnp.exp(m2 - m1)
    comb_ref[...] = jnp.where(lane == i1, 1.0 / (1.0 + t), 0.0) + jnp.where(lane == i2, t / (1.0 + t), 0.0)
    sel = jnp.where(lane == i1, 1.0, jnp.where(lane == i2, 1.0, 0.0))
    selb = sel.astype(BF16)
    r = lax.broadcasted_iota(jnp.int32, (tm, tm), 0)
    c = lax.broadcasted_iota(jnp.int32, (tm, tm), 1)
    rank = jnp.dot(jnp.where(c < r, 1.0, 0.0).astype(BF16), selb, preferred_element_type=F32)
    pc_ref[...] = jnp.where(sel > 0.0, rank, -1.0)
    eye = jnp.where(lax.broadcasted_iota(jnp.int32, (8, LANES), 0) == lax.broadcasted_iota(jnp.int32, (8, LANES), 1),
                    1.0, 0.0).astype(BF16)
    sel_t = _dot_nt(eye, selb)
    rank_t = jnp.dot(sel_t.astype(BF16), jnp.where(r < c, 1.0, 0.0).astype(BF16), preferred_element_type=F32)
    pt_ref[...] = jnp.where(sel_t > 0.0, rank_t, -1.0)
    cnt_ref[...] = jnp.broadcast_to(jnp.sum(sel, axis=0, keepdims=True), (8, LANES))


def _route(x, g, router_pad):
    tm = MOE_TM
    nt = TOKENS // tm
    return pl.pallas_call(
        _route_kernel,
        grid=(nt,),
        in_specs=[pl.BlockSpec((tm, D_MODEL), lambda i: (i, 0)), pl.BlockSpec((1, D_MODEL), lambda i: (0, 0)),
                  pl.BlockSpec((D_MODEL, LANES), lambda i: (0, 0))],
        out_specs=[pl.BlockSpec((tm, D_MODEL), lambda i: (i, 0)), pl.BlockSpec((tm, LANES), lambda i: (i, 0)),
                   pl.BlockSpec((tm, LANES), lambda i: (i, 0)), pl.BlockSpec((8, tm), lambda i: (0, i)),
                   pl.BlockSpec((8, LANES), lambda i: (i, 0))],
        out_shape=[jax.ShapeDtypeStruct((TOKENS, D_MODEL), BF16), jax.ShapeDtypeStruct((TOKENS, LANES), F32),
                   jax.ShapeDtypeStruct((TOKENS, LANES), F32), jax.ShapeDtypeStruct((8, TOKENS), F32),
                   jax.ShapeDtypeStruct((nt * 8, LANES), F32)],
        compiler_params=_params("parallel"),
        name="moe_route",
    )(x, g, router_pad)


def _moe_kernel(cnt_ref, h_ref, pt_ref, pc_ref, comb_ref, wg_ref, wu_ref, wd_ref, o_ref, xg_ref, yacc_ref, acc_ref):
    tm = h_ref.shape[0]
    sm, oc = MOE_STATIC_ROWS, MOE_CHUNK
    i = pl.program_id(0)
    e = pl.program_id(1)
    f = pl.program_id(2)
    n_over = jnp.maximum(cnt_ref[i * N_EXPERTS + e] - sm + oc - 1, 0) // oc

    def overflow(body):
        def step(c, carry):
            body(pl.multiple_of(sm + c * oc, oc), oc)
            return carry
        lax.fori_loop(0, n_over, step, 0)

    @pl.when((e == 0) & (f == 0))
    def _():
        acc_ref[...] = jnp.zeros(acc_ref.shape, F32)

    @pl.when(f == 0)
    def _():
        pt = pt_ref[pl.ds(e, 1), :]

        def gather(row0, nrows):
            ridx = lax.broadcasted_iota(jnp.int32, (nrows, tm), 0) + row0
            onehot = jnp.where(ridx.astype(F32) == pt, 1.0, 0.0).astype(BF16)
            xg_ref[pl.ds(row0, nrows), :] = jnp.dot(onehot, h_ref[...], preferred_element_type=F32).astype(BF16)
            yacc_ref[pl.ds(row0, nrows), :] = jnp.zeros((nrows, D_MODEL), F32)

        gather(0, sm)
        overflow(gather)

    def ffn(row0, nrows):
        rows = pl.ds(row0, nrows)
        yacc_ref[rows, :] += _swiglu(xg_ref[rows, :], wg_ref[0], wu_ref[0], wd_ref[0])

    ffn(0, sm)
    overflow(ffn)

    @pl.when(f == pl.num_programs(2) - 1)
    def _():
        lane = lax.broadcasted_iota(jnp.int32, (tm, LANES), 1)
        mine = lane == e
        pc = jnp.sum(jnp.where(mine, pc_ref[...], 0.0), axis=-1, keepdims=True)
        gate = jnp.sum(jnp.where(mine, comb_ref[...], 0.0), axis=-1, keepdims=True)

        def scatter(row0, nrows):
            cidx = lax.broadcasted_iota(jnp.int32, (tm, nrows), 1) + row0
            onehot = jnp.where(cidx.astype(F32) == pc, 1.0, 0.0).astype(BF16)
            y = yacc_ref[pl.ds(row0, nrows), :].astype(BF16)
            acc_ref[...] += jnp.dot(onehot, y, preferred_element_type=F32) * gate

        scatter(0, sm)
        overflow(scatter)

    @pl.when((e == pl.num_programs(1) - 1) & (f == pl.num_programs(2) - 1))
    def _():
        o_ref[...] = acc_ref[...].astype(o_ref.dtype)


def _moe(h, pt, pc, comb, counts, wg, wu, wd):
    tm, tf = MOE_TM, 512
    grid_spec = pltpu.PrefetchScalarGridSpec(
        num_scalar_prefetch=1,
        grid=(TOKENS // tm, N_EXPERTS, FFN_DIM // tf),
        in_specs=[pl.BlockSpec((tm, D_MODEL), lambda i, e, f, cnt: (i, 0)),
                  pl.BlockSpec((8, tm), lambda i, e, f, cnt: (0, i)),
                  pl.BlockSpec((tm, LANES), lambda i, e, f, cnt: (i, 0)),
                  pl.BlockSpec((tm, LANES), lambda i, e, f, cnt: (i, 0)),
                  pl.BlockSpec((1, D_MODEL, tf), lambda i, e, f, cnt: (e, 0, f)),
                  pl.BlockSpec((1, D_MODEL, tf), lambda i, e, f, cnt: (e, 0, f)),
                  pl.BlockSpec((1, tf, D_MODEL), lambda i, e, f, cnt: (e, f, 0))],
        out_specs=pl.BlockSpec((tm, D_MODEL), lambda i, e, f, cnt: (i, 0)),
        scratch_shapes=[pltpu.VMEM((tm, D_MODEL), BF16), pltpu.VMEM((tm, D_MODEL), F32),
                        pltpu.VMEM((tm, D_MODEL), F32)])
    return pl.pallas_call(
        _moe_kernel,
        grid_spec=grid_spec,
        out_shape=jax.ShapeDtypeStruct((TOKENS, D_MODEL), BF16),
        compiler_params=_params("parallel", "arbitrary", "arbitrary"),
        name="moe_ffn",
    )(counts, h, pt, pc, comb, wg, wu, wd)


def _final_kernel(x_ref, y_ref, g_ref, o_ref):
    o_ref[...] = _rms(x_ref[...] + y_ref[...].astype(F32), g_ref[...])


def _final(x, y, g):
    tm = 1024
    return pl.pallas_call(
        _final_kernel,
        grid=(TOKENS // tm,),
        in_specs=[pl.BlockSpec((tm, D_MODEL), lambda i: (i, 0)), pl.BlockSpec((tm, D_MODEL), lambda i: (i, 0)),
                  pl.BlockSpec((1, D_MODEL), lambda i: (0, 0))],
        out_specs=pl.BlockSpec((tm, D_MODEL), lambda i: (i, 0)),
        out_shape=jax.ShapeDtypeStruct((TOKENS, D_MODEL), F32),
        compiler_params=_params("parallel"),
        name="final_norm",
    )(x, y, g)


def kernel(x, positions, attn_norm, w_in, b_forget, mla_q_norm, w_q_up, mla_kv_norm, w_kv_up, sinks, w_out, ffn_norm, dense_w_gate, dense_w_up, dense_w_down, router, moe_w_gate, moe_w_up, moe_w_down, final_norm):
    assert x.shape == (BATCH, SEQ, D_MODEL) and positions.shape == (BATCH, SEQ)
    p_idx, p_sgn = _proj_columns()
    q_idx, q_sgn = _mla_q_columns()
    kv_idx, kv_sgn = _mla_kv_columns()
    mix_rows = _mix_rows()
    cos, sin = _rope_tables(positions)
    xt = x.reshape(TOKENS, D_MODEL).astype(F32)
    for layer in range(DEPTH):
        w = (w_in[layer][:, p_idx] * p_sgn).astype(BF16)
        wf = jnp.zeros((8, D_MODEL), F32).at[:FOX_HEADS].set(w_in[layer][:, _A_F:_A_F + FOX_HEADS].T).astype(BF16)
        wq = (w_q_up[layer][:, q_idx] * q_sgn).astype(BF16)
        wkv = (w_kv_up[layer][:, kv_idx] * kv_sgn).astype(BF16)
        pa, ft, pb, qc, kc, vc, pd = _project(
            xt, attn_norm[layer].reshape(1, D_MODEL), w, wf, mla_q_norm[layer].reshape(1, MLA_Q_LORA), wq,
            mla_kv_norm[layer].reshape(1, MLA_KV_LORA), wkv, cos, sin)
        neg_cum = _neg_cum_log_forget(ft, b_forget[layer])
        same = lambda j: j
        y_a = _flash(pa, (same, same), pa, (lambda j: 2 + j, lambda j: 2 + j), pa, 4, neg_cum, mask_q=True)
        y_b = _swa(pb, sinks[layer])
        y_c = _flash(qc, (lambda j: 2 * j, lambda j: 2 * j + 1), kc, (lambda j: 2 * j, lambda j: 2 * j + 1), vc, 0,
                     None, mask_q=False)
        y_d = _dilated(pd)
        xt = _out_proj(xt, (y_a, y_b, y_c, y_d), w_out[layer][mix_rows].astype(BF16))
        j = layer // 2
        g = ffn_norm[layer].reshape(1, D_MODEL)
        if layer % 2 == 0:
            xt = _ffn(xt, g, dense_w_gate[j].astype(BF16), dense_w_up[j].astype(BF16), dense_w_down[j].astype(BF16))
        else:
            assert layer == DEPTH - 1
            router_pad = jnp.zeros((D_MODEL, LANES), F32).at[:, :N_EXPERTS].set(router[j])
            h, comb, pc, pt, cnt = _route(xt, g, router_pad)
            counts = cnt[::8, :N_EXPERTS].astype(jnp.int32).reshape(-1)
            y = _moe(h, pt, pc, comb, counts, moe_w_gate[j].astype(BF16), moe_w_up[j].astype(BF16),
                     moe_w_down[j].astype(BF16))
            xt = _final(xt, y, final_norm.reshape(1, D_MODEL))
    return xt.reshape(BATCH, SEQ, D_MODEL)
```

```python
import functools

import numpy as np
import jax
import jax.numpy as jnp
from jax import lax
from jax.experimental import pallas as pl
from jax.experimental.pallas import tpu as pltpu

D_MODEL = 1024
BATCH = 2
SEQ = 8192
DEPTH = 2
TOKENS = BATCH * SEQ
HEAD_DIM = 64
BAND = 128
NORM_EPS = 1e-6
FOX_HEADS = 4
SWA_Q_HEADS = 4
SWA_KV_HEADS = 2
SWA_WINDOW = 128
MLA_HEADS = 4
MLA_Q_LORA = 256
MLA_KV_LORA = 128
MLA_NOPE_DIM = 64
MLA_ROPE_DIM = 32
MLA_V_DIM = 64
ROPE_THETA = 10000.0
DIL_HEADS = 4
DIL_PATTERNS = ((128, 1), (512, 4), (2048, 16))
DIL_BLOCK = BAND * max(d for _, d in DIL_PATTERNS)
FFN_DIM = 3584
N_EXPERTS = 8
LANES = 128
NEG = -1e30
VMEM_LIMIT = 56 * 1024 * 1024

_OFF = np.cumsum([0, 256, 256, 256, 4, 256, 128, 128, 256, 128, 32, 256, 256, 256])
(_A_Q, _A_K, _A_V, _A_F, _B_Q, _B_K, _B_V, _C_Q, _C_KV, _C_KR, _D_Q, _D_K, _D_V) = _OFF[:13].tolist()
N_PROJ_BLOCKS = 22
LOG2E = 1.4426950408889634

BF16 = jnp.bfloat16
F32 = jnp.float32


def _alibi_slopes():
    n = SWA_Q_HEADS + DIL_HEADS
    return [2.0 ** (-8.0 * i / n) for i in range(1, n + 1)]


def _proj_columns():
    idx = np.zeros((N_PROJ_BLOCKS * LANES,), np.int32)
    sgn = np.zeros((N_PROJ_BLOCKS * LANES,), np.float32)

    def put(dst, src, n, sign=1.0):
        idx[dst:dst + n] = np.arange(src, src + n)
        sgn[dst:dst + n] = sign

    put(0, _A_Q, 256); put(256, _A_K, 256); put(512, _A_V, 256)
    for blk, heads in ((6, (0, 2)), (7, (1, 3))):
        for half, h in enumerate(heads):
            put(blk * LANES + half * HEAD_DIM, _B_Q + h * HEAD_DIM, HEAD_DIM)
    put(8 * LANES, _B_K, 128); put(9 * LANES, _B_V, 128)
    put(10 * LANES, _C_Q, 256); put(12 * LANES, _C_KV, 128)
    half = MLA_ROPE_DIM // 2
    put(13 * LANES + MLA_NOPE_DIM, _C_KR, MLA_ROPE_DIM)
    put(14 * LANES + MLA_NOPE_DIM, _C_KR + half, half, -1.0)
    put(14 * LANES + MLA_NOPE_DIM + half, _C_KR, half)
    put(15 * LANES, _D_Q, 256); put(17 * LANES, _D_K, 256); put(19 * LANES, _D_V, 256)
    put(21 * LANES, _A_F, FOX_HEADS)
    return idx, sgn


def _bias_lane0(h):
    return HEAD_DIM if h % 2 == 0 else 0


def _bias_placement():
    place = np.zeros((3 * LANES, FOX_HEADS * LANES), np.float32)
    for piece in range(3):
        for h in range(FOX_HEADS):
            place[piece * LANES + h, h * LANES + _bias_lane0(h) + piece] = 1.0
    return place


def _mla_q_columns():
    idx = np.zeros((8 * LANES,), np.int32)
    sgn = np.zeros((8 * LANES,), np.float32)
    half = MLA_ROPE_DIM // 2
    dq = MLA_NOPE_DIM + MLA_ROPE_DIM
    for h in range(MLA_HEADS):
        a = h * LANES
        idx[a:a + dq] = np.arange(h * dq, (h + 1) * dq); sgn[a:a + dq] = 1.0
        b = (MLA_HEADS + h) * LANES + MLA_NOPE_DIM
        r = h * dq + MLA_NOPE_DIM
        idx[b:b + half] = np.arange(r + half, r + 2 * half); sgn[b:b + half] = -1.0
        idx[b + half:b + 2 * half] = np.arange(r, r + half); sgn[b + half:b + 2 * half] = 1.0
    return idx, sgn


def _mla_kv_columns():
    idx = np.zeros((6 * LANES,), np.int32)
    sgn = np.zeros((6 * LANES,), np.float32)
    dkv = MLA_NOPE_DIM + MLA_V_DIM
    for h in range(MLA_HEADS):
        idx[h * LANES:h * LANES + MLA_NOPE_DIM] = np.arange(h * dkv, h * dkv + MLA_NOPE_DIM)
        sgn[h * LANES:h * LANES + MLA_NOPE_DIM] = 1.0
        b = MLA_HEADS * LANES + h * MLA_V_DIM
        idx[b:b + MLA_V_DIM] = np.arange(h * dkv + MLA_NOPE_DIM, (h + 1) * dkv)
        sgn[b:b + MLA_V_DIM] = 1.0
    return idx, sgn


def _mix_rows():
    rows = np.arange(4 * 256)
    b = 256
    perm = np.concatenate([np.arange(b + h * HEAD_DIM, b + (h + 1) * HEAD_DIM) for h in (0, 2, 1, 3)])
    rows[b:b + 256] = perm
    return rows


def _rms(x, g):
    return x * lax.rsqrt(jnp.mean(x * x, axis=-1, keepdims=True) + NORM_EPS) * g


def _dot_nt(a, b):
    return lax.dot_general(a, b, (((1,), (1,)), ((), ())), preferred_element_type=F32)


def _lane_tile(x, width):
    return x if width == LANES else jnp.concatenate([x] * (width // LANES), axis=1)


def _params(*sem):
    return pltpu.CompilerParams(dimension_semantics=sem, vmem_limit_bytes=VMEM_LIMIT)


def _rope_table_kernel(pos_ref, invf_ref, cos_ref, sin_ref):
    ang = pos_ref[...].astype(F32) * invf_ref[...]
    cos_ref[...] = jnp.cos(ang)
    sin_ref[...] = jnp.sin(ang)


def _rope_tables(positions):
    tm = 2048
    half = MLA_ROPE_DIM // 2
    invf = np.zeros((1, LANES), np.float32)
    f = (ROPE_THETA ** (-np.arange(half, dtype=np.float32) / np.float32(half))).astype(np.float32)
    invf[0, MLA_NOPE_DIM:MLA_NOPE_DIM + half] = f
    invf[0, MLA_NOPE_DIM + half:MLA_NOPE_DIM + 2 * half] = f
    return pl.pallas_call(
        _rope_table_kernel,
        grid=(TOKENS // tm,),
        in_specs=[pl.BlockSpec((tm, 1), lambda i: (i, 0)), pl.BlockSpec((1, LANES), lambda i: (0, 0))],
        out_specs=[pl.BlockSpec((tm, LANES), lambda i: (i, 0))] * 2,
        out_shape=[jax.ShapeDtypeStruct((TOKENS, LANES), F32)] * 2,
        compiler_params=_params("parallel"),
        name="rope_tables",
    )(positions.reshape(TOKENS, 1), jnp.asarray(invf))


def _proj_kernel(x_ref, g_ref, w_ref, qn_ref, wq_ref, kvn_ref, wkv_ref, cos_ref, sin_ref,
                 fq_ref, fk_ref, fv_ref, fl_ref, pb_ref, qc_ref, kc_ref, vc_ref, pd_ref):
    hb = _rms(x_ref[...], g_ref[...]).astype(BF16)
    res = jnp.dot(hb, w_ref[...], preferred_element_type=F32)
    lane = lax.broadcasted_iota(jnp.int32, (x_ref.shape[0], LANES), 1)
    low = lane < HEAD_DIM

    def blk(j, n=1):
        return res[:, j * LANES:(j + n) * LANES]

    def own(h):
        return low if h % 2 == 0 else ~low

    qscale = HEAD_DIM ** -0.5
    for h in range(FOX_HEADS):
        ones = (lane >= _bias_lane0(h)) & (lane < _bias_lane0(h) + 3)
        fq_ref[h] = jnp.where(own(h), blk(h // 2) * (qscale * LOG2E), jnp.where(ones, 1.0, 0.0)).astype(BF16)
        fv_ref[h] = jnp.where(own(h), blk(4 + h // 2), 1.0).astype(BF16)
    for j in range(2):
        fk_ref[j] = blk(2 + j).astype(BF16)
    fl_ref[...] = blk(21)
    for j in range(2):
        pb_ref[j] = (blk(6 + j) * qscale).astype(BF16)
    pb_ref[2] = blk(8).astype(BF16)
    pb_ref[3] = blk(9).astype(BF16)

    cos = cos_ref[...]
    sin = sin_ref[...]
    cq = _rms(blk(10, 2), qn_ref[...]).astype(BF16)
    qab = jnp.dot(cq, wq_ref[...], preferred_element_type=F32)
    mla_scale = (MLA_NOPE_DIM + MLA_ROPE_DIM) ** -0.5
    for h in range(MLA_HEADS):
        qa = qab[:, h * LANES:(h + 1) * LANES]
        qb = qab[:, (MLA_HEADS + h) * LANES:(MLA_HEADS + h + 1) * LANES]
        qc_ref[h] = ((qa * cos + qb * sin) * (mla_scale * LOG2E)).astype(BF16)
    ckv = _rms(blk(12), kvn_ref[...]).astype(BF16)
    kv = jnp.dot(ckv, wkv_ref[...], preferred_element_type=F32)
    k_rot = blk(13) * cos + blk(14) * sin
    for h in range(MLA_HEADS):
        kc_ref[h] = (kv[:, h * LANES:(h + 1) * LANES] + k_rot).astype(BF16)
        v_pair = kv[:, (MLA_HEADS + h // 2) * LANES:(MLA_HEADS + h // 2 + 1) * LANES]
        vc_ref[h] = jnp.where(own(h), v_pair, 1.0).astype(BF16)

    for j in range(2):
        pd_ref[j] = blk(15 + j) * qscale
    for j in range(2, 6):
        pd_ref[j] = blk(15 + j)


def _project(x, g, w, qn, wq, kvn, wkv, cos, sin):
    tm = 512
    full = lambda shape: pl.BlockSpec(shape, lambda i: (0,) * len(shape))
    out_blk = lambda n: pl.BlockSpec((n, tm, LANES), lambda i: (0, i, 0))
    out_sds = lambda n, dt: jax.ShapeDtypeStruct((n, TOKENS, LANES), dt)
    tok_blk = pl.BlockSpec((tm, LANES), lambda i: (i, 0))
    return pl.pallas_call(
        _proj_kernel,
        grid=(TOKENS // tm,),
        in_specs=[pl.BlockSpec((tm, D_MODEL), lambda i: (i, 0)), full((1, D_MODEL)),
                  full((D_MODEL, N_PROJ_BLOCKS * LANES)),
                  full((1, MLA_Q_LORA)), full((MLA_Q_LORA, 8 * LANES)),
                  full((1, MLA_KV_LORA)), full((MLA_KV_LORA, 6 * LANES)), tok_blk, tok_blk],
        out_specs=[out_blk(4), out_blk(2), out_blk(4), tok_blk, out_blk(4), out_blk(4), out_blk(4), out_blk(4),
                   out_blk(6)],
        out_shape=[out_sds(4, BF16), out_sds(2, BF16), out_sds(4, BF16), jax.ShapeDtypeStruct((TOKENS, LANES), F32),
                   out_sds(4, BF16), out_sds(4, BF16), out_sds(4, BF16), out_sds(4, BF16), out_sds(6, F32)],
        compiler_params=_params("parallel"),
        name="in_proj",
    )(x, g, w, qn, wq, kvn, wkv, cos, sin)


def _split3(x):
    hi = x.astype(BF16)
    r1 = x - hi.astype(F32)
    mid = r1.astype(BF16)
    return hi, mid, (r1 - mid.astype(F32)).astype(BF16)


def _fox_keys_kernel(fl_ref, b_ref, k_ref, place_ref, kf_ref, carry_ref, *, cs):
    @pl.when(pl.program_id(1) == 0)
    def _():
        carry_ref[...] = jnp.zeros_like(carry_ref)

    lf = jax.nn.log_sigmoid(fl_ref[...] + b_ref[...])
    row = lax.broadcasted_iota(jnp.int32, (cs, cs), 0)
    col = lax.broadcasted_iota(jnp.int32, (cs, cs), 1)
    tri = jnp.where(col <= row, 1.0, 0.0).astype(BF16)
    cum = carry_ref[0:1, :] + sum(jnp.dot(tri, piece, preferred_element_type=F32) for piece in _split3(lf))
    carry_ref[0:1, :] = cum[cs - 1:cs, :]
    pieces = jnp.concatenate(_split3(cum * (-LOG2E)), axis=1)
    placed = jnp.dot(pieces, place_ref[...], preferred_element_type=F32).astype(BF16)
    low = lax.broadcasted_iota(jnp.int32, (cs, LANES), 1) < HEAD_DIM
    for h in range(FOX_HEADS):
        kf_ref[h] = jnp.where(low if h % 2 == 0 else ~low, k_ref[h // 2], placed[:, h * LANES:(h + 1) * LANES])


def _fox_keys(fl, b_forget, fk):
    cs = 512
    nc = SEQ // cs
    b_row = jnp.zeros((1, LANES), F32).at[0, :FOX_HEADS].set(b_forget.astype(F32))
    return pl.pallas_call(
        functools.partial(_fox_keys_kernel, cs=cs),
        grid=(BATCH, nc),
        in_specs=[pl.BlockSpec((cs, LANES), lambda b, i: (b * nc + i, 0)),
                  pl.BlockSpec((1, LANES), lambda b, i: (0, 0)),
                  pl.BlockSpec((2, cs, LANES), lambda b, i: (0, b * nc + i, 0)),
                  pl.BlockSpec((3 * LANES, FOX_HEADS * LANES), lambda b, i: (0, 0))],
        out_specs=pl.BlockSpec((FOX_HEADS, cs, LANES), lambda b, i: (0, b * nc + i, 0)),
        out_shape=jax.ShapeDtypeStruct((FOX_HEADS, TOKENS, LANES), BF16),
        scratch_shapes=[pltpu.VMEM((8, LANES), F32)],
        compiler_params=_params("parallel", "arbitrary"),
        name="fox_keys",
    )(fl, b_row, fk, jnp.asarray(_bias_placement(), BF16))


def _flash_kernel(q_ref, k_ref, v_ref, o_ref, acc_ref, m_ref, *, tq, tk):
    i = pl.program_id(2)
    m_ref[...] = jnp.full(m_ref.shape, NEG, F32)
    acc_ref[...] = jnp.zeros(acc_ref.shape, F32)

    def step(kb, row0, diagonal):
        ks = pl.multiple_of(kb * tk, tk)
        rows = pl.ds(row0, tq - row0)
        for h in range(2):
            s = _dot_nt(q_ref[h, rows, :], k_ref[h, pl.ds(ks, tk), :])
            if diagonal:
                r = lax.broadcasted_iota(jnp.int32, s.shape, 0)
                c = lax.broadcasted_iota(jnp.int32, s.shape, 1)
                s = jnp.where(c <= r, s, NEG)
            m_old = m_ref[h, rows, :]
            m_new = jnp.maximum(m_old, jnp.max(s, axis=-1, keepdims=True))
            p = jnp.exp2(s - _lane_tile(m_new, tk)).astype(BF16)
            acc_ref[h, rows, :] = (jnp.exp2(m_old - m_new) * acc_ref[h, rows, :]
                                   + jnp.dot(p, v_ref[h, pl.ds(ks, tk), :], preferred_element_type=F32))
            m_ref[h, rows, :] = m_new

    def body(kb, carry):
        step(kb, 0, False)
        return carry

    per_q = tq // tk
    lax.fori_loop(0, i * per_q, body, 0)
    for u in range(per_q):
        step(i * per_q + u, u * tk, True)
    outs = [acc_ref[h] / pltpu.roll(acc_ref[h], HEAD_DIM, axis=1) for h in range(2)]
    low = lax.broadcasted_iota(jnp.int32, (tq, LANES), 1) < HEAD_DIM
    o_ref[0] = jnp.where(low, outs[0], outs[1]).astype(o_ref.dtype)


def _flash(q, k, v, name):
    tq, tk = 2048, 512
    nq = SEQ // tq
    kv_spec = pl.BlockSpec((2, SEQ, LANES), lambda b, j, i: (j, b, 0))
    return pl.pallas_call(
        functools.partial(_flash_kernel, tq=tq, tk=tk),
        grid=(BATCH, 2, nq),
        in_specs=[pl.BlockSpec((2, tq, LANES), lambda b, j, i: (j, b * nq + i, 0)), kv_spec, kv_spec],
        out_specs=pl.BlockSpec((1, tq, LANES), lambda b, j, i: (j, b * nq + i, 0)),
        out_shape=jax.ShapeDtypeStruct((2, TOKENS, LANES), BF16),
        scratch_shapes=[pltpu.VMEM((2, tq, LANES), F32)] * 2,
        compiler_params=_params("parallel", "parallel", "parallel"),
        name=name,
    )(q, k, v)


def _band_geometry(span):
    qi = lax.broadcasted_iota(jnp.int32, (BAND, 2 * BAND), 0)
    kj = lax.broadcasted_iota(jnp.int32, (BAND, 2 * BAND), 1)
    dist = BAND + qi - kj
    return dist.astype(F32), (dist >= 0) & (dist <= span), kj


def _band_tile(q, kk, vv, bias, valid):
    s = jnp.where(valid, _dot_nt(q, kk) + bias, NEG)
    m = jnp.max(s, axis=-1, keepdims=True)
    p = jnp.exp(s - m)
    l = jnp.sum(p, axis=-1, keepdims=True)
    o = jnp.dot(p.astype(BF16), vv, preferred_element_type=F32) / l
    return o, m + jnp.log(l)


def _swa_kernel(q_ref, k_ref, kp_ref, v_ref, vp_ref, sink_ref, o_ref, kk_ref, vv_ref, *, tb, slopes):
    n = pl.program_id(1)
    kk_ref[0:BAND] = kp_ref[0]
    kk_ref[BAND:] = k_ref[0]
    vv_ref[0:BAND] = vp_ref[0]
    vv_ref[BAND:] = v_ref[0]
    dist, in_span, kj = _band_geometry(SWA_WINDOW - 1)
    lane = lax.broadcasted_iota(jnp.int32, (BAND, LANES), 1)
    low = lane < HEAD_DIM
    for c in range(tb // BAND):
        kk = kk_ref[c * BAND:(c + 2) * BAND]
        vv = vv_ref[c * BAND:(c + 2) * BAND]
        valid = in_span & (kj >= jnp.where(n == 0, BAND, 0)) if c == 0 else in_span
        for jb in range(2):
            q = q_ref[jb, c * BAND:(c + 1) * BAND, :]
            outs = []
            for half in range(2):
                head = jb + 2 * half
                qm = jnp.where(low if half == 0 else ~low, q, jnp.zeros_like(q))
                o, lse = _band_tile(qm, kk, vv, dist * (-slopes[head]), valid)
                outs.append(o * jax.nn.sigmoid(lse - sink_ref[jb:jb + 1, :]))
            o_ref[jb, c * BAND:(c + 1) * BAND, :] = jnp.where(low, outs[0], outs[1]).astype(o_ref.dtype)


def _swa(pb, sinks):
    tb = 512
    nb = SEQ // tb
    r = tb // BAND
    s = sinks.astype(F32)
    sink_lanes = jnp.stack([jnp.concatenate([jnp.full((HEAD_DIM,), s[jb]), jnp.full((HEAD_DIM,), s[jb + 2])])
                            for jb in range(2)])
    cur = lambda blk: pl.BlockSpec((1, tb, LANES), lambda b, n: (blk, b * nb + n, 0))
    prev = lambda blk: pl.BlockSpec((1, BAND, LANES), lambda b, n: (blk, jnp.maximum((b * nb + n) * r - 1, 0), 0))
    return pl.pallas_call(
        functools.partial(_swa_kernel, tb=tb, slopes=_alibi_slopes()[:SWA_Q_HEADS]),
        grid=(BATCH, nb),
        in_specs=[pl.BlockSpec((2, tb, LANES), lambda b, n: (0, b * nb + n, 0)), cur(2), prev(2), cur(3), prev(3),
                  pl.BlockSpec((2, LANES), lambda b, n: (0, 0))],
        out_specs=pl.BlockSpec((2, tb, LANES), lambda b, n: (0, b * nb + n, 0)),
        out_shape=jax.ShapeDtypeStruct((2, TOKENS, LANES), BF16),
        scratch_shapes=[pltpu.VMEM((tb + BAND, LANES), BF16)] * 2,
        compiler_params=_params("parallel", "parallel"),
        name="swa_attention",
    )(pb, pb, pb, pb, pb, sink_lanes)


def _dil_kernel(q_ref, k_ref, kp_ref, v_ref, vp_ref, o_ref, kk_ref, vv_ref, po_ref, pl_ref, *, slopes):
    pair = pl.program_id(1)
    n = pl.program_id(2)
    kk_ref[0:DIL_BLOCK] = kp_ref[0]
    kk_ref[DIL_BLOCK:] = k_ref[0]
    vv_ref[0:DIL_BLOCK] = vp_ref[0]
    vv_ref[DIL_BLOCK:] = v_ref[0]
    lane = lax.broadcasted_iota(jnp.int32, (BAND, LANES), 1)
    low = lane < HEAD_DIM
    for p, (window, dil) in enumerate(DIL_PATTERNS):
        dist, in_span, kj = _band_geometry(window // dil)
        units = DIL_BLOCK // BAND

        def unit(u, carry, p=p, dil=dil, dist=dist, in_span=in_span, kj=kj):
            c = u // dil
            start = c * (BAND * dil) + u % dil
            if dil == 1:
                rows = pl.ds(pl.multiple_of(start, BAND), BAND)
                krows = pl.ds(pl.multiple_of(DIL_BLOCK + start - BAND, BAND), 2 * BAND)
            else:
                rows = pl.ds(start, BAND, stride=dil)
                krows = pl.ds(DIL_BLOCK + start - BAND * dil, 2 * BAND, stride=dil)
            q = q_ref[0, rows, :].astype(BF16)
            kk = kk_ref[krows, :].astype(BF16)
            vv = vv_ref[krows, :].astype(BF16)
            valid = in_span & (kj >= jnp.where((n == 0) & (c == 0), BAND, 0))
            outs, lses = [], []
            for half in range(2):
                slope = jnp.where(pair == 0, slopes[half], slopes[2 + half])
                qm = jnp.where(low if half == 0 else ~low, q, jnp.zeros_like(q))
                o, lse = _band_tile(qm, kk, vv, dist * (-slope * dil), valid)
                outs.append(o)
                lses.append(jnp.broadcast_to(lse, (BAND, LANES)))
            po_ref[p, rows, :] = jnp.where(low, outs[0], outs[1])
            pl_ref[p, rows, :] = jnp.where(low, lses[0], lses[1])
            return carry

        lax.fori_loop(0, units, unit, 0, unroll=4)
    chunk = 256

    def merge(t, carry):
        rows = pl.ds(pl.multiple_of(t * chunk, chunk), chunk)
        lse = pl_ref[:, rows, :]
        w = jnp.exp(lse - jnp.max(lse, axis=0, keepdims=True))
        o_ref[0, rows, :] = (jnp.sum(w * po_ref[:, rows, :], axis=0) / jnp.sum(w, axis=0)).astype(o_ref.dtype)
        return carry

    lax.fori_loop(0, DIL_BLOCK // chunk, merge, 0)


def _dilated(pd):
    nb = SEQ // DIL_BLOCK
    cur = lambda off: pl.BlockSpec((1, DIL_BLOCK, LANES), lambda b, j, n: (off + j, b * nb + n, 0))
    prev = lambda off: pl.BlockSpec((1, DIL_BLOCK, LANES),
                                    lambda b, j, n: (off + j, b * nb + jnp.maximum(n - 1, 0), 0))
    return pl.pallas_call(
        functools.partial(_dil_kernel, slopes=_alibi_slopes()[SWA_Q_HEADS:]),
        grid=(BATCH, 2, nb),
        in_specs=[cur(0), cur(2), prev(2), cur(4), prev(4)],
        out_specs=pl.BlockSpec((1, DIL_BLOCK, LANES), lambda b, j, n: (j, b * nb + n, 0)),
        out_shape=jax.ShapeDtypeStruct((2, TOKENS, LANES), BF16),
        scratch_shapes=[pltpu.VMEM((2 * DIL_BLOCK, LANES), F32)] * 2
                       + [pltpu.VMEM((len(DIL_PATTERNS), DIL_BLOCK, LANES), F32)] * 2,
        compiler_params=_params("parallel", "parallel", "parallel"),
        name="dilated_attention",
    )(pd, pd, pd, pd, pd)


def _out_kernel(x_ref, ya_ref, yb_ref, yc_ref, yd_ref, w_ref, o_ref):
    mixed = jnp.concatenate([y[j] for y in (ya_ref, yb_ref, yc_ref, yd_ref) for j in range(2)], axis=1)
    o_ref[...] = x_ref[...] + jnp.dot(mixed, w_ref[...], preferred_element_type=F32)


def _out_proj(x, ys, w):
    tm = 512
    y_spec = pl.BlockSpec((2, tm, LANES), lambda i: (0, i, 0))
    return pl.pallas_call(
        _out_kernel,
        grid=(TOKENS // tm,),
        in_specs=[pl.BlockSpec((tm, D_MODEL), lambda i: (i, 0))] + [y_spec] * 4
                 + [pl.BlockSpec((D_MODEL, D_MODEL), lambda i: (0, 0))],
        out_specs=pl.BlockSpec((tm, D_MODEL), lambda i: (i, 0)),
        out_shape=jax.ShapeDtypeStruct((TOKENS, D_MODEL), F32),
        compiler_params=_params("parallel"),
        name="out_proj",
    )(x, *ys, w)


def _swiglu(h, wg, wu, wd):
    gate = jnp.dot(h, wg, preferred_element_type=F32)
    up = jnp.dot(h, wu, preferred_element_type=F32)
    act = (gate * jax.nn.sigmoid(gate) * up).astype(BF16)
    return jnp.dot(act, wd, preferred_element_type=F32)


def _ffn_kernel(x_ref, g_ref, wg_ref, wu_ref, wd_ref, o_ref, h_ref, acc_ref):
    f = pl.program_id(1)

    @pl.when(f == 0)
    def _():
        x = x_ref[...]
        h_ref[...] = _rms(x, g_ref[...]).astype(BF16)
        acc_ref[...] = x

    acc_ref[...] += _swiglu(h_ref[...], wg_ref[...], wu_ref[...], wd_ref[...])

    @pl.when(f == pl.num_programs(1) - 1)
    def _():
        o_ref[...] = acc_ref[...]


def _ffn(x, g, wg, wu, wd):
    tm, tf = 1024, 512
    return pl.pallas_call(
        _ffn_kernel,
        grid=(TOKENS // tm, FFN_DIM // tf),
        in_specs=[pl.BlockSpec((tm, D_MODEL), lambda i, f: (i, 0)),
                  pl.BlockSpec((1, D_MODEL), lambda i, f: (0, 0)),
                  pl.BlockSpec((D_MODEL, tf), lambda i, f: (0, f)),
                  pl.BlockSpec((D_MODEL, tf), lambda i, f: (0, f)),
                  pl.BlockSpec((tf, D_MODEL), lambda i, f: (f, 0))],
        out_specs=pl.BlockSpec((tm, D_MODEL), lambda i, f: (i, 0)),
        out_shape=jax.ShapeDtypeStruct((TOKENS, D_MODEL), F32),
        scratch_shapes=[pltpu.VMEM((tm, D_MODEL), BF16), pltpu.VMEM((tm, D_MODEL), F32)],
        compiler_params=_params("parallel", "arbitrary"),
        name="dense_ffn",
    )(x, g, wg, wu, wd)


MOE_TM = 1024
MOE_STATIC_ROWS = 320
MOE_CHUNK = 64


def _route_kernel(x_ref, g_ref, router_ref, h_ref, comb_ref, pc_ref, pt_ref, cnt_ref):
    tm = x_ref.shape[0]
    h = _rms(x_ref[...], g_ref[...])
    h_ref[...] = h.astype(BF16)
    lane = lax.broadcasted_iota(jnp.int32, (tm, LANES), 1).astype(F32)
    logits = jnp.dot(h, router_ref[...], preferred_element_type=F32, precision=lax.Precision.HIGHEST)
    logits = jnp.where(lane < N_EXPERTS, logits, NEG)
    m1 = jnp.max(logits, axis=-1, keepdims=True)
    i1 = jnp.min(jnp.where(logits == m1, lane, float(LANES)), axis=-1, keepdims=True)
    rest = jnp.where(lane == i1, NEG, logits)
    m2 = jnp.max(rest, axis=-1, keepdims=True)
    i2 = jnp.min(jnp.where(rest == m2, lane, float(LANES)), axis=-1, keepdims=True)
    t = jnp.exp(m2 - m1)
    comb_ref[...] = jnp.where(lane == i1, 1.0 / (1.0 + t), 0.0) + jnp.where(lane == i2, t / (1.0 + t), 0.0)
    sel = jnp.where(lane == i1, 1.0, jnp.where(lane == i2, 1.0, 0.0))
    selb = sel.astype(BF16)
    r = lax.broadcasted_iota(jnp.int32, (tm, tm), 0)
    c = lax.broadcasted_iota(jnp.int32, (tm, tm), 1)
    rank = jnp.dot(jnp.where(c < r, 1.0, 0.0).astype(BF16), selb, preferred_element_type=F32)
    pc_ref[...] = jnp.where(sel > 0.0, rank, -1.0)
    eye = jnp.where(lax.broadcasted_iota(jnp.int32, (8, LANES), 0) == lax.broadcasted_iota(jnp.int32, (8, LANES), 1),
                    1.0, 0.0).astype(BF16)
    sel_t = _dot_nt(eye, selb)
    rank_t = jnp.dot(sel_t.astype(BF16), jnp.where(r < c, 1.0, 0.0).astype(BF16), preferred_element_type=F32)
    pt_ref[...] = jnp.where(sel_t > 0.0, rank_t, -1.0)
    cnt_ref[...] = jnp.broadcast_to(jnp.sum(sel, axis=0, keepdims=True), (8, LANES))


def _route(x, g, router_pad):
    tm = MOE_TM
    nt = TOKENS // tm
    return pl.pallas_call(
        _route_kernel,
        grid=(nt,),
        in_specs=[pl.BlockSpec((tm, D_MODEL), lambda i: (i, 0)), pl.BlockSpec((1, D_MODEL), lambda i: (0, 0)),
                  pl.BlockSpec((D_MODEL, LANES), lambda i: (0, 0))],
        out_specs=[pl.BlockSpec((tm, D_MODEL), lambda i: (i, 0)), pl.BlockSpec((tm, LANES), lambda i: (i, 0)),
                   pl.BlockSpec((tm, LANES), lambda i: (i, 0)), pl.BlockSpec((8, tm), lambda i: (0, i)),
                   pl.BlockSpec((8, LANES), lambda i: (i, 0))],
        out_shape=[jax.ShapeDtypeStruct((TOKENS, D_MODEL), BF16), jax.ShapeDtypeStruct((TOKENS, LANES), F32),
                   jax.ShapeDtypeStruct((TOKENS, LANES), F32), jax.ShapeDtypeStruct((8, TOKENS), F32),
                   jax.ShapeDtypeStruct((nt * 8, LANES), F32)],
        compiler_params=_params("parallel"),
        name="moe_route",
    )(x, g, router_pad)


def _moe_kernel(cnt_ref, h_ref, pt_ref, pc_ref, comb_ref, wg_ref, wu_ref, wd_ref, o_ref, xg_ref, yacc_ref, acc_ref):
    tm = h_ref.shape[0]
    sm, oc = MOE_STATIC_ROWS, MOE_CHUNK
    i = pl.program_id(0)
    e = pl.program_id(1)
    f = pl.program_id(2)
    n_over = jnp.maximum(cnt_ref[i * N_EXPERTS + e] - sm + oc - 1, 0) // oc

    def overflow(body):
        def step(c, carry):
            body(pl.multiple_of(sm + c * oc, oc), oc)
            return carry
        lax.fori_loop(0, n_over, step, 0)

    @pl.when((e == 0) & (f == 0))
    def _():
        acc_ref[...] = jnp.zeros(acc_ref.shape, F32)

    @pl.when(f == 0)
    def _():
        pt = pt_ref[pl.ds(e, 1), :]

        def gather(row0, nrows):
            ridx = lax.broadcasted_iota(jnp.int32, (nrows, tm), 0) + row0
            onehot = jnp.where(ridx.astype(F32) == pt, 1.0, 0.0).astype(BF16)
            xg_ref[pl.ds(row0, nrows), :] = jnp.dot(onehot, h_ref[...], preferred_element_type=F32).astype(BF16)
            yacc_ref[pl.ds(row0, nrows), :] = jnp.zeros((nrows, D_MODEL), F32)

        gather(0, sm)
        overflow(gather)

    def ffn(row0, nrows):
        rows = pl.ds(row0, nrows)
        yacc_ref[rows, :] += _swiglu(xg_ref[rows, :], wg_ref[0], wu_ref[0], wd_ref[0])

    ffn(0, sm)
    overflow(ffn)

    @pl.when(f == pl.num_programs(2) - 1)
    def _():
        lane = lax.broadcasted_iota(jnp.int32, (tm, LANES), 1)
        mine = lane == e
        pc = jnp.sum(jnp.where(mine, pc_ref[...], 0.0), axis=-1, keepdims=True)
        gate = jnp.sum(jnp.where(mine, comb_ref[...], 0.0), axis=-1, keepdims=True)

        def scatter(row0, nrows):
            cidx = lax.broadcasted_iota(jnp.int32, (tm, nrows), 1) + row0
            onehot = jnp.where(cidx.astype(F32) == pc, 1.0, 0.0).astype(BF16)
            y = yacc_ref[pl.ds(row0, nrows), :].astype(BF16)
            acc_ref[...] += jnp.dot(onehot, y, preferred_element_type=F32) * gate

        scatter(0, sm)
        overflow(scatter)

    @pl.when((e == pl.num_programs(1) - 1) & (f == pl.num_programs(2) - 1))
    def _():
        o_ref[...] = acc_ref[...].astype(o_ref.dtype)


def _moe(h, pt, pc, comb, counts, wg, wu, wd):
    tm, tf = MOE_TM, FFN_DIM // 2
    grid_spec = pltpu.PrefetchScalarGridSpec(
        num_scalar_prefetch=1,
        grid=(TOKENS // tm, N_EXPERTS, FFN_DIM // tf),
        in_specs=[pl.BlockSpec((tm, D_MODEL), lambda i, e, f, cnt: (i, 0)),
                  pl.BlockSpec((8, tm), lambda i, e, f, cnt: (0, i)),
                  pl.BlockSpec((tm, LANES), lambda i, e, f, cnt: (i, 0)),
                  pl.BlockSpec((tm, LANES), lambda i, e, f, cnt: (i, 0)),
                  pl.BlockSpec((1, D_MODEL, tf), lambda i, e, f, cnt: (e, 0, f)),
                  pl.BlockSpec((1, D_MODEL, tf), lambda i, e, f, cnt: (e, 0, f)),
                  pl.BlockSpec((1, tf, D_MODEL), lambda i, e, f, cnt: (e, f, 0))],
        out_specs=pl.BlockSpec((tm, D_MODEL), lambda i, e, f, cnt: (i, 0)),
        scratch_shapes=[pltpu.VMEM((tm, D_MODEL), BF16), pltpu.VMEM((tm, D_MODEL), F32),
                        pltpu.VMEM((tm, D_MODEL), F32)])
    return pl.pallas_call(
        _moe_kernel,
        grid_spec=grid_spec,
        out_shape=jax.ShapeDtypeStruct((TOKENS, D_MODEL), BF16),
        compiler_params=_params("parallel", "arbitrary", "arbitrary"),
        name="moe_ffn",
    )(counts, h, pt, pc, comb, wg, wu, wd)


def _final_kernel(x_ref, y_ref, g_ref, o_ref):
    o_ref[...] = _rms(x_ref[...] + y_ref[...].astype(F32), g_ref[...])


def _final(x, y, g):
    tm = 1024
    return pl.pallas_call(
        _final_kernel,
        grid=(TOKENS // tm,),
        in_specs=[pl.BlockSpec((tm, D_MODEL), lambda i: (i, 0)), pl.BlockSpec((tm, D_MODEL), lambda i: (i, 0)),
                  pl.BlockSpec((1, D_MODEL), lambda i: (0, 0))],
        out_specs=pl.BlockSpec((tm, D_MODEL), lambda i: (i, 0)),
        out_shape=jax.ShapeDtypeStruct((TOKENS, D_MODEL), F32),
        compiler_params=_params("parallel"),
        name="final_norm",
    )(x, y, g)


def kernel(x, positions, attn_norm, w_in, b_forget, mla_q_norm, w_q_up, mla_kv_norm, w_kv_up, sinks, w_out, ffn_norm, dense_w_gate, dense_w_up, dense_w_down, router, moe_w_gate, moe_w_up, moe_w_down, final_norm):
    assert x.shape == (BATCH, SEQ, D_MODEL) and positions.shape == (BATCH, SEQ)
    p_idx, p_sgn = _proj_columns()
    q_idx, q_sgn = _mla_q_columns()
    kv_idx, kv_sgn = _mla_kv_columns()
    mix_rows = _mix_rows()
    cos, sin = _rope_tables(positions)
    xt = x.reshape(TOKENS, D_MODEL).astype(F32)
    for layer in range(DEPTH):
        w = (w_in[layer][:, p_idx] * p_sgn).astype(BF16)
        wq = (w_q_up[layer][:, q_idx] * q_sgn).astype(BF16)
        wkv = (w_kv_up[layer][:, kv_idx] * kv_sgn).astype(BF16)
        fq, fk, fv, fl, pb, qc, kc, vc, pd = _project(
            xt, attn_norm[layer].reshape(1, D_MODEL), w, mla_q_norm[layer].reshape(1, MLA_Q_LORA), wq,
            mla_kv_norm[layer].reshape(1, MLA_KV_LORA), wkv, cos, sin)
        y_a = _flash(fq, _fox_keys(fl, b_forget[layer], fk), fv, "fox_attention")
        y_b = _swa(pb, sinks[layer])
        y_c = _flash(qc, kc, vc, "mla_attention")
        y_d = _dilated(pd)
        xt = _out_proj(xt, (y_a, y_b, y_c, y_d), w_out[layer][mix_rows].astype(BF16))
        j = layer // 2
        g = ffn_norm[layer].reshape(1, D_MODEL)
        if layer % 2 == 0:
            xt = _ffn(xt, g, dense_w_gate[j].astype(BF16), dense_w_up[j].astype(BF16), dense_w_down[j].astype(BF16))
        else:
            assert layer == DEPTH - 1
            router_pad = jnp.zeros((D_MODEL, LANES), F32).at[:, :N_EXPERTS].set(router[j])
            h, comb, pc, pt, cnt = _route(xt, g, router_pad)
            counts = cnt[::8, :N_EXPERTS].astype(jnp.int32).reshape(-1)
            y = _moe(h, pt, pc, comb, counts, moe_w_gate[j].astype(BF16), moe_w_up[j].astype(BF16),
                     moe_w_down[j].astype(BF16))
            xt = _final(xt, y, final_norm.reshape(1, D_MODEL))
    return xt.reshape(BATCH, SEQ, D_MODEL)
```

```python
import functools

import numpy as np
import jax
import jax.numpy as jnp
from jax import lax
from jax.experimental import pallas as pl
from jax.experimental.pallas import tpu as pltpu

D_MODEL = 1024
BATCH = 2
SEQ = 8192
DEPTH = 2
TOKENS = BATCH * SEQ
HEAD_DIM = 64
BAND = 128
NORM_EPS = 1e-6
FOX_HEADS = 4
SWA_Q_HEADS = 4
SWA_KV_HEADS = 2
SWA_WINDOW = 128
MLA_HEADS = 4
MLA_Q_LORA = 256
MLA_KV_LORA = 128
MLA_NOPE_DIM = 64
MLA_ROPE_DIM = 32
MLA_V_DIM = 64
ROPE_THETA = 10000.0
DIL_HEADS = 4
DIL_PATTERNS = ((128, 1), (512, 4), (2048, 16))
DIL_BLOCK = BAND * max(d for _, d in DIL_PATTERNS)
FFN_DIM = 3584
N_EXPERTS = 8
LANES = 128
NEG = -1e30
VMEM_LIMIT = 56 * 1024 * 1024

_OFF = np.cumsum([0, 256, 256, 256, 4, 256, 128, 128, 256, 128, 32, 256, 256, 256])
(_A_Q, _A_K, _A_V, _A_F, _B_Q, _B_K, _B_V, _C_Q, _C_KV, _C_KR, _D_Q, _D_K, _D_V) = _OFF[:13].tolist()
N_PROJ_BLOCKS = 22
LOG2E = 1.4426950408889634

BF16 = jnp.bfloat16
F32 = jnp.float32


def _alibi_slopes():
    n = SWA_Q_HEADS + DIL_HEADS
    return [2.0 ** (-8.0 * i / n) for i in range(1, n + 1)]


def _proj_columns():
    idx = np.zeros((N_PROJ_BLOCKS * LANES,), np.int32)
    sgn = np.zeros((N_PROJ_BLOCKS * LANES,), np.float32)

    def put(dst, src, n, sign=1.0):
        idx[dst:dst + n] = np.arange(src, src + n)
        sgn[dst:dst + n] = sign

    put(0, _A_Q, 256); put(256, _A_K, 256); put(512, _A_V, 256)
    for blk, heads in ((6, (0, 2)), (7, (1, 3))):
        for half, h in enumerate(heads):
            put(blk * LANES + half * HEAD_DIM, _B_Q + h * HEAD_DIM, HEAD_DIM)
    put(8 * LANES, _B_K, 128); put(9 * LANES, _B_V, 128)
    put(10 * LANES, _C_Q, 256); put(12 * LANES, _C_KV, 128)
    half = MLA_ROPE_DIM // 2
    put(13 * LANES + MLA_NOPE_DIM, _C_KR, MLA_ROPE_DIM)
    put(14 * LANES + MLA_NOPE_DIM, _C_KR + half, half, -1.0)
    put(14 * LANES + MLA_NOPE_DIM + half, _C_KR, half)
    put(15 * LANES, _D_Q, 256); put(17 * LANES, _D_K, 256); put(19 * LANES, _D_V, 256)
    put(21 * LANES, _A_F, FOX_HEADS)
    return idx, sgn


def _bias_lane0(h):
    return HEAD_DIM if h % 2 == 0 else 0


def _bias_placement():
    place = np.zeros((3 * LANES, FOX_HEADS * LANES), np.float32)
    for piece in range(3):
        for h in range(FOX_HEADS):
            place[piece * LANES + h, h * LANES + _bias_lane0(h) + piece] = 1.0
    return place


def _mla_q_columns():
    idx = np.zeros((8 * LANES,), np.int32)
    sgn = np.zeros((8 * LANES,), np.float32)
    half = MLA_ROPE_DIM // 2
    dq = MLA_NOPE_DIM + MLA_ROPE_DIM
    for h in range(MLA_HEADS):
        a = h * LANES
        idx[a:a + dq] = np.arange(h * dq, (h + 1) * dq); sgn[a:a + dq] = 1.0
        b = (MLA_HEADS + h) * LANES + MLA_NOPE_DIM
        r = h * dq + MLA_NOPE_DIM
        idx[b:b + half] = np.arange(r + half, r + 2 * half); sgn[b:b + half] = -1.0
        idx[b + half:b + 2 * half] = np.arange(r, r + half); sgn[b + half:b + 2 * half] = 1.0
    return idx, sgn


def _mla_kv_columns():
    idx = np.zeros((6 * LANES,), np.int32)
    sgn = np.zeros((6 * LANES,), np.float32)
    dkv = MLA_NOPE_DIM + MLA_V_DIM
    for h in range(MLA_HEADS):
        idx[h * LANES:h * LANES + MLA_NOPE_DIM] = np.arange(h * dkv, h * dkv + MLA_NOPE_DIM)
        sgn[h * LANES:h * LANES + MLA_NOPE_DIM] = 1.0
        b = MLA_HEADS * LANES + h * MLA_V_DIM
        idx[b:b + MLA_V_DIM] = np.arange(h * dkv + MLA_NOPE_DIM, (h + 1) * dkv)
        sgn[b:b + MLA_V_DIM] = 1.0
    return idx, sgn


def _take_columns(w, idx, sgn):
    parts = []
    a = 0
    while a < len(idx):
        b = a + 1
        while b < len(idx) and sgn[b] == sgn[a] and (sgn[a] == 0 or idx[b] == idx[b - 1] + 1):
            b += 1
        if sgn[a] == 0:
            parts.append(jnp.zeros((w.shape[0], b - a), w.dtype))
        else:
            piece = w[:, int(idx[a]):int(idx[a]) + (b - a)]
            parts.append(piece if sgn[a] > 0 else -piece)
        a = b
    return jnp.concatenate(parts, axis=1)


def _mix_rows():
    rows = np.arange(4 * 256)
    b = 256
    perm = np.concatenate([np.arange(b + h * HEAD_DIM, b + (h + 1) * HEAD_DIM) for h in (0, 2, 1, 3)])
    rows[b:b + 256] = perm
    return rows


def _rms(x, g):
    return x * lax.rsqrt(jnp.mean(x * x, axis=-1, keepdims=True) + NORM_EPS) * g


def _dot_nt(a, b):
    return lax.dot_general(a, b, (((1,), (1,)), ((), ())), preferred_element_type=F32)


def _lane_tile(x, width):
    return x if width == LANES else jnp.concatenate([x] * (width // LANES), axis=1)


def _params(*sem):
    return pltpu.CompilerParams(dimension_semantics=sem, vmem_limit_bytes=VMEM_LIMIT)


def _rope_table_kernel(pos_ref, invf_ref, cos_ref, sin_ref):
    ang = pos_ref[...].astype(F32) * invf_ref[...]
    cos_ref[...] = jnp.cos(ang)
    sin_ref[...] = jnp.sin(ang)


def _rope_tables(positions):
    tm = 2048
    half = MLA_ROPE_DIM // 2
    invf = np.zeros((1, LANES), np.float32)
    f = (ROPE_THETA ** (-np.arange(half, dtype=np.float32) / np.float32(half))).astype(np.float32)
    invf[0, MLA_NOPE_DIM:MLA_NOPE_DIM + half] = f
    invf[0, MLA_NOPE_DIM + half:MLA_NOPE_DIM + 2 * half] = f
    return pl.pallas_call(
        _rope_table_kernel,
        grid=(TOKENS // tm,),
        in_specs=[pl.BlockSpec((tm, 1), lambda i: (i, 0)), pl.BlockSpec((1, LANES), lambda i: (0, 0))],
        out_specs=[pl.BlockSpec((tm, LANES), lambda i: (i, 0))] * 2,
        out_shape=[jax.ShapeDtypeStruct((TOKENS, LANES), F32)] * 2,
        compiler_params=_params("parallel"),
        name="rope_tables",
    )(positions.reshape(TOKENS, 1), jnp.asarray(invf))


def _proj_kernel(x_ref, g_ref, w_ref, qn_ref, wq_ref, kvn_ref, wkv_ref, cos_ref, sin_ref,
                 fq_ref, fk_ref, fv_ref, fl_ref, pb_ref, qc_ref, kc_ref, vc_ref, pd_ref):
    hb = _rms(x_ref[...], g_ref[...]).astype(BF16)
    res = jnp.dot(hb, w_ref[...], preferred_element_type=F32)
    lane = lax.broadcasted_iota(jnp.int32, (x_ref.shape[0], LANES), 1)
    low = lane < HEAD_DIM

    def blk(j, n=1):
        return res[:, j * LANES:(j + n) * LANES]

    def own(h):
        return low if h % 2 == 0 else ~low

    qscale = HEAD_DIM ** -0.5
    for h in range(FOX_HEADS):
        ones = (lane >= _bias_lane0(h)) & (lane < _bias_lane0(h) + 3)
        fq_ref[h] = jnp.where(own(h), blk(h // 2) * (qscale * LOG2E), jnp.where(ones, 1.0, 0.0)).astype(BF16)
        fv_ref[h] = jnp.where(own(h), blk(4 + h // 2), 1.0).astype(BF16)
    for j in range(2):
        fk_ref[j] = blk(2 + j).astype(BF16)
    fl_ref[...] = blk(21)
    for j in range(2):
        pb_ref[j] = (blk(6 + j) * qscale).astype(BF16)
    pb_ref[2] = blk(8).astype(BF16)
    pb_ref[3] = blk(9).astype(BF16)

    cos = cos_ref[...]
    sin = sin_ref[...]
    cq = _rms(blk(10, 2), qn_ref[...]).astype(BF16)
    qab = jnp.dot(cq, wq_ref[...], preferred_element_type=F32)
    mla_scale = (MLA_NOPE_DIM + MLA_ROPE_DIM) ** -0.5
    for h in range(MLA_HEADS):
        qa = qab[:, h * LANES:(h + 1) * LANES]
        qb = qab[:, (MLA_HEADS + h) * LANES:(MLA_HEADS + h + 1) * LANES]
        qc_ref[h] = ((qa * cos + qb * sin) * (mla_scale * LOG2E)).astype(BF16)
    ckv = _rms(blk(12), kvn_ref[...]).astype(BF16)
    kv = jnp.dot(ckv, wkv_ref[...], preferred_element_type=F32)
    k_rot = blk(13) * cos + blk(14) * sin
    for h in range(MLA_HEADS):
        kc_ref[h] = (kv[:, h * LANES:(h + 1) * LANES] + k_rot).astype(BF16)
        v_pair = kv[:, (MLA_HEADS + h // 2) * LANES:(MLA_HEADS + h // 2 + 1) * LANES]
        vc_ref[h] = jnp.where(own(h), v_pair, 1.0).astype(BF16)

    for j in range(2):
        pd_ref[j] = blk(15 + j) * qscale
    for j in range(2, 6):
        pd_ref[j] = blk(15 + j)


def _project(x, g, w, qn, wq, kvn, wkv, cos, sin):
    tm = 512
    full = lambda shape: pl.BlockSpec(shape, lambda i: (0,) * len(shape))
    out_blk = lambda n: pl.BlockSpec((n, tm, LANES), lambda i: (0, i, 0))
    out_sds = lambda n, dt: jax.ShapeDtypeStruct((n, TOKENS, LANES), dt)
    tok_blk = pl.BlockSpec((tm, LANES), lambda i: (i, 0))
    return pl.pallas_call(
        _proj_kernel,
        grid=(TOKENS // tm,),
        in_specs=[pl.BlockSpec((tm, D_MODEL), lambda i: (i, 0)), full((1, D_MODEL)),
                  full((D_MODEL, N_PROJ_BLOCKS * LANES)),
                  full((1, MLA_Q_LORA)), full((MLA_Q_LORA, 8 * LANES)),
                  full((1, MLA_KV_LORA)), full((MLA_KV_LORA, 6 * LANES)), tok_blk, tok_blk],
        out_specs=[out_blk(4), out_blk(2), out_blk(4), tok_blk, out_blk(4), out_blk(4), out_blk(4), out_blk(4),
                   out_blk(6)],
        out_shape=[out_sds(4, BF16), out_sds(2, BF16), out_sds(4, BF16), jax.ShapeDtypeStruct((TOKENS, LANES), F32),
                   out_sds(4, BF16), out_sds(4, BF16), out_sds(4, BF16), out_sds(4, BF16), out_sds(6, F32)],
        compiler_params=_params("parallel"),
        name="in_proj",
    )(x, g, w, qn, wq, kvn, wkv, cos, sin)


def _split3(x):
    hi = x.astype(BF16)
    r1 = x - hi.astype(F32)
    mid = r1.astype(BF16)
    return hi, mid, (r1 - mid.astype(F32)).astype(BF16)


def _fox_keys_kernel(fl_ref, b_ref, k_ref, place_ref, kf_ref, carry_ref, *, cs):
    @pl.when(pl.program_id(1) == 0)
    def _():
        carry_ref[...] = jnp.zeros_like(carry_ref)

    lf = jax.nn.log_sigmoid(fl_ref[...] + b_ref[...])
    row = lax.broadcasted_iota(jnp.int32, (cs, cs), 0)
    col = lax.broadcasted_iota(jnp.int32, (cs, cs), 1)
    tri = jnp.where(col <= row, 1.0, 0.0).astype(BF16)
    cum = carry_ref[0:1, :] + sum(jnp.dot(tri, piece, preferred_element_type=F32) for piece in _split3(lf))
    carry_ref[0:1, :] = cum[cs - 1:cs, :]
    pieces = jnp.concatenate(_split3(cum * (-LOG2E)), axis=1)
    placed = jnp.dot(pieces, place_ref[...], preferred_element_type=F32).astype(BF16)
    low = lax.broadcasted_iota(jnp.int32, (cs, LANES), 1) < HEAD_DIM
    for h in range(FOX_HEADS):
        kf_ref[h] = jnp.where(low if h % 2 == 0 else ~low, k_ref[h // 2], placed[:, h * LANES:(h + 1) * LANES])


def _fox_keys(fl, b_forget, fk):
    cs = 512
    nc = SEQ // cs
    b_row = jnp.zeros((1, LANES), F32).at[0, :FOX_HEADS].set(b_forget.astype(F32))
    return pl.pallas_call(
        functools.partial(_fox_keys_kernel, cs=cs),
        grid=(BATCH, nc),
        in_specs=[pl.BlockSpec((cs, LANES), lambda b, i: (b * nc + i, 0)),
                  pl.BlockSpec((1, LANES), lambda b, i: (0, 0)),
                  pl.BlockSpec((2, cs, LANES), lambda b, i: (0, b * nc + i, 0)),
                  pl.BlockSpec((3 * LANES, FOX_HEADS * LANES), lambda b, i: (0, 0))],
        out_specs=pl.BlockSpec((FOX_HEADS, cs, LANES), lambda b, i: (0, b * nc + i, 0)),
        out_shape=jax.ShapeDtypeStruct((FOX_HEADS, TOKENS, LANES), BF16),
        scratch_shapes=[pltpu.VMEM((8, LANES), F32)],
        compiler_params=_params("parallel", "arbitrary"),
        name="fox_keys",
    )(fl, b_row, fk, jnp.asarray(_bias_placement(), BF16))


def _flash_kernel(q_ref, k_ref, v_ref, o_ref, acc_ref, m_ref, *, tq, tk):
    i = pl.program_id(2)
    m_ref[...] = jnp.full(m_ref.shape, NEG, F32)
    acc_ref[...] = jnp.zeros(acc_ref.shape, F32)

    def step(kb, row0, diagonal):
        ks = pl.multiple_of(kb * tk, tk)
        rows = pl.ds(row0, tq - row0)
        for h in range(2):
            s = _dot_nt(q_ref[h, rows, :], k_ref[h, pl.ds(ks, tk), :])
            if diagonal:
                r = lax.broadcasted_iota(jnp.int32, s.shape, 0)
                c = lax.broadcasted_iota(jnp.int32, s.shape, 1)
                s = jnp.where(c <= r, s, NEG)
            m_old = m_ref[h, rows, :]
            m_new = jnp.maximum(m_old, jnp.max(s, axis=-1, keepdims=True))
            p = jnp.exp2(s - _lane_tile(m_new, tk)).astype(BF16)
            acc_ref[h, rows, :] = (jnp.exp2(m_old - m_new) * acc_ref[h, rows, :]
                                   + jnp.dot(p, v_ref[h, pl.ds(ks, tk), :], preferred_element_type=F32))
            m_ref[h, rows, :] = m_new

    def body(kb, carry):
        step(kb, 0, False)
        return carry

    per_q = tq // tk
    lax.fori_loop(0, i * per_q, body, 0)
    for u in range(per_q):
        step(i * per_q + u, u * tk, True)
    outs = [acc_ref[h] / pltpu.roll(acc_ref[h], HEAD_DIM, axis=1) for h in range(2)]
    low = lax.broadcasted_iota(jnp.int32, (tq, LANES), 1) < HEAD_DIM
    o_ref[0] = jnp.where(low, outs[0], outs[1]).astype(o_ref.dtype)


def _flash(q, k, v, name):
    tq, tk = 2048, 512
    nq = SEQ // tq
    kv_spec = pl.BlockSpec((2, SEQ, LANES), lambda b, j, i: (j, b, 0))
    return pl.pallas_call(
        functools.partial(_flash_kernel, tq=tq, tk=tk),
        grid=(BATCH, 2, nq),
        in_specs=[pl.BlockSpec((2, tq, LANES), lambda b, j, i: (j, b * nq + i, 0)), kv_spec, kv_spec],
        out_specs=pl.BlockSpec((1, tq, LANES), lambda b, j, i: (j, b * nq + i, 0)),
        out_shape=jax.ShapeDtypeStruct((2, TOKENS, LANES), BF16),
        scratch_shapes=[pltpu.VMEM((2, tq, LANES), F32)] * 2,
        compiler_params=_params("parallel", "parallel", "parallel"),
        name=name,
    )(q, k, v)


def _band_geometry(span):
    qi = lax.broadcasted_iota(jnp.int32, (BAND, 2 * BAND), 0)
    kj = lax.broadcasted_iota(jnp.int32, (BAND, 2 * BAND), 1)
    dist = BAND + qi - kj
    return dist.astype(F32), (dist >= 0) & (dist <= span), kj


def _band_tile(q, kk, vv, bias, valid):
    s = jnp.where(valid, _dot_nt(q, kk) + bias, NEG)
    m = jnp.max(s, axis=-1, keepdims=True)
    p = jnp.exp(s - m)
    l = jnp.sum(p, axis=-1, keepdims=True)
    o = jnp.dot(p.astype(BF16), vv, preferred_element_type=F32) / l
    return o, m + jnp.log(l)


def _swa_kernel(q_ref, k_ref, kp_ref, v_ref, vp_ref, sink_ref, o_ref, kk_ref, vv_ref, *, tb, slopes):
    n = pl.program_id(1)
    kk_ref[0:BAND] = kp_ref[0]
    kk_ref[BAND:] = k_ref[0]
    vv_ref[0:BAND] = vp_ref[0]
    vv_ref[BAND:] = v_ref[0]
    dist, in_span, kj = _band_geometry(SWA_WINDOW - 1)
    lane = lax.broadcasted_iota(jnp.int32, (BAND, LANES), 1)
    low = lane < HEAD_DIM
    for c in range(tb // BAND):
        kk = kk_ref[c * BAND:(c + 2) * BAND]
        vv = vv_ref[c * BAND:(c + 2) * BAND]
        valid = in_span & (kj >= jnp.where(n == 0, BAND, 0)) if c == 0 else in_span
        for jb in range(2):
            q = q_ref[jb, c * BAND:(c + 1) * BAND, :]
            outs = []
            for half in range(2):
                head = jb + 2 * half
                qm = jnp.where(low if half == 0 else ~low, q, jnp.zeros_like(q))
                o, lse = _band_tile(qm, kk, vv, dist * (-slopes[head]), valid)
                outs.append(o * jax.nn.sigmoid(lse - sink_ref[jb:jb + 1, :]))
            o_ref[jb, c * BAND:(c + 1) * BAND, :] = jnp.where(low, outs[0], outs[1]).astype(o_ref.dtype)


def _swa(pb, sinks):
    tb = 512
    nb = SEQ // tb
    r = tb // BAND
    s = sinks.astype(F32)
    sink_lanes = jnp.stack([jnp.concatenate([jnp.full((HEAD_DIM,), s[jb]), jnp.full((HEAD_DIM,), s[jb + 2])])
                            for jb in range(2)])
    cur = lambda blk: pl.BlockSpec((1, tb, LANES), lambda b, n: (blk, b * nb + n, 0))
    prev = lambda blk: pl.BlockSpec((1, BAND, LANES), lambda b, n: (blk, jnp.maximum((b * nb + n) * r - 1, 0), 0))
    return pl.pallas_call(
        functools.partial(_swa_kernel, tb=tb, slopes=_alibi_slopes()[:SWA_Q_HEADS]),
        grid=(BATCH, nb),
        in_specs=[pl.BlockSpec((2, tb, LANES), lambda b, n: (0, b * nb + n, 0)), cur(2), prev(2), cur(3), prev(3),
                  pl.BlockSpec((2, LANES), lambda b, n: (0, 0))],
        out_specs=pl.BlockSpec((2, tb, LANES), lambda b, n: (0, b * nb + n, 0)),
        out_shape=jax.ShapeDtypeStruct((2, TOKENS, LANES), BF16),
        scratch_shapes=[pltpu.VMEM((tb + BAND, LANES), BF16)] * 2,
        compiler_params=_params("parallel", "parallel"),
        name="swa_attention",
    )(pb, pb, pb, pb, pb, sink_lanes)


def _dil_kernel(q_ref, k_ref, kp_ref, v_ref, vp_ref, o_ref, kk_ref, vv_ref, po_ref, pl_ref, *, slopes):
    pair = pl.program_id(1)
    n = pl.program_id(2)
    kk_ref[0:DIL_BLOCK] = kp_ref[0]
    kk_ref[DIL_BLOCK:] = k_ref[0]
    vv_ref[0:DIL_BLOCK] = vp_ref[0]
    vv_ref[DIL_BLOCK:] = v_ref[0]
    lane = lax.broadcasted_iota(jnp.int32, (BAND, LANES), 1)
    low = lane < HEAD_DIM
    for p, (window, dil) in enumerate(DIL_PATTERNS):
        dist, in_span, kj = _band_geometry(window // dil)
        units = DIL_BLOCK // BAND

        def unit(u, carry, p=p, dil=dil, dist=dist, in_span=in_span, kj=kj):
            c = u // dil
            start = c * (BAND * dil) + u % dil
            if dil == 1:
                rows = pl.ds(pl.multiple_of(start, BAND), BAND)
                krows = pl.ds(pl.multiple_of(DIL_BLOCK + start - BAND, BAND), 2 * BAND)
            else:
                rows = pl.ds(start, BAND, stride=dil)
                krows = pl.ds(DIL_BLOCK + start - BAND * dil, 2 * BAND, stride=dil)
            q = q_ref[0, rows, :].astype(BF16)
            kk = kk_ref[krows, :].astype(BF16)
            vv = vv_ref[krows, :].astype(BF16)
            valid = in_span & (kj >= jnp.where((n == 0) & (c == 0), BAND, 0))
            outs, lses = [], []
            for half in range(2):
                slope = jnp.where(pair == 0, slopes[half], slopes[2 + half])
                qm = jnp.where(low if half == 0 else ~low, q, jnp.zeros_like(q))
                o, lse = _band_tile(qm, kk, vv, dist * (-slope * dil), valid)
                outs.append(o)
                lses.append(jnp.broadcast_to(lse, (BAND, LANES)))
            po_ref[p, rows, :] = jnp.where(low, outs[0], outs[1])
            pl_ref[p, rows, :] = jnp.where(low, lses[0], lses[1])
            return carry

        lax.fori_loop(0, units, unit, 0, unroll=8)
    chunk = 256

    def merge(t, carry):
        rows = pl.ds(pl.multiple_of(t * chunk, chunk), chunk)
        lse = pl_ref[:, rows, :]
        w = jnp.exp(lse - jnp.max(lse, axis=0, keepdims=True))
        o_ref[0, rows, :] = (jnp.sum(w * po_ref[:, rows, :], axis=0) / jnp.sum(w, axis=0)).astype(o_ref.dtype)
        return carry

    lax.fori_loop(0, DIL_BLOCK // chunk, merge, 0)


def _dilated(pd):
    nb = SEQ // DIL_BLOCK
    cur = lambda off: pl.BlockSpec((1, DIL_BLOCK, LANES), lambda b, j, n: (off + j, b * nb + n, 0))
    prev = lambda off: pl.BlockSpec((1, DIL_BLOCK, LANES),
                                    lambda b, j, n: (off + j, b * nb + jnp.maximum(n - 1, 0), 0))
    return pl.pallas_call(
        functools.partial(_dil_kernel, slopes=_alibi_slopes()[SWA_Q_HEADS:]),
        grid=(BATCH, 2, nb),
        in_specs=[cur(0), cur(2), prev(2), cur(4), prev(4)],
        out_specs=pl.BlockSpec((1, DIL_BLOCK, LANES), lambda b, j, n: (j, b * nb + n, 0)),
        out_shape=jax.ShapeDtypeStruct((2, TOKENS, LANES), BF16),
        scratch_shapes=[pltpu.VMEM((2 * DIL_BLOCK, LANES), F32)] * 2
                       + [pltpu.VMEM((len(DIL_PATTERNS), DIL_BLOCK, LANES), F32)] * 2,
        compiler_params=_params("parallel", "parallel", "parallel"),
        name="dilated_attention",
    )(pd, pd, pd, pd, pd)


def _out_kernel(x_ref, ya_ref, yb_ref, yc_ref, yd_ref, w_ref, o_ref):
    mixed = jnp.concatenate([y[j] for y in (ya_ref, yb_ref, yc_ref, yd_ref) for j in range(2)], axis=1)
    o_ref[...] = x_ref[...] + jnp.dot(mixed, w_ref[...], preferred_element_type=F32)


def _out_proj(x, ys, w):
    tm = 512
    y_spec = pl.BlockSpec((2, tm, LANES), lambda i: (0, i, 0))
    return pl.pallas_call(
        _out_kernel,
        grid=(TOKENS // tm,),
        in_specs=[pl.BlockSpec((tm, D_MODEL), lambda i: (i, 0))] + [y_spec] * 4
                 + [pl.BlockSpec((D_MODEL, D_MODEL), lambda i: (0, 0))],
        out_specs=pl.BlockSpec((tm, D_MODEL), lambda i: (i, 0)),
        out_shape=jax.ShapeDtypeStruct((TOKENS, D_MODEL), F32),
        compiler_params=_params("parallel"),
        name="out_proj",
    )(x, *ys, w)


def _swiglu(h, wg, wu, wd):
    gate = jnp.dot(h, wg, preferred_element_type=F32)
    up = jnp.dot(h, wu, preferred_element_type=F32)
    act = (gate * jax.nn.sigmoid(gate) * up).astype(BF16)
    return jnp.dot(act, wd, preferred_element_type=F32)


def _ffn_kernel(x_ref, g_ref, wg_ref, wu_ref, wd_ref, o_ref, h_ref, acc_ref):
    f = pl.program_id(1)

    @pl.when(f == 0)
    def _():
        x = x_ref[...]
        h_ref[...] = _rms(x, g_ref[...]).astype(BF16)
        acc_ref[...] = x

    acc_ref[...] += _swiglu(h_ref[...], wg_ref[...].astype(BF16), wu_ref[...].astype(BF16), wd_ref[...].astype(BF16))

    @pl.when(f == pl.num_programs(1) - 1)
    def _():
        o_ref[...] = acc_ref[...]


def _ffn(x, g, wg, wu, wd, j):
    tm, tf = 1024, 512
    return pl.pallas_call(
        _ffn_kernel,
        grid=(TOKENS // tm, FFN_DIM // tf),
        in_specs=[pl.BlockSpec((tm, D_MODEL), lambda i, f: (i, 0)),
                  pl.BlockSpec((1, D_MODEL), lambda i, f: (0, 0)),
                  pl.BlockSpec((None, D_MODEL, tf), lambda i, f: (j, 0, f)),
                  pl.BlockSpec((None, D_MODEL, tf), lambda i, f: (j, 0, f)),
                  pl.BlockSpec((None, tf, D_MODEL), lambda i, f: (j, f, 0))],
        out_specs=pl.BlockSpec((tm, D_MODEL), lambda i, f: (i, 0)),
        out_shape=jax.ShapeDtypeStruct((TOKENS, D_MODEL), F32),
        scratch_shapes=[pltpu.VMEM((tm, D_MODEL), BF16), pltpu.VMEM((tm, D_MODEL), F32)],
        compiler_params=_params("parallel", "arbitrary"),
        name="dense_ffn",
    )(x, g, wg, wu, wd)


MOE_TM = 1024
MOE_STATIC_ROWS = 288
MOE_CHUNK = 32


def _route_kernel(x_ref, g_ref, router_ref, h_ref, comb_ref, pc_ref, pt_ref, cnt_ref):
    tm = x_ref.shape[0]
    h = _rms(x_ref[...], g_ref[...])
    h_ref[...] = h.astype(BF16)
    lane = lax.broadcasted_iota(jnp.int32, (tm, LANES), 1).astype(F32)
    logits = jnp.dot(h, router_ref[...], preferred_element_type=F32, precision=lax.Precision.HIGHEST)
    logits = jnp.where(lane < N_EXPERTS, logits, NEG)
    m1 = jnp.max(logits, axis=-1, keepdims=True)
    i1 = jnp.min(jnp.where(logits == m1, lane, float(LANES)), axis=-1, keepdims=True)
    rest = jnp.where(lane == i1, NEG, logits)
    m2 = jnp.max(rest, axis=-1, keepdims=True)
    i2 = jnp.min(jnp.where(rest == m2, lane, float(LANES)), axis=-1, keepdims=True)
    t = jnp.exp(m2 - m1)
    comb_ref[...] = jnp.where(lane == i1, 1.0 / (1.0 + t), 0.0) + jnp.where(lane == i2, t / (1.0 + t), 0.0)
    sel = jnp.where(lane == i1, 1.0, jnp.where(lane == i2, 1.0, 0.0))
    selb = sel.astype(BF16)
    r = lax.broadcasted_iota(jnp.int32, (tm, tm), 0)
    c = lax.broadcasted_iota(jnp.int32, (tm, tm), 1)
    rank = jnp.dot(jnp.where(c < r, 1.0, 0.0).astype(BF16), selb, preferred_element_type=F32)
    pc_ref[...] = jnp.where(sel > 0.0, rank, -1.0)
    eye = jnp.where(lax.broadcasted_iota(jnp.int32, (8, LANES), 0) == lax.broadcasted_iota(jnp.int32, (8, LANES), 1),
                    1.0, 0.0).astype(BF16)
    sel_t = _dot_nt(eye, selb)
    rank_t = jnp.dot(sel_t.astype(BF16), jnp.where(r < c, 1.0, 0.0).astype(BF16), preferred_element_type=F32)
    pt_ref[...] = jnp.where(sel_t > 0.0, rank_t, -1.0)
    cnt_ref[...] = jnp.broadcast_to(jnp.sum(sel, axis=0, keepdims=True), (8, LANES))


def _route(x, g, router_pad):
    tm = MOE_TM
    nt = TOKENS // tm
    return pl.pallas_call(
        _route_kernel,
        grid=(nt,),
        in_specs=[pl.BlockSpec((tm, D_MODEL), lambda i: (i, 0)), pl.BlockSpec((1, D_MODEL), lambda i: (0, 0)),
                  pl.BlockSpec((D_MODEL, LANES), lambda i: (0, 0))],
        out_specs=[pl.BlockSpec((tm, D_MODEL), lambda i: (i, 0)), pl.BlockSpec((tm, LANES), lambda i: (i, 0)),
                   pl.BlockSpec((tm, LANES), lambda i: (i, 0)), pl.BlockSpec((8, tm), lambda i: (0, i)),
                   pl.BlockSpec((8, LANES), lambda i: (i, 0))],
        out_shape=[jax.ShapeDtypeStruct((TOKENS, D_MODEL), BF16), jax.ShapeDtypeStruct((TOKENS, LANES), F32),
                   jax.ShapeDtypeStruct((TOKENS, LANES), F32), jax.ShapeDtypeStruct((8, TOKENS), F32),
                   jax.ShapeDtypeStruct((nt * 8, LANES), F32)],
        compiler_params=_params("parallel"),
        name="moe_route",
    )(x, g, router_pad)


def _moe_kernel(cnt_ref, h_ref, pt_ref, pc_ref, comb_ref, wg_ref, wu_ref, wd_ref, o_ref, xg_ref, yacc_ref, acc_ref):
    tm = h_ref.shape[0]
    sm, oc = MOE_STATIC_ROWS, MOE_CHUNK
    i = pl.program_id(0)
    e = pl.program_id(1)
    f = pl.program_id(2)
    n_over = jnp.maximum(cnt_ref[i * N_EXPERTS + e] - sm + oc - 1, 0) // oc

    def overflow(body):
        def step(c, carry):
            body(pl.multiple_of(sm + c * oc, oc), oc)
            return carry
        lax.fori_loop(0, n_over, step, 0)

    @pl.when((e == 0) & (f == 0))
    def _():
        acc_ref[...] = jnp.zeros(acc_ref.shape, F32)

    @pl.when(f == 0)
    def _():
        pt = pt_ref[pl.ds(e, 1), :]

        def gather(row0, nrows):
            ridx = lax.broadcasted_iota(jnp.int32, (nrows, tm), 0) + row0
            onehot = jnp.where(ridx.astype(F32) == pt, 1.0, 0.0).astype(BF16)
            xg_ref[pl.ds(row0, nrows), :] = jnp.dot(onehot, h_ref[...], preferred_element_type=F32).astype(BF16)
            yacc_ref[pl.ds(row0, nrows), :] = jnp.zeros((nrows, D_MODEL), F32)

        gather(0, sm)
        overflow(gather)

    def ffn(row0, nrows):
        rows = pl.ds(row0, nrows)
        yacc_ref[rows, :] += _swiglu(xg_ref[rows, :], wg_ref[0], wu_ref[0], wd_ref[0])

    ffn(0, sm)
    overflow(ffn)

    @pl.when(f == pl.num_programs(2) - 1)
    def _():
        lane = lax.broadcasted_iota(jnp.int32, (tm, LANES), 1)
        mine = lane == e
        pc = jnp.sum(jnp.where(mine, pc_ref[...], 0.0), axis=-1, keepdims=True)
        gate = jnp.sum(jnp.where(mine, comb_ref[...], 0.0), axis=-1, keepdims=True)

        def scatter(row0, nrows):
            cidx = lax.broadcasted_iota(jnp.int32, (tm, nrows), 1) + row0
            onehot = jnp.where(cidx.astype(F32) == pc, 1.0, 0.0).astype(BF16)
            y = yacc_ref[pl.ds(row0, nrows), :].astype(BF16)
            acc_ref[...] += jnp.dot(onehot, y, preferred_element_type=F32) * gate

        scatter(0, sm)
        overflow(scatter)

    @pl.when((e == pl.num_programs(1) - 1) & (f == pl.num_programs(2) - 1))
    def _():
        o_ref[...] = acc_ref[...].astype(o_ref.dtype)


def _moe(h, pt, pc, comb, counts, wg, wu, wd):
    tm, tf = MOE_TM, FFN_DIM // 2
    grid_spec = pltpu.PrefetchScalarGridSpec(
        num_scalar_prefetch=1,
        grid=(TOKENS // tm, N_EXPERTS, FFN_DIM // tf),
        in_specs=[pl.BlockSpec((tm, D_MODEL), lambda i, e, f, cnt: (i, 0)),
                  pl.BlockSpec((8, tm), lambda i, e, f, cnt: (0, i)),
                  pl.BlockSpec((tm, LANES), lambda i, e, f, cnt: (i, 0)),
                  pl.BlockSpec((tm, LANES), lambda i, e, f, cnt: (i, 0)),
                  pl.BlockSpec((1, D_MODEL, tf), lambda i, e, f, cnt: (e, 0, f)),
                  pl.BlockSpec((1, D_MODEL, tf), lambda i, e, f, cnt: (e, 0, f)),
                  pl.BlockSpec((1, tf, D_MODEL), lambda i, e, f, cnt: (e, f, 0))],
        out_specs=pl.BlockSpec((tm, D_MODEL), lambda i, e, f, cnt: (i, 0)),
        scratch_shapes=[pltpu.VMEM((tm, D_MODEL), BF16), pltpu.VMEM((tm, D_MODEL), F32),
                        pltpu.VMEM((tm, D_MODEL), F32)])
    return pl.pallas_call(
        _moe_kernel,
        grid_spec=grid_spec,
        out_shape=jax.ShapeDtypeStruct((TOKENS, D_MODEL), BF16),
        compiler_params=_params("parallel", "arbitrary", "arbitrary"),
        name="moe_ffn",
    )(counts, h, pt, pc, comb, wg, wu, wd)


def _final_kernel(x_ref, y_ref, g_ref, o_ref):
    o_ref[...] = _rms(x_ref[...] + y_ref[...].astype(F32), g_ref[...])


def _final(x, y, g):
    tm = 1024
    return pl.pallas_call(
        _final_kernel,
        grid=(TOKENS // tm,),
        in_specs=[pl.BlockSpec((tm, D_MODEL), lambda i: (i, 0)), pl.BlockSpec((tm, D_MODEL), lambda i: (i, 0)),
                  pl.BlockSpec((1, D_MODEL), lambda i: (0, 0))],
        out_specs=pl.BlockSpec((tm, D_MODEL), lambda i: (i, 0)),
        out_shape=jax.ShapeDtypeStruct((TOKENS, D_MODEL), F32),
        compiler_params=_params("parallel"),
        name="final_norm",
    )(x, y, g)


def kernel(x, positions, attn_norm, w_in, b_forget, mla_q_norm, w_q_up, mla_kv_norm, w_kv_up, sinks, w_out, ffn_norm, dense_w_gate, dense_w_up, dense_w_down, router, moe_w_gate, moe_w_up, moe_w_down, final_norm):
    assert x.shape == (BATCH, SEQ, D_MODEL) and positions.shape == (BATCH, SEQ)
    p_idx, p_sgn = _proj_columns()
    q_idx, q_sgn = _mla_q_columns()
    kv_idx, kv_sgn = _mla_kv_columns()
    mix_rows = _mix_rows()
    cos, sin = _rope_tables(positions)
    xt = x.reshape(TOKENS, D_MODEL).astype(F32)
    for layer in range(DEPTH):
        w = _take_columns(w_in[layer], p_idx, p_sgn).astype(BF16)
        wq = _take_columns(w_q_up[layer], q_idx, q_sgn).astype(BF16)
        wkv = _take_columns(w_kv_up[layer], kv_idx, kv_sgn).astype(BF16)
        fq, fk, fv, fl, pb, qc, kc, vc, pd = _project(
            xt, attn_norm[layer].reshape(1, D_MODEL), w, mla_q_norm[layer].reshape(1, MLA_Q_LORA), wq,
            mla_kv_norm[layer].reshape(1, MLA_KV_LORA), wkv, cos, sin)
        y_a = _flash(fq, _fox_keys(fl, b_forget[layer], fk), fv, "fox_attention")
        y_b = _swa(pb, sinks[layer])
        y_c = _flash(qc, kc, vc, "mla_attention")
        y_d = _dilated(pd)
        cuts = [0] + [r for r in range(1, len(mix_rows)) if mix_rows[r] != mix_rows[r - 1] + 1] + [len(mix_rows)]
        w_o = jnp.concatenate([w_out[layer][int(mix_rows[a]):int(mix_rows[a]) + (b - a)]
                               for a, b in zip(cuts[:-1], cuts[1:])], axis=0).astype(BF16)
        xt = _out_proj(xt, (y_a, y_b, y_c, y_d), w_o)
        j = layer // 2
        g = ffn_norm[layer].reshape(1, D_MODEL)
        if layer % 2 == 0:
            xt = _ffn(xt, g, dense_w_gate, dense_w_up, dense_w_down, j)
        else:
            assert layer == DEPTH - 1
            router_pad = jnp.zeros((D_MODEL, LANES), F32).at[:, :N_EXPERTS].set(router[j])
            h, comb, pc, pt, cnt = _route(xt, g, router_pad)
            counts = cnt[::8, :N_EXPERTS].astype(jnp.int32).reshape(-1)
            y = _moe(h, pt, pc, comb, counts, moe_w_gate[j].astype(BF16), moe_w_up[j].astype(BF16),
                     moe_w_down[j].astype(BF16))
            xt = _final(xt, y, final_norm.reshape(1, D_MODEL))
    return xt.reshape(BATCH, SEQ, D_MODEL)
```

```python
import functools

import numpy as np
import jax
import jax.numpy as jnp
from jax import lax
from jax.experimental import pallas as pl
from jax.experimental.pallas import tpu as pltpu

D_MODEL = 1024
BATCH = 2
SEQ = 8192
DEPTH = 2
TOKENS = BATCH * SEQ
HEAD_DIM = 64
BAND = 128
NORM_EPS = 1e-6
FOX_HEADS = 4
SWA_Q_HEADS = 4
SWA_KV_HEADS = 2
SWA_WINDOW = 128
MLA_HEADS = 4
MLA_Q_LORA = 256
MLA_KV_LORA = 128
MLA_NOPE_DIM = 64
MLA_ROPE_DIM = 32
MLA_V_DIM = 64
ROPE_THETA = 10000.0
DIL_HEADS = 4
DIL_PATTERNS = ((128, 1), (512, 4), (2048, 16))
DIL_BLOCK = BAND * max(d for _, d in DIL_PATTERNS)
FFN_DIM = 3584
N_EXPERTS = 8
LANES = 128
NEG = -1e30
VMEM_LIMIT = 56 * 1024 * 1024

_OFF = np.cumsum([0, 256, 256, 256, 4, 256, 128, 128, 256, 128, 32, 256, 256, 256])
(_A_Q, _A_K, _A_V, _A_F, _B_Q, _B_K, _B_V, _C_Q, _C_KV, _C_KR, _D_Q, _D_K, _D_V) = _OFF[:13].tolist()
N_PROJ_BLOCKS = 22
LOG2E = 1.4426950408889634

BF16 = jnp.bfloat16
F32 = jnp.float32


def _alibi_slopes():
    n = SWA_Q_HEADS + DIL_HEADS
    return [2.0 ** (-8.0 * i / n) for i in range(1, n + 1)]


def _proj_columns():
    idx = np.zeros((N_PROJ_BLOCKS * LANES,), np.int32)
    sgn = np.zeros((N_PROJ_BLOCKS * LANES,), np.float32)

    def put(dst, src, n, sign=1.0):
        idx[dst:dst + n] = np.arange(src, src + n)
        sgn[dst:dst + n] = sign

    put(0, _A_Q, 256); put(256, _A_K, 256); put(512, _A_V, 256)
    for blk, heads in ((6, (0, 2)), (7, (1, 3))):
        for half, h in enumerate(heads):
            put(blk * LANES + half * HEAD_DIM, _B_Q + h * HEAD_DIM, HEAD_DIM)
    put(8 * LANES, _B_K, 128); put(9 * LANES, _B_V, 128)
    put(10 * LANES, _C_Q, 256); put(12 * LANES, _C_KV, 128)
    half = MLA_ROPE_DIM // 2
    put(13 * LANES + MLA_NOPE_DIM, _C_KR, MLA_ROPE_DIM)
    put(14 * LANES + MLA_NOPE_DIM, _C_KR + half, half, -1.0)
    put(14 * LANES + MLA_NOPE_DIM + half, _C_KR, half)
    put(15 * LANES, _D_Q, 256); put(17 * LANES, _D_K, 256); put(19 * LANES, _D_V, 256)
    put(21 * LANES, _A_F, FOX_HEADS)
    return idx, sgn


def _bias_lane0(h):
    return HEAD_DIM if h % 2 == 0 else 0


def _bias_placement():
    place = np.zeros((3 * LANES, FOX_HEADS * LANES), np.float32)
    for piece in range(3):
        for h in range(FOX_HEADS):
            place[piece * LANES + h, h * LANES + _bias_lane0(h) + piece] = 1.0
    return place


def _mla_q_columns():
    idx = np.zeros((8 * LANES,), np.int32)
    sgn = np.zeros((8 * LANES,), np.float32)
    half = MLA_ROPE_DIM // 2
    dq = MLA_NOPE_DIM + MLA_ROPE_DIM
    for h in range(MLA_HEADS):
        a = h * LANES
        idx[a:a + dq] = np.arange(h * dq, (h + 1) * dq); sgn[a:a + dq] = 1.0
        b = (MLA_HEADS + h) * LANES + MLA_NOPE_DIM
        r = h * dq + MLA_NOPE_DIM
        idx[b:b + half] = np.arange(r + half, r + 2 * half); sgn[b:b + half] = -1.0
        idx[b + half:b + 2 * half] = np.arange(r, r + half); sgn[b + half:b + 2 * half] = 1.0
    return idx, sgn


def _mla_kv_columns():
    idx = np.zeros((6 * LANES,), np.int32)
    sgn = np.zeros((6 * LANES,), np.float32)
    dkv = MLA_NOPE_DIM + MLA_V_DIM
    for h in range(MLA_HEADS):
        idx[h * LANES:h * LANES + MLA_NOPE_DIM] = np.arange(h * dkv, h * dkv + MLA_NOPE_DIM)
        sgn[h * LANES:h * LANES + MLA_NOPE_DIM] = 1.0
        b = MLA_HEADS * LANES + h * MLA_V_DIM
        idx[b:b + MLA_V_DIM] = np.arange(h * dkv + MLA_NOPE_DIM, (h + 1) * dkv)
        sgn[b:b + MLA_V_DIM] = 1.0
    return idx, sgn


def _take_columns(w, idx, sgn):
    parts = []
    a = 0
    while a < len(idx):
        b = a + 1
        while b < len(idx) and sgn[b] == sgn[a] and (sgn[a] == 0 or idx[b] == idx[b - 1] + 1):
            b += 1
        if sgn[a] == 0:
            parts.append(jnp.zeros((w.shape[0], b - a), w.dtype))
        else:
            piece = w[:, int(idx[a]):int(idx[a]) + (b - a)]
            parts.append(piece if sgn[a] > 0 else -piece)
        a = b
    return jnp.concatenate(parts, axis=1)


def _mix_rows():
    rows = np.arange(4 * 256)
    b = 256
    perm = np.concatenate([np.arange(b + h * HEAD_DIM, b + (h + 1) * HEAD_DIM) for h in (0, 2, 1, 3)])
    rows[b:b + 256] = perm
    return rows


def _rms(x, g):
    return x * lax.rsqrt(jnp.mean(x * x, axis=-1, keepdims=True) + NORM_EPS) * g


def _dot_nt(a, b):
    return lax.dot_general(a, b, (((1,), (1,)), ((), ())), preferred_element_type=F32)


def _lane_tile(x, width):
    return x if width == LANES else jnp.concatenate([x] * (width // LANES), axis=1)


def _params(*sem):
    return pltpu.CompilerParams(dimension_semantics=sem, vmem_limit_bytes=VMEM_LIMIT)


def _rope_table_kernel(pos_ref, invf_ref, cos_ref, sin_ref):
    ang = pos_ref[...].astype(F32) * invf_ref[...]
    cos_ref[...] = jnp.cos(ang)
    sin_ref[...] = jnp.sin(ang)


def _rope_tables(positions):
    tm = 2048
    half = MLA_ROPE_DIM // 2
    invf = np.zeros((1, LANES), np.float32)
    f = (ROPE_THETA ** (-np.arange(half, dtype=np.float32) / np.float32(half))).astype(np.float32)
    invf[0, MLA_NOPE_DIM:MLA_NOPE_DIM + half] = f
    invf[0, MLA_NOPE_DIM + half:MLA_NOPE_DIM + 2 * half] = f
    return pl.pallas_call(
        _rope_table_kernel,
        grid=(TOKENS // tm,),
        in_specs=[pl.BlockSpec((tm, 1), lambda i: (i, 0)), pl.BlockSpec((1, LANES), lambda i: (0, 0))],
        out_specs=[pl.BlockSpec((tm, LANES), lambda i: (i, 0))] * 2,
        out_shape=[jax.ShapeDtypeStruct((TOKENS, LANES), F32)] * 2,
        compiler_params=_params("parallel"),
        name="rope_tables",
    )(positions.reshape(TOKENS, 1), jnp.asarray(invf))


def _proj_kernel(x_ref, g_ref, w_ref, qn_ref, wq_ref, kvn_ref, wkv_ref, cos_ref, sin_ref,
                 fq_ref, fk_ref, fv_ref, fl_ref, pb_ref, qc_ref, kc_ref, vc_ref, pd_ref):
    hb = _rms(x_ref[...], g_ref[...]).astype(BF16)
    res = jnp.dot(hb, w_ref[...], preferred_element_type=F32)
    lane = lax.broadcasted_iota(jnp.int32, (x_ref.shape[0], LANES), 1)
    low = lane < HEAD_DIM

    def blk(j, n=1):
        return res[:, j * LANES:(j + n) * LANES]

    def own(h):
        return low if h % 2 == 0 else ~low

    qscale = HEAD_DIM ** -0.5 * LOG2E
    for h in range(FOX_HEADS):
        ones = (lane >= _bias_lane0(h)) & (lane < _bias_lane0(h) + 3)
        fq_ref[h] = jnp.where(own(h), blk(h // 2) * qscale, jnp.where(ones, 1.0, 0.0)).astype(BF16)
        fv_ref[h] = jnp.where(own(h), blk(4 + h // 2), 1.0).astype(BF16)
    for j in range(2):
        fk_ref[j] = blk(2 + j).astype(BF16)
    fl_ref[...] = blk(21)
    for j in range(2):
        pb_ref[j] = (blk(6 + j) * qscale).astype(BF16)
    pb_ref[2] = blk(8).astype(BF16)
    pb_ref[3] = blk(9).astype(BF16)

    cos = cos_ref[...]
    sin = sin_ref[...]
    cq = _rms(blk(10, 2), qn_ref[...]).astype(BF16)
    qab = jnp.dot(cq, wq_ref[...], preferred_element_type=F32)
    mla_scale = (MLA_NOPE_DIM + MLA_ROPE_DIM) ** -0.5
    for h in range(MLA_HEADS):
        qa = qab[:, h * LANES:(h + 1) * LANES]
        qb = qab[:, (MLA_HEADS + h) * LANES:(MLA_HEADS + h + 1) * LANES]
        qc_ref[h] = ((qa * cos + qb * sin) * (mla_scale * LOG2E)).astype(BF16)
    ckv = _rms(blk(12), kvn_ref[...]).astype(BF16)
    kv = jnp.dot(ckv, wkv_ref[...], preferred_element_type=F32)
    k_rot = blk(13) * cos + blk(14) * sin
    for h in range(MLA_HEADS):
        kc_ref[h] = (kv[:, h * LANES:(h + 1) * LANES] + k_rot).astype(BF16)
        v_pair = kv[:, (MLA_HEADS + h // 2) * LANES:(MLA_HEADS + h // 2 + 1) * LANES]
        vc_ref[h] = jnp.where(own(h), v_pair, 1.0).astype(BF16)

    for j in range(2):
        pd_ref[j] = blk(15 + j) * qscale
    for j in range(2, 6):
        pd_ref[j] = blk(15 + j)


def _project(x, g, w, qn, wq, kvn, wkv, cos, sin):
    tm = 512
    full = lambda shape: pl.BlockSpec(shape, lambda i: (0,) * len(shape))
    out_blk = lambda n: pl.BlockSpec((n, tm, LANES), lambda i: (0, i, 0))
    out_sds = lambda n, dt: jax.ShapeDtypeStruct((n, TOKENS, LANES), dt)
    tok_blk = pl.BlockSpec((tm, LANES), lambda i: (i, 0))
    return pl.pallas_call(
        _proj_kernel,
        grid=(TOKENS // tm,),
        in_specs=[pl.BlockSpec((tm, D_MODEL), lambda i: (i, 0)), full((1, D_MODEL)),
                  full((D_MODEL, N_PROJ_BLOCKS * LANES)),
                  full((1, MLA_Q_LORA)), full((MLA_Q_LORA, 8 * LANES)),
                  full((1, MLA_KV_LORA)), full((MLA_KV_LORA, 6 * LANES)), tok_blk, tok_blk],
        out_specs=[out_blk(4), out_blk(2), out_blk(4), tok_blk, out_blk(4), out_blk(4), out_blk(4), out_blk(4),
                   out_blk(6)],
        out_shape=[out_sds(4, BF16), out_sds(2, BF16), out_sds(4, BF16), jax.ShapeDtypeStruct((TOKENS, LANES), F32),
                   out_sds(4, BF16), out_sds(4, BF16), out_sds(4, BF16), out_sds(4, BF16), out_sds(6, F32)],
        compiler_params=_params("parallel"),
        name="in_proj",
    )(x, g, w, qn, wq, kvn, wkv, cos, sin)


def _split3(x):
    hi = x.astype(BF16)
    r1 = x - hi.astype(F32)
    mid = r1.astype(BF16)
    return hi, mid, (r1 - mid.astype(F32)).astype(BF16)


def _fox_keys_kernel(fl_ref, b_ref, k_ref, place_ref, kf_ref, carry_ref, *, cs):
    @pl.when(pl.program_id(1) == 0)
    def _():
        carry_ref[...] = jnp.zeros_like(carry_ref)

    lf = jax.nn.log_sigmoid(fl_ref[...] + b_ref[...])
    row = lax.broadcasted_iota(jnp.int32, (cs, cs), 0)
    col = lax.broadcasted_iota(jnp.int32, (cs, cs), 1)
    tri = jnp.where(col <= row, 1.0, 0.0).astype(BF16)
    cum = carry_ref[0:1, :] + sum(jnp.dot(tri, piece, preferred_element_type=F32) for piece in _split3(lf))
    carry_ref[0:1, :] = cum[cs - 1:cs, :]
    pieces = jnp.concatenate(_split3(cum * (-LOG2E)), axis=1)
    placed = jnp.dot(pieces, place_ref[...], preferred_element_type=F32).astype(BF16)
    low = lax.broadcasted_iota(jnp.int32, (cs, LANES), 1) < HEAD_DIM
    for h in range(FOX_HEADS):
        kf_ref[h] = jnp.where(low if h % 2 == 0 else ~low, k_ref[h // 2], placed[:, h * LANES:(h + 1) * LANES])


def _fox_keys(fl, b_forget, fk):
    cs = 512
    nc = SEQ // cs
    b_row = jnp.zeros((1, LANES), F32).at[0, :FOX_HEADS].set(b_forget.astype(F32))
    return pl.pallas_call(
        functools.partial(_fox_keys_kernel, cs=cs),
        grid=(BATCH, nc),
        in_specs=[pl.BlockSpec((cs, LANES), lambda b, i: (b * nc + i, 0)),
                  pl.BlockSpec((1, LANES), lambda b, i: (0, 0)),
                  pl.BlockSpec((2, cs, LANES), lambda b, i: (0, b * nc + i, 0)),
                  pl.BlockSpec((3 * LANES, FOX_HEADS * LANES), lambda b, i: (0, 0))],
        out_specs=pl.BlockSpec((FOX_HEADS, cs, LANES), lambda b, i: (0, b * nc + i, 0)),
        out_shape=jax.ShapeDtypeStruct((FOX_HEADS, TOKENS, LANES), BF16),
        scratch_shapes=[pltpu.VMEM((8, LANES), F32)],
        compiler_params=_params("parallel", "arbitrary"),
        name="fox_keys",
    )(fl, b_row, fk, jnp.asarray(_bias_placement(), BF16))


def _flash_kernel(q_ref, k_ref, v_ref, o_ref, acc_ref, m_ref, *, tq, tk):
    i = pl.program_id(2)
    m_ref[...] = jnp.full(m_ref.shape, NEG, F32)
    acc_ref[...] = jnp.zeros(acc_ref.shape, F32)

    def step(kb, row0, diagonal):
        ks = pl.multiple_of(kb * tk, tk)
        rows = pl.ds(row0, tq - row0)
        for h in range(2):
            s = _dot_nt(q_ref[h, rows, :], k_ref[h, pl.ds(ks, tk), :])
            if diagonal:
                r = lax.broadcasted_iota(jnp.int32, s.shape, 0)
                c = lax.broadcasted_iota(jnp.int32, s.shape, 1)
                s = jnp.where(c <= r, s, NEG)
            m_old = m_ref[h, rows, :]
            m_new = jnp.maximum(m_old, jnp.max(s, axis=-1, keepdims=True))
            p = jnp.exp2(s - _lane_tile(m_new, tk)).astype(BF16)
            acc_ref[h, rows, :] = (jnp.exp2(m_old - m_new) * acc_ref[h, rows, :]
                                   + jnp.dot(p, v_ref[h, pl.ds(ks, tk), :], preferred_element_type=F32))
            m_ref[h, rows, :] = m_new

    def body(kb, carry):
        step(kb, 0, False)
        return carry

    per_q = tq // tk
    lax.fori_loop(0, i * per_q, body, 0)
    for u in range(per_q):
        step(i * per_q + u, u * tk, True)
    outs = [acc_ref[h] / pltpu.roll(acc_ref[h], HEAD_DIM, axis=1) for h in range(2)]
    low = lax.broadcasted_iota(jnp.int32, (tq, LANES), 1) < HEAD_DIM
    o_ref[0] = jnp.where(low, outs[0], outs[1]).astype(o_ref.dtype)


def _flash(q, k, v, name):
    tq, tk = 2048, 512
    nq = SEQ // tq
    kv_spec = pl.BlockSpec((2, SEQ, LANES), lambda b, j, i: (j, b, 0))
    return pl.pallas_call(
        functools.partial(_flash_kernel, tq=tq, tk=tk),
        grid=(BATCH, 2, nq),
        in_specs=[pl.BlockSpec((2, tq, LANES), lambda b, j, i: (j, b * nq + i, 0)), kv_spec, kv_spec],
        out_specs=pl.BlockSpec((1, tq, LANES), lambda b, j, i: (j, b * nq + i, 0)),
        out_shape=jax.ShapeDtypeStruct((2, TOKENS, LANES), BF16),
        scratch_shapes=[pltpu.VMEM((2, tq, LANES), F32)] * 2,
        compiler_params=_params("parallel", "parallel", "parallel"),
        name=name,
    )(q, k, v)


def _band_mask_bias(span, slope_step, first):
    qi = lax.broadcasted_iota(jnp.int32, (BAND, 2 * BAND), 0)
    kj = lax.broadcasted_iota(jnp.int32, (BAND, 2 * BAND), 1)
    dist = BAND + qi - kj
    ok = (dist >= 0) & (dist <= span) & (kj >= jnp.where(first, BAND, 0))
    return jnp.where(ok, dist.astype(F32) * (-slope_step), NEG)


def _band_pair(q, kk, vv, mask_bias):
    low = lax.broadcasted_iota(jnp.int32, (BAND, LANES), 1) < HEAD_DIM
    low_kv = lax.broadcasted_iota(jnp.int32, (2 * BAND, LANES), 1) < HEAD_DIM
    accs, ms = [], []
    for half in range(2):
        qm = jnp.where(low if half == 0 else ~low, q, jnp.zeros_like(q))
        vh = jnp.where(low_kv if half == 0 else ~low_kv, vv, jnp.ones_like(vv))
        s = _dot_nt(qm, kk) + mask_bias[half]
        m = jnp.max(s, axis=-1, keepdims=True)
        accs.append(jnp.dot(jnp.exp2(s - m).astype(BF16), vh, preferred_element_type=F32))
        ms.append(m)
    l = pltpu.roll(jnp.where(low, accs[1], accs[0]), HEAD_DIM, axis=1)
    return jnp.where(low, accs[0], accs[1]) / l, jnp.where(low, ms[0], ms[1]) + jnp.log2(l)


def _swa_kernel(q_ref, k_ref, kp_ref, v_ref, vp_ref, sink_ref, o_ref, kk_ref, vv_ref, *, tb, slopes):
    n = pl.program_id(1)
    kk_ref[0:BAND] = kp_ref[0]
    kk_ref[BAND:] = k_ref[0]
    vv_ref[0:BAND] = vp_ref[0]
    vv_ref[BAND:] = v_ref[0]
    mask_bias = [[_band_mask_bias(SWA_WINDOW - 1, slope * LOG2E, first) for slope in slopes]
                 for first in (n == 0, False)]
    for c in range(tb // BAND):
        kk = kk_ref[c * BAND:(c + 2) * BAND]
        vv = vv_ref[c * BAND:(c + 2) * BAND]
        mb = mask_bias[0 if c == 0 else 1]
        for jb in range(2):
            o, lse2 = _band_pair(q_ref[jb, c * BAND:(c + 1) * BAND, :], kk, vv, (mb[jb], mb[jb + 2]))
            o = o / (1.0 + jnp.exp2(sink_ref[jb:jb + 1, :] - lse2))
            o_ref[jb, c * BAND:(c + 1) * BAND, :] = o.astype(o_ref.dtype)


def _swa(pb, sinks):
    tb = 512
    nb = SEQ // tb
    r = tb // BAND
    s = sinks.astype(F32) * LOG2E
    sink_lanes = jnp.stack([jnp.concatenate([jnp.full((HEAD_DIM,), s[jb]), jnp.full((HEAD_DIM,), s[jb + 2])])
                            for jb in range(2)])
    cur = lambda blk: pl.BlockSpec((1, tb, LANES), lambda b, n: (blk, b * nb + n, 0))
    prev = lambda blk: pl.BlockSpec((1, BAND, LANES), lambda b, n: (blk, jnp.maximum((b * nb + n) * r - 1, 0), 0))
    return pl.pallas_call(
        functools.partial(_swa_kernel, tb=tb, slopes=_alibi_slopes()[:SWA_Q_HEADS]),
        grid=(BATCH, nb),
        in_specs=[pl.BlockSpec((2, tb, LANES), lambda b, n: (0, b * nb + n, 0)), cur(2), prev(2), cur(3), prev(3),
                  pl.BlockSpec((2, LANES), lambda b, n: (0, 0))],
        out_specs=pl.BlockSpec((2, tb, LANES), lambda b, n: (0, b * nb + n, 0)),
        out_shape=jax.ShapeDtypeStruct((2, TOKENS, LANES), BF16),
        scratch_shapes=[pltpu.VMEM((tb + BAND, LANES), BF16)] * 2,
        compiler_params=_params("parallel", "parallel"),
        name="swa_attention",
    )(pb, pb, pb, pb, pb, sink_lanes)


def _dil_kernel(q_ref, k_ref, kp_ref, v_ref, vp_ref, o_ref, kk_ref, vv_ref, po_ref, pl_ref, *, slopes):
    pair = pl.program_id(1)
    n = pl.program_id(2)
    kk_ref[0:DIL_BLOCK] = kp_ref[0]
    kk_ref[DIL_BLOCK:] = k_ref[0]
    vv_ref[0:DIL_BLOCK] = vp_ref[0]
    vv_ref[DIL_BLOCK:] = v_ref[0]
    units = DIL_BLOCK // BAND
    for p, (window, dil) in enumerate(DIL_PATTERNS):
        steps = [jnp.where(pair == 0, slopes[half], slopes[2 + half]) * (dil * LOG2E) for half in range(2)]
        mask_bias = [[_band_mask_bias(window // dil, step, first) for step in steps] for first in (n == 0, False)]

        def unit(u, carry, p=p, dil=dil, mb=None):
            start = (u // dil) * (BAND * dil) + u % dil
            if dil == 1:
                rows = pl.ds(pl.multiple_of(start, BAND), BAND)
                krows = pl.ds(pl.multiple_of(DIL_BLOCK + start - BAND, BAND), 2 * BAND)
            else:
                rows = pl.ds(start, BAND, stride=dil)
                krows = pl.ds(DIL_BLOCK + start - BAND * dil, 2 * BAND, stride=dil)
            o, lse2 = _band_pair(q_ref[0, rows, :].astype(BF16), kk_ref[krows, :].astype(BF16),
                                 vv_ref[krows, :].astype(BF16), mb)
            po_ref[p, rows, :] = o
            pl_ref[p, rows, :] = lse2
            return carry

        lax.fori_loop(0, dil, functools.partial(unit, mb=mask_bias[0]), 0, unroll=min(dil, 8))
        if dil < units:
            lax.fori_loop(dil, units, functools.partial(unit, mb=mask_bias[1]), 0, unroll=min(units - dil, 8))
    chunk = 256

    def merge(t, carry):
        rows = pl.ds(pl.multiple_of(t * chunk, chunk), chunk)
        lse = pl_ref[:, rows, :]
        w = jnp.exp2(lse - jnp.max(lse, axis=0, keepdims=True))
        o_ref[0, rows, :] = (jnp.sum(w * po_ref[:, rows, :], axis=0) / jnp.sum(w, axis=0)).astype(o_ref.dtype)
        return carry

    lax.fori_loop(0, DIL_BLOCK // chunk, merge, 0)


def _dilated(pd):
    nb = SEQ // DIL_BLOCK
    cur = lambda off: pl.BlockSpec((1, DIL_BLOCK, LANES), lambda b, j, n: (off + j, b * nb + n, 0))
    prev = lambda off: pl.BlockSpec((1, DIL_BLOCK, LANES),
                                    lambda b, j, n: (off + j, b * nb + jnp.maximum(n - 1, 0), 0))
    return pl.pallas_call(
        functools.partial(_dil_kernel, slopes=_alibi_slopes()[SWA_Q_HEADS:]),
        grid=(BATCH, 2, nb),
        in_specs=[cur(0), cur(2), prev(2), cur(4), prev(4)],
        out_specs=pl.BlockSpec((1, DIL_BLOCK, LANES), lambda b, j, n: (j, b * nb + n, 0)),
        out_shape=jax.ShapeDtypeStruct((2, TOKENS, LANES), BF16),
        scratch_shapes=[pltpu.VMEM((2 * DIL_BLOCK, LANES), F32)] * 2
                       + [pltpu.VMEM((len(DIL_PATTERNS), DIL_BLOCK, LANES), F32)] * 2,
        compiler_params=_params("parallel", "parallel", "parallel"),
        name="dilated_attention",
    )(pd, pd, pd, pd, pd)


def _out_kernel(x_ref, ya_ref, yb_ref, yc_ref, yd_ref, w_ref, o_ref):
    mixed = jnp.concatenate([y[j] for y in (ya_ref, yb_ref, yc_ref, yd_ref) for j in range(2)], axis=1)
    o_ref[...] = x_ref[...] + jnp.dot(mixed, w_ref[...], preferred_element_type=F32)


def _out_proj(x, ys, w):
    tm = 512
    y_spec = pl.BlockSpec((2, tm, LANES), lambda i: (0, i, 0))
    return pl.pallas_call(
        _out_kernel,
        grid=(TOKENS // tm,),
        in_specs=[pl.BlockSpec((tm, D_MODEL), lambda i: (i, 0))] + [y_spec] * 4
                 + [pl.BlockSpec((D_MODEL, D_MODEL), lambda i: (0, 0))],
        out_specs=pl.BlockSpec((tm, D_MODEL), lambda i: (i, 0)),
        out_shape=jax.ShapeDtypeStruct((TOKENS, D_MODEL), F32),
        compiler_params=_params("parallel"),
        name="out_proj",
    )(x, *ys, w)


def _swiglu(h, wg, wu, wd):
    gate = jnp.dot(h, wg, preferred_element_type=F32)
    up = jnp.dot(h, wu, preferred_element_type=F32)
    act = (gate * jax.nn.sigmoid(gate) * up).astype(BF16)
    return jnp.dot(act, wd, preferred_element_type=F32)


def _ffn_kernel(x_ref, g_ref, wg_ref, wu_ref, wd_ref, o_ref, h_ref, acc_ref):
    f = pl.program_id(1)

    @pl.when(f == 0)
    def _():
        x = x_ref[...]
        h_ref[...] = _rms(x, g_ref[...]).astype(BF16)
        acc_ref[...] = x

    acc_ref[...] += _swiglu(h_ref[...], wg_ref[...].astype(BF16), wu_ref[...].astype(BF16), wd_ref[...].astype(BF16))

    @pl.when(f == pl.num_programs(1) - 1)
    def _():
        o_ref[...] = acc_ref[...]


def _ffn(x, g, wg, wu, wd, j):
    tm, tf = 1024, 512
    return pl.pallas_call(
        _ffn_kernel,
        grid=(TOKENS // tm, FFN_DIM // tf),
        in_specs=[pl.BlockSpec((tm, D_MODEL), lambda i, f: (i, 0)),
                  pl.BlockSpec((1, D_MODEL), lambda i, f: (0, 0)),
                  pl.BlockSpec((None, D_MODEL, tf), lambda i, f: (j, 0, f)),
                  pl.BlockSpec((None, D_MODEL, tf), lambda i, f: (j, 0, f)),
                  pl.BlockSpec((None, tf, D_MODEL), lambda i, f: (j, f, 0))],
        out_specs=pl.BlockSpec((tm, D_MODEL), lambda i, f: (i, 0)),
        out_shape=jax.ShapeDtypeStruct((TOKENS, D_MODEL), F32),
        scratch_shapes=[pltpu.VMEM((tm, D_MODEL), BF16), pltpu.VMEM((tm, D_MODEL), F32)],
        compiler_params=_params("parallel", "arbitrary"),
        name="dense_ffn",
    )(x, g, wg, wu, wd)


MOE_TM = 1024
MOE_STATIC_ROWS = 288
MOE_CHUNK = 32


def _route_kernel(x_ref, g_ref, router_ref, h_ref, comb_ref, pc_ref, pt_ref, cnt_ref):
    tm = x_ref.shape[0]
    h = _rms(x_ref[...], g_ref[...])
    h_ref[...] = h.astype(BF16)
    lane = lax.broadcasted_iota(jnp.int32, (tm, LANES), 1).astype(F32)
    h_hi, h_lo, _ = _split3(h)
    r_hi, r_lo, _ = _split3(router_ref[...])
    logits = (jnp.dot(h_hi, r_hi, preferred_element_type=F32) + jnp.dot(h_hi, r_lo, preferred_element_type=F32)
              + jnp.dot(h_lo, r_hi, preferred_element_type=F32))
    logits = jnp.where(lane < N_EXPERTS, logits, NEG)
    m1 = jnp.max(logits, axis=-1, keepdims=True)
    i1 = jnp.min(jnp.where(logits == m1, lane, float(LANES)), axis=-1, keepdims=True)
    rest = jnp.where(lane == i1, NEG, logits)
    m2 = jnp.max(rest, axis=-1, keepdims=True)
    i2 = jnp.min(jnp.where(rest == m2, lane, float(LANES)), axis=-1, keepdims=True)
    t = jnp.exp(m2 - m1)
    comb_ref[...] = jnp.where(lane == i1, 1.0 / (1.0 + t), 0.0) + jnp.where(lane == i2, t / (1.0 + t), 0.0)
    sel = jnp.where(lane == i1, 1.0, jnp.where(lane == i2, 1.0, 0.0))
    selb = sel.astype(BF16)
    r = lax.broadcasted_iota(jnp.int32, (tm, tm), 0)
    c = lax.broadcasted_iota(jnp.int32, (tm, tm), 1)
    rank = jnp.dot(jnp.where(c < r, 1.0, 0.0).astype(BF16), selb, preferred_element_type=F32)
    pc_ref[...] = jnp.where(sel > 0.0, rank, -1.0)
    eye = jnp.where(lax.broadcasted_iota(jnp.int32, (8, LANES), 0) == lax.broadcasted_iota(jnp.int32, (8, LANES), 1),
                    1.0, 0.0).astype(BF16)
    sel_t = _dot_nt(eye, selb)
    rank_t = jnp.dot(sel_t.astype(BF16), jnp.where(r < c, 1.0, 0.0).astype(BF16), preferred_element_type=F32)
    pt_ref[...] = jnp.where(sel_t > 0.0, rank_t, -1.0)
    cnt_ref[...] = jnp.broadcast_to(jnp.sum(sel, axis=0, keepdims=True), (8, LANES))


def _route(x, g, router_pad):
    tm = MOE_TM
    nt = TOKENS // tm
    return pl.pallas_call(
        _route_kernel,
        grid=(nt,),
        in_specs=[pl.BlockSpec((tm, D_MODEL), lambda i: (i, 0)), pl.BlockSpec((1, D_MODEL), lambda i: (0, 0)),
                  pl.BlockSpec((D_MODEL, LANES), lambda i: (0, 0))],
        out_specs=[pl.BlockSpec((tm, D_MODEL), lambda i: (i, 0)), pl.BlockSpec((tm, LANES), lambda i: (i, 0)),
                   pl.BlockSpec((tm, LANES), lambda i: (i, 0)), pl.BlockSpec((8, tm), lambda i: (0, i)),
                   pl.BlockSpec((8, LANES), lambda i: (i, 0))],
        out_shape=[jax.ShapeDtypeStruct((TOKENS, D_MODEL), BF16), jax.ShapeDtypeStruct((TOKENS, LANES), F32),
                   jax.ShapeDtypeStruct((TOKENS, LANES), F32), jax.ShapeDtypeStruct((8, TOKENS), F32),
                   jax.ShapeDtypeStruct((nt * 8, LANES), F32)],
        compiler_params=_params("parallel"),
        name="moe_route",
    )(x, g, router_pad)


def _moe_kernel(cnt_ref, h_ref, pt_ref, pc_ref, comb_ref, wg_ref, wu_ref, wd_ref, o_ref, xg_ref, yacc_ref, acc_ref):
    tm = h_ref.shape[0]
    sm, oc = MOE_STATIC_ROWS, MOE_CHUNK
    i = pl.program_id(0)
    e = pl.program_id(1)
    f = pl.program_id(2)
    n_over = jnp.maximum(cnt_ref[i * N_EXPERTS + e] - sm + oc - 1, 0) // oc

    def overflow(body):
        def step(c, carry):
            body(pl.multiple_of(sm + c * oc, oc), oc)
            return carry
        lax.fori_loop(0, n_over, step, 0)

    @pl.when((e == 0) & (f == 0))
    def _():
        acc_ref[...] = jnp.zeros(acc_ref.shape, F32)

    @pl.when(f == 0)
    def _():
        pt = pt_ref[pl.ds(e, 1), :]

        def gather(row0, nrows):
            ridx = lax.broadcasted_iota(jnp.int32, (nrows, tm), 0) + row0
            onehot = jnp.where(ridx.astype(F32) == pt, 1.0, 0.0).astype(BF16)
            xg_ref[pl.ds(row0, nrows), :] = jnp.dot(onehot, h_ref[...], preferred_element_type=F32).astype(BF16)
            yacc_ref[pl.ds(row0, nrows), :] = jnp.zeros((nrows, D_MODEL), F32)

        gather(0, sm)
        overflow(gather)

    def ffn(row0, nrows):
        rows = pl.ds(row0, nrows)
        yacc_ref[rows, :] += _swiglu(xg_ref[rows, :], wg_ref[0], wu_ref[0], wd_ref[0])

    ffn(0, sm)
    overflow(ffn)

    @pl.when(f == pl.num_programs(2) - 1)
    def _():
        lane = lax.broadcasted_iota(jnp.int32, (tm, LANES), 1)
        mine = lane == e
        pc = jnp.sum(jnp.where(mine, pc_ref[...], 0.0), axis=-1, keepdims=True)
        gate = jnp.sum(jnp.where(mine, comb_ref[...], 0.0), axis=-1, keepdims=True)

        def scatter(row0, nrows):
            cidx = lax.broadcasted_iota(jnp.int32, (tm, nrows), 1) + row0
            onehot = jnp.where(cidx.astype(F32) == pc, 1.0, 0.0).astype(BF16)
            y = yacc_ref[pl.ds(row0, nrows), :].astype(BF16)
            acc_ref[...] += jnp.dot(onehot, y, preferred_element_type=F32) * gate

        scatter(0, sm)
        overflow(scatter)

    @pl.when((e == pl.num_programs(1) - 1) & (f == pl.num_programs(2) - 1))
    def _():
        o_ref[...] = acc_ref[...].astype(o_ref.dtype)


def _moe(h, pt, pc, comb, counts, wg, wu, wd):
    tm, tf = MOE_TM, FFN_DIM // 2
    grid_spec = pltpu.PrefetchScalarGridSpec(
        num_scalar_prefetch=1,
        grid=(TOKENS // tm, N_EXPERTS, FFN_DIM // tf),
        in_specs=[pl.BlockSpec((tm, D_MODEL), lambda i, e, f, cnt: (i, 0)),
                  pl.BlockSpec((8, tm), lambda i, e, f, cnt: (0, i)),
                  pl.BlockSpec((tm, LANES), lambda i, e, f, cnt: (i, 0)),
                  pl.BlockSpec((tm, LANES), lambda i, e, f, cnt: (i, 0)),
                  pl.BlockSpec((1, D_MODEL, tf), lambda i, e, f, cnt: (e, 0, f)),
                  pl.BlockSpec((1, D_MODEL, tf), lambda i, e, f, cnt: (e, 0, f)),
                  pl.BlockSpec((1, tf, D_MODEL), lambda i, e, f, cnt: (e, f, 0))],
        out_specs=pl.BlockSpec((tm, D_MODEL), lambda i, e, f, cnt: (i, 0)),
        scratch_shapes=[pltpu.VMEM((tm, D_MODEL), BF16), pltpu.VMEM((tm, D_MODEL), F32),
                        pltpu.VMEM((tm, D_MODEL), F32)])
    return pl.pallas_call(
        _moe_kernel,
        grid_spec=grid_spec,
        out_shape=jax.ShapeDtypeStruct((TOKENS, D_MODEL), BF16),
        compiler_params=_params("parallel", "arbitrary", "arbitrary"),
        name="moe_ffn",
    )(counts, h, pt, pc, comb, wg, wu, wd)


def _final_kernel(x_ref, y_ref, g_ref, o_ref):
    o_ref[...] = _rms(x_ref[...] + y_ref[...].astype(F32), g_ref[...])


def _final(x, y, g):
    tm = 1024
    return pl.pallas_call(
        _final_kernel,
        grid=(TOKENS // tm,),
        in_specs=[pl.BlockSpec((tm, D_MODEL), lambda i: (i, 0)), pl.BlockSpec((tm, D_MODEL), lambda i: (i, 0)),
                  pl.BlockSpec((1, D_MODEL), lambda i: (0, 0))],
        out_specs=pl.BlockSpec((tm, D_MODEL), lambda i: (i, 0)),
        out_shape=jax.ShapeDtypeStruct((TOKENS, D_MODEL), F32),
        compiler_params=_params("parallel"),
        name="final_norm",
    )(x, y, g)


def kernel(x, positions, attn_norm, w_in, b_forget, mla_q_norm, w_q_up, mla_kv_norm, w_kv_up, sinks, w_out, ffn_norm, dense_w_gate, dense_w_up, dense_w_down, router, moe_w_gate, moe_w_up, moe_w_down, final_norm):
    assert x.shape == (BATCH, SEQ, D_MODEL) and positions.shape == (BATCH, SEQ)
    p_idx, p_sgn = _proj_columns()
    q_idx, q_sgn = _mla_q_columns()
    kv_idx, kv_sgn = _mla_kv_columns()
    mix_rows = _mix_rows()
    cos, sin = _rope_tables(positions)
    xt = x.reshape(TOKENS, D_MODEL).astype(F32)
    for layer in range(DEPTH):
        w = _take_columns(w_in[layer], p_idx, p_sgn).astype(BF16)
        wq = _take_columns(w_q_up[layer], q_idx, q_sgn).astype(BF16)
        wkv = _take_columns(w_kv_up[layer], kv_idx, kv_sgn).astype(BF16)
        fq, fk, fv, fl, pb, qc, kc, vc, pd = _project(
            xt, attn_norm[layer].reshape(1, D_MODEL), w, mla_q_norm[layer].reshape(1, MLA_Q_LORA), wq,
            mla_kv_norm[layer].reshape(1, MLA_KV_LORA), wkv, cos, sin)
        y_a = _flash(fq, _fox_keys(fl, b_forget[layer], fk), fv, "fox_attention")
        y_b = _swa(pb, sinks[layer])
        y_c = _flash(qc, kc, vc, "mla_attention")
        y_d = _dilated(pd)
        cuts = [0] + [r for r in range(1, len(mix_rows)) if mix_rows[r] != mix_rows[r - 1] + 1] + [len(mix_rows)]
        w_o = jnp.concatenate([w_out[layer][int(mix_rows[a]):int(mix_rows[a]) + (b - a)]
                               for a, b in zip(cuts[:-1], cuts[1:])], axis=0).astype(BF16)
        xt = _out_proj(xt, (y_a, y_b, y_c, y_d), w_o)
        j = layer // 2
        g = ffn_norm[layer].reshape(1, D_MODEL)
        if layer % 2 == 0:
            xt = _ffn(xt, g, dense_w_gate, dense_w_up, dense_w_down, j)
        else:
            assert layer == DEPTH - 1
            router_pad = jnp.zeros((D_MODEL, LANES), F32).at[:, :N_EXPERTS].set(router[j])
            h, comb, pc, pt, cnt = _route(xt, g, router_pad)
            counts = cnt[::8, :N_EXPERTS].astype(jnp.int32).reshape(-1)
            y = _moe(h, pt, pc, comb, counts, moe_w_gate[j].astype(BF16), moe_w_up[j].astype(BF16),
                     moe_w_down[j].astype(BF16))
            xt = _final(xt, y, final_norm.reshape(1, D_MODEL))
    return xt.reshape(BATCH, SEQ, D_MODEL)
```

```python
import functools

import numpy as np
import jax
import jax.numpy as jnp
from jax import lax
from jax.experimental import pallas as pl
from jax.experimental.pallas import tpu as pltpu

D_MODEL = 1024
BATCH = 2
SEQ = 8192
DEPTH = 2
TOKENS = BATCH * SEQ
HEAD_DIM = 64
BAND = 128
NORM_EPS = 1e-6
FOX_HEADS = 4
SWA_Q_HEADS = 4
SWA_KV_HEADS = 2
SWA_WINDOW = 128
MLA_HEADS = 4
MLA_Q_LORA = 256
MLA_KV_LORA = 128
MLA_NOPE_DIM = 64
MLA_ROPE_DIM = 32
MLA_V_DIM = 64
ROPE_THETA = 10000.0
DIL_HEADS = 4
DIL_PATTERNS = ((128, 1), (512, 4), (2048, 16))
DIL_BLOCK = BAND * max(d for _, d in DIL_PATTERNS)
FFN_DIM = 3584
N_EXPERTS = 8
LANES = 128
NEG = -1e30
VMEM_LIMIT = 56 * 1024 * 1024

_OFF = np.cumsum([0, 256, 256, 256, 4, 256, 128, 128, 256, 128, 32, 256, 256, 256])
(_A_Q, _A_K, _A_V, _A_F, _B_Q, _B_K, _B_V, _C_Q, _C_KV, _C_KR, _D_Q, _D_K, _D_V) = _OFF[:13].tolist()
N_PROJ_BLOCKS = 22
LOG2E = 1.4426950408889634

BF16 = jnp.bfloat16
F32 = jnp.float32


def _alibi_slopes():
    n = SWA_Q_HEADS + DIL_HEADS
    return [2.0 ** (-8.0 * i / n) for i in range(1, n + 1)]


def _proj_columns():
    idx = np.zeros((N_PROJ_BLOCKS * LANES,), np.int32)
    sgn = np.zeros((N_PROJ_BLOCKS * LANES,), np.float32)

    def put(dst, src, n, sign=1.0):
        idx[dst:dst + n] = np.arange(src, src + n)
        sgn[dst:dst + n] = sign

    put(0, _A_Q, 256); put(256, _A_K, 256); put(512, _A_V, 256)
    for blk, heads in ((6, (0, 2)), (7, (1, 3))):
        for half, h in enumerate(heads):
            put(blk * LANES + half * HEAD_DIM, _B_Q + h * HEAD_DIM, HEAD_DIM)
    put(8 * LANES, _B_K, 128); put(9 * LANES, _B_V, 128)
    put(10 * LANES, _C_Q, 256); put(12 * LANES, _C_KV, 128)
    half = MLA_ROPE_DIM // 2
    put(13 * LANES + MLA_NOPE_DIM, _C_KR, MLA_ROPE_DIM)
    put(14 * LANES + MLA_NOPE_DIM, _C_KR + half, half, -1.0)
    put(14 * LANES + MLA_NOPE_DIM + half, _C_KR, half)
    put(15 * LANES, _D_Q, 256); put(17 * LANES, _D_K, 256); put(19 * LANES, _D_V, 256)
    put(21 * LANES, _A_F, FOX_HEADS)
    return idx, sgn


def _bias_lane0(h):
    return HEAD_DIM if h % 2 == 0 else 0


def _bias_placement():
    place = np.zeros((3 * LANES, FOX_HEADS * LANES), np.float32)
    for piece in range(3):
        for h in range(FOX_HEADS):
            place[piece * LANES + h, h * LANES + _bias_lane0(h) + piece] = 1.0
    return place


def _mla_q_columns():
    idx = np.zeros((8 * LANES,), np.int32)
    sgn = np.zeros((8 * LANES,), np.float32)
    half = MLA_ROPE_DIM // 2
    dq = MLA_NOPE_DIM + MLA_ROPE_DIM
    for h in range(MLA_HEADS):
        a = h * LANES
        idx[a:a + dq] = np.arange(h * dq, (h + 1) * dq); sgn[a:a + dq] = 1.0
        b = (MLA_HEADS + h) * LANES + MLA_NOPE_DIM
        r = h * dq + MLA_NOPE_DIM
        idx[b:b + half] = np.arange(r + half, r + 2 * half); sgn[b:b + half] = -1.0
        idx[b + half:b + 2 * half] = np.arange(r, r + half); sgn[b + half:b + 2 * half] = 1.0
    return idx, sgn


def _mla_kv_columns():
    idx = np.zeros((6 * LANES,), np.int32)
    sgn = np.zeros((6 * LANES,), np.float32)
    dkv = MLA_NOPE_DIM + MLA_V_DIM
    for h in range(MLA_HEADS):
        idx[h * LANES:h * LANES + MLA_NOPE_DIM] = np.arange(h * dkv, h * dkv + MLA_NOPE_DIM)
        sgn[h * LANES:h * LANES + MLA_NOPE_DIM] = 1.0
        b = MLA_HEADS * LANES + h * MLA_V_DIM
        idx[b:b + MLA_V_DIM] = np.arange(h * dkv + MLA_NOPE_DIM, (h + 1) * dkv)
        sgn[b:b + MLA_V_DIM] = 1.0
    return idx, sgn


def _take_columns(w, idx, sgn):
    parts = []
    a = 0
    while a < len(idx):
        b = a + 1
        while b < len(idx) and sgn[b] == sgn[a] and (sgn[a] == 0 or idx[b] == idx[b - 1] + 1):
            b += 1
        if sgn[a] == 0:
            parts.append(jnp.zeros((w.shape[0], b - a), w.dtype))
        else:
            piece = w[:, int(idx[a]):int(idx[a]) + (b - a)]
            parts.append(piece if sgn[a] > 0 else -piece)
        a = b
    return jnp.concatenate(parts, axis=1)


def _mix_rows():
    rows = np.arange(4 * 256)
    b = 256
    perm = np.concatenate([np.arange(b + h * HEAD_DIM, b + (h + 1) * HEAD_DIM) for h in (0, 2, 1, 3)])
    rows[b:b + 256] = perm
    return rows


def _rms(x, g):
    return x * lax.rsqrt(jnp.mean(x * x, axis=-1, keepdims=True) + NORM_EPS) * g


def _dot_nt(a, b):
    return lax.dot_general(a, b, (((1,), (1,)), ((), ())), preferred_element_type=F32)


def _lane_tile(x, width):
    return x if width == LANES else jnp.concatenate([x] * (width // LANES), axis=1)


def _params(*sem):
    return pltpu.CompilerParams(dimension_semantics=sem, vmem_limit_bytes=VMEM_LIMIT)


def _rope_table_kernel(pos_ref, invf_ref, cos_ref, sin_ref):
    ang = pos_ref[...].astype(F32) * invf_ref[...]
    cos_ref[...] = jnp.cos(ang)
    sin_ref[...] = jnp.sin(ang)


def _rope_tables(positions):
    tm = 2048
    half = MLA_ROPE_DIM // 2
    invf = np.zeros((1, LANES), np.float32)
    f = (ROPE_THETA ** (-np.arange(half, dtype=np.float32) / np.float32(half))).astype(np.float32)
    invf[0, MLA_NOPE_DIM:MLA_NOPE_DIM + half] = f
    invf[0, MLA_NOPE_DIM + half:MLA_NOPE_DIM + 2 * half] = f
    return pl.pallas_call(
        _rope_table_kernel,
        grid=(TOKENS // tm,),
        in_specs=[pl.BlockSpec((tm, 1), lambda i: (i, 0)), pl.BlockSpec((1, LANES), lambda i: (0, 0))],
        out_specs=[pl.BlockSpec((tm, LANES), lambda i: (i, 0))] * 2,
        out_shape=[jax.ShapeDtypeStruct((TOKENS, LANES), F32)] * 2,
        compiler_params=_params("parallel"),
        name="rope_tables",
    )(positions.reshape(TOKENS, 1), jnp.asarray(invf))


def _proj_kernel(x_ref, g_ref, w_ref, qn_ref, wq_ref, kvn_ref, wkv_ref, cos_ref, sin_ref,
                 fq_ref, fk_ref, fv_ref, fl_ref, pb_ref, qc_ref, kc_ref, vc_ref, pd_ref):
    hb = _rms(x_ref[...], g_ref[...]).astype(BF16)
    res = jnp.dot(hb, w_ref[...], preferred_element_type=F32)
    lane = lax.broadcasted_iota(jnp.int32, (x_ref.shape[0], LANES), 1)
    low = lane < HEAD_DIM

    def blk(j, n=1):
        return res[:, j * LANES:(j + n) * LANES]

    def own(h):
        return low if h % 2 == 0 else ~low

    qscale = HEAD_DIM ** -0.5 * LOG2E
    for h in range(FOX_HEADS):
        ones = (lane >= _bias_lane0(h)) & (lane < _bias_lane0(h) + 3)
        fq_ref[h] = jnp.where(own(h), blk(h // 2) * qscale, jnp.where(ones, 1.0, 0.0)).astype(BF16)
        fv_ref[h] = jnp.where(own(h), blk(4 + h // 2), 1.0).astype(BF16)
    for j in range(2):
        fk_ref[j] = blk(2 + j).astype(BF16)
    fl_ref[...] = blk(21)
    for j in range(2):
        pb_ref[j] = (blk(6 + j) * qscale).astype(BF16)
    pb_ref[2] = blk(8).astype(BF16)
    pb_ref[3] = blk(9).astype(BF16)

    cos = cos_ref[...]
    sin = sin_ref[...]
    cq = _rms(blk(10, 2), qn_ref[...]).astype(BF16)
    qab = jnp.dot(cq, wq_ref[...], preferred_element_type=F32)
    mla_scale = (MLA_NOPE_DIM + MLA_ROPE_DIM) ** -0.5
    for h in range(MLA_HEADS):
        qa = qab[:, h * LANES:(h + 1) * LANES]
        qb = qab[:, (MLA_HEADS + h) * LANES:(MLA_HEADS + h + 1) * LANES]
        qc_ref[h] = ((qa * cos + qb * sin) * (mla_scale * LOG2E)).astype(BF16)
    ckv = _rms(blk(12), kvn_ref[...]).astype(BF16)
    kv = jnp.dot(ckv, wkv_ref[...], preferred_element_type=F32)
    k_rot = blk(13) * cos + blk(14) * sin
    for h in range(MLA_HEADS):
        kc_ref[h] = (kv[:, h * LANES:(h + 1) * LANES] + k_rot).astype(BF16)
        v_pair = kv[:, (MLA_HEADS + h // 2) * LANES:(MLA_HEADS + h // 2 + 1) * LANES]
        vc_ref[h] = jnp.where(own(h), v_pair, 1.0).astype(BF16)

    for j in range(2):
        pd_ref[j] = blk(15 + j) * qscale
    for j in range(2, 6):
        pd_ref[j] = blk(15 + j)


def _project(x, g, w, qn, wq, kvn, wkv, cos, sin):
    tm = 512
    full = lambda shape: pl.BlockSpec(shape, lambda i: (0,) * len(shape))
    out_blk = lambda n: pl.BlockSpec((n, tm, LANES), lambda i: (0, i, 0))
    out_sds = lambda n, dt: jax.ShapeDtypeStruct((n, TOKENS, LANES), dt)
    tok_blk = pl.BlockSpec((tm, LANES), lambda i: (i, 0))
    return pl.pallas_call(
        _proj_kernel,
        grid=(TOKENS // tm,),
        in_specs=[pl.BlockSpec((tm, D_MODEL), lambda i: (i, 0)), full((1, D_MODEL)),
                  full((D_MODEL, N_PROJ_BLOCKS * LANES)),
                  full((1, MLA_Q_LORA)), full((MLA_Q_LORA, 8 * LANES)),
                  full((1, MLA_KV_LORA)), full((MLA_KV_LORA, 6 * LANES)), tok_blk, tok_blk],
        out_specs=[out_blk(4), out_blk(2), out_blk(4), tok_blk, out_blk(4), out_blk(4), out_blk(4), out_blk(4),
                   out_blk(6)],
        out_shape=[out_sds(4, BF16), out_sds(2, BF16), out_sds(4, BF16), jax.ShapeDtypeStruct((TOKENS, LANES), F32),
                   out_sds(4, BF16), out_sds(4, BF16), out_sds(4, BF16), out_sds(4, BF16), out_sds(6, F32)],
        compiler_params=_params("parallel"),
        name="in_proj",
    )(x, g, w, qn, wq, kvn, wkv, cos, sin)


def _split3(x):
    hi = x.astype(BF16)
    r1 = x - hi.astype(F32)
    mid = r1.astype(BF16)
    return hi, mid, (r1 - mid.astype(F32)).astype(BF16)


def _fox_keys_kernel(fl_ref, b_ref, k_ref, place_ref, kf_ref, carry_ref, *, cs):
    @pl.when(pl.program_id(1) == 0)
    def _():
        carry_ref[...] = jnp.zeros_like(carry_ref)

    lf = jax.nn.log_sigmoid(fl_ref[...] + b_ref[...])
    row = lax.broadcasted_iota(jnp.int32, (cs, cs), 0)
    col = lax.broadcasted_iota(jnp.int32, (cs, cs), 1)
    tri = jnp.where(col <= row, 1.0, 0.0).astype(BF16)
    cum = carry_ref[0:1, :] + sum(jnp.dot(tri, piece, preferred_element_type=F32) for piece in _split3(lf))
    carry_ref[0:1, :] = cum[cs - 1:cs, :]
    pieces = jnp.concatenate(_split3(cum * (-LOG2E)), axis=1)
    placed = jnp.dot(pieces, place_ref[...], preferred_element_type=F32).astype(BF16)
    low = lax.broadcasted_iota(jnp.int32, (cs, LANES), 1) < HEAD_DIM
    for h in range(FOX_HEADS):
        kf_ref[h] = jnp.where(low if h % 2 == 0 else ~low, k_ref[h // 2], placed[:, h * LANES:(h + 1) * LANES])


def _fox_keys(fl, b_forget, fk):
    cs = 512
    nc = SEQ // cs
    b_row = jnp.zeros((1, LANES), F32).at[0, :FOX_HEADS].set(b_forget.astype(F32))
    return pl.pallas_call(
        functools.partial(_fox_keys_kernel, cs=cs),
        grid=(BATCH, nc),
        in_specs=[pl.BlockSpec((cs, LANES), lambda b, i: (b * nc + i, 0)),
                  pl.BlockSpec((1, LANES), lambda b, i: (0, 0)),
                  pl.BlockSpec((2, cs, LANES), lambda b, i: (0, b * nc + i, 0)),
                  pl.BlockSpec((3 * LANES, FOX_HEADS * LANES), lambda b, i: (0, 0))],
        out_specs=pl.BlockSpec((FOX_HEADS, cs, LANES), lambda b, i: (0, b * nc + i, 0)),
        out_shape=jax.ShapeDtypeStruct((FOX_HEADS, TOKENS, LANES), BF16),
        scratch_shapes=[pltpu.VMEM((8, LANES), F32)],
        compiler_params=_params("parallel", "arbitrary"),
        name="fox_keys",
    )(fl, b_row, fk, jnp.asarray(_bias_placement(), BF16))


def _flash_kernel(q_ref, k_ref, v_ref, o_ref, acc_ref, m_ref, *, tq, tk):
    i = pl.program_id(2)
    m_ref[...] = jnp.full(m_ref.shape, NEG, F32)
    acc_ref[...] = jnp.zeros(acc_ref.shape, F32)

    def step(kb, row0, diagonal):
        ks = pl.multiple_of(kb * tk, tk)
        rows = pl.ds(row0, tq - row0)
        for h in range(2):
            s = _dot_nt(q_ref[h, rows, :], k_ref[h, pl.ds(ks, tk), :])
            if diagonal:
                r = lax.broadcasted_iota(jnp.int32, s.shape, 0)
                c = lax.broadcasted_iota(jnp.int32, s.shape, 1)
                s = jnp.where(c <= r, s, NEG)
            m_old = m_ref[h, rows, :]
            m_new = jnp.maximum(m_old, jnp.max(s, axis=-1, keepdims=True))
            p = jnp.exp2(s - _lane_tile(m_new, tk)).astype(BF16)
            acc_ref[h, rows, :] = (jnp.exp2(m_old - m_new) * acc_ref[h, rows, :]
                                   + jnp.dot(p, v_ref[h, pl.ds(ks, tk), :], preferred_element_type=F32))
            m_ref[h, rows, :] = m_new

    def body(kb, carry):
        step(kb, 0, False)
        return carry

    per_q = tq // tk
    lax.fori_loop(0, i * per_q, body, 0)
    for u in range(per_q):
        step(i * per_q + u, u * tk, True)
    outs = [acc_ref[h] / pltpu.roll(acc_ref[h], HEAD_DIM, axis=1) for h in range(2)]
    low = lax.broadcasted_iota(jnp.int32, (tq, LANES), 1) < HEAD_DIM
    o_ref[0] = jnp.where(low, outs[0], outs[1]).astype(o_ref.dtype)


def _flash(q, k, v, name):
    tq, tk = 4096, 512
    nq = SEQ // tq
    kv_spec = pl.BlockSpec((2, SEQ, LANES), lambda b, j, i: (j, b, 0))
    return pl.pallas_call(
        functools.partial(_flash_kernel, tq=tq, tk=tk),
        grid=(BATCH, 2, nq),
        in_specs=[pl.BlockSpec((2, tq, LANES), lambda b, j, i: (j, b * nq + i, 0)), kv_spec, kv_spec],
        out_specs=pl.BlockSpec((1, tq, LANES), lambda b, j, i: (j, b * nq + i, 0)),
        out_shape=jax.ShapeDtypeStruct((2, TOKENS, LANES), BF16),
        scratch_shapes=[pltpu.VMEM((2, tq, LANES), F32)] * 2,
        compiler_params=_params("parallel", "parallel", "parallel"),
        name=name,
    )(q, k, v)


def _band_mask_bias(span, slope_step, first):
    qi = lax.broadcasted_iota(jnp.int32, (BAND, 2 * BAND), 0)
    kj = lax.broadcasted_iota(jnp.int32, (BAND, 2 * BAND), 1)
    dist = BAND + qi - kj
    ok = (dist >= 0) & (dist <= span) & (kj >= jnp.where(first, BAND, 0))
    return jnp.where(ok, dist.astype(F32) * (-slope_step), NEG)


def _band_pair(q, kk, vv, mask_bias):
    low = lax.broadcasted_iota(jnp.int32, (BAND, LANES), 1) < HEAD_DIM
    low_kv = lax.broadcasted_iota(jnp.int32, (2 * BAND, LANES), 1) < HEAD_DIM
    accs, ms = [], []
    for half in range(2):
        qm = jnp.where(low if half == 0 else ~low, q, jnp.zeros_like(q))
        vh = jnp.where(low_kv if half == 0 else ~low_kv, vv, jnp.ones_like(vv))
        s = _dot_nt(qm, kk) + mask_bias[half]
        m = jnp.max(s, axis=-1, keepdims=True)
        accs.append(jnp.dot(jnp.exp2(s - m).astype(BF16), vh, preferred_element_type=F32))
        ms.append(m)
    l = pltpu.roll(jnp.where(low, accs[1], accs[0]), HEAD_DIM, axis=1)
    return jnp.where(low, accs[0], accs[1]) / l, jnp.where(low, ms[0], ms[1]) + jnp.log2(l)


def _swa_kernel(q_ref, k_ref, kp_ref, v_ref, vp_ref, sink_ref, o_ref, kk_ref, vv_ref, *, tb, slopes):
    n = pl.program_id(1)
    kk_ref[0:BAND] = kp_ref[0]
    kk_ref[BAND:] = k_ref[0]
    vv_ref[0:BAND] = vp_ref[0]
    vv_ref[BAND:] = v_ref[0]
    mask_bias = [[_band_mask_bias(SWA_WINDOW - 1, slope * LOG2E, first) for slope in slopes]
                 for first in (n == 0, False)]
    for c in range(tb // BAND):
        kk = kk_ref[c * BAND:(c + 2) * BAND]
        vv = vv_ref[c * BAND:(c + 2) * BAND]
        mb = mask_bias[0 if c == 0 else 1]
        for jb in range(2):
            o, lse2 = _band_pair(q_ref[jb, c * BAND:(c + 1) * BAND, :], kk, vv, (mb[jb], mb[jb + 2]))
            o = o / (1.0 + jnp.exp2(sink_ref[jb:jb + 1, :] - lse2))
            o_ref[jb, c * BAND:(c + 1) * BAND, :] = o.astype(o_ref.dtype)


def _swa(pb, sinks):
    tb = 512
    nb = SEQ // tb
    r = tb // BAND
    s = sinks.astype(F32) * LOG2E
    sink_lanes = jnp.stack([jnp.concatenate([jnp.full((HEAD_DIM,), s[jb]), jnp.full((HEAD_DIM,), s[jb + 2])])
                            for jb in range(2)])
    cur = lambda blk: pl.BlockSpec((1, tb, LANES), lambda b, n: (blk, b * nb + n, 0))
    prev = lambda blk: pl.BlockSpec((1, BAND, LANES), lambda b, n: (blk, jnp.maximum((b * nb + n) * r - 1, 0), 0))
    return pl.pallas_call(
        functools.partial(_swa_kernel, tb=tb, slopes=_alibi_slopes()[:SWA_Q_HEADS]),
        grid=(BATCH, nb),
        in_specs=[pl.BlockSpec((2, tb, LANES), lambda b, n: (0, b * nb + n, 0)), cur(2), prev(2), cur(3), prev(3),
                  pl.BlockSpec((2, LANES), lambda b, n: (0, 0))],
        out_specs=pl.BlockSpec((2, tb, LANES), lambda b, n: (0, b * nb + n, 0)),
        out_shape=jax.ShapeDtypeStruct((2, TOKENS, LANES), BF16),
        scratch_shapes=[pltpu.VMEM((tb + BAND, LANES), BF16)] * 2,
        compiler_params=_params("parallel", "parallel"),
        name="swa_attention",
    )(pb, pb, pb, pb, pb, sink_lanes)


def _dil_kernel(q_ref, k_ref, kp_ref, v_ref, vp_ref, o_ref, kk_ref, vv_ref, po_ref, pl_ref, *, slopes):
    pair = pl.program_id(1)
    n = pl.program_id(2)
    kk_ref[0:DIL_BLOCK] = kp_ref[0]
    kk_ref[DIL_BLOCK:] = k_ref[0]
    vv_ref[0:DIL_BLOCK] = vp_ref[0]
    vv_ref[DIL_BLOCK:] = v_ref[0]
    units = DIL_BLOCK // BAND
    for p, (window, dil) in enumerate(DIL_PATTERNS):
        steps = [jnp.where(pair == 0, slopes[half], slopes[2 + half]) * (dil * LOG2E) for half in range(2)]
        mask_bias = [[_band_mask_bias(window // dil, step, first) for step in steps] for first in (n == 0, False)]

        def unit(u, carry, p=p, dil=dil, mb=None):
            start = (u // dil) * (BAND * dil) + u % dil
            if dil == 1:
                rows = pl.ds(pl.multiple_of(start, BAND), BAND)
                krows = pl.ds(pl.multiple_of(DIL_BLOCK + start - BAND, BAND), 2 * BAND)
            else:
                rows = pl.ds(start, BAND, stride=dil)
                krows = pl.ds(DIL_BLOCK + start - BAND * dil, 2 * BAND, stride=dil)
            o, lse2 = _band_pair(q_ref[0, rows, :].astype(BF16), kk_ref[krows, :].astype(BF16),
                                 vv_ref[krows, :].astype(BF16), mb)
            po_ref[p, rows, :] = o
            pl_ref[p, rows, :] = lse2
            return carry

        lax.fori_loop(0, dil, functools.partial(unit, mb=mask_bias[0]), 0, unroll=min(dil, 8))
        if dil < units:
            lax.fori_loop(dil, units, functools.partial(unit, mb=mask_bias[1]), 0, unroll=min(units - dil, 8))
    chunk = 256

    def merge(t, carry):
        rows = pl.ds(pl.multiple_of(t * chunk, chunk), chunk)
        lse = pl_ref[:, rows, :]
        w = jnp.exp2(lse - jnp.max(lse, axis=0, keepdims=True))
        o_ref[0, rows, :] = (jnp.sum(w * po_ref[:, rows, :], axis=0) / jnp.sum(w, axis=0)).astype(o_ref.dtype)
        return carry

    lax.fori_loop(0, DIL_BLOCK // chunk, merge, 0)


def _dilated(pd):
    nb = SEQ // DIL_BLOCK
    cur = lambda off: pl.BlockSpec((1, DIL_BLOCK, LANES), lambda b, j, n: (off + j, b * nb + n, 0))
    prev = lambda off: pl.BlockSpec((1, DIL_BLOCK, LANES),
                                    lambda b, j, n: (off + j, b * nb + jnp.maximum(n - 1, 0), 0))
    return pl.pallas_call(
        functools.partial(_dil_kernel, slopes=_alibi_slopes()[SWA_Q_HEADS:]),
        grid=(BATCH, 2, nb),
        in_specs=[cur(0), cur(2), prev(2), cur(4), prev(4)],
        out_specs=pl.BlockSpec((1, DIL_BLOCK, LANES), lambda b, j, n: (j, b * nb + n, 0)),
        out_shape=jax.ShapeDtypeStruct((2, TOKENS, LANES), BF16),
        scratch_shapes=[pltpu.VMEM((2 * DIL_BLOCK, LANES), F32)] * 2
                       + [pltpu.VMEM((len(DIL_PATTERNS), DIL_BLOCK, LANES), F32)] * 2,
        compiler_params=_params("parallel", "parallel", "parallel"),
        name="dilated_attention",
    )(pd, pd, pd, pd, pd)


def _mix_residual(x_ref, y_refs, w_ref):
    mixed = jnp.concatenate([y[j] for y in y_refs for j in range(2)], axis=1)
    return x_ref[...] + jnp.dot(mixed, w_ref[...], preferred_element_type=F32)


def _mix_specs(tm):
    return ([pl.BlockSpec((tm, D_MODEL), lambda i, *_: (i, 0))]
            + [pl.BlockSpec((2, tm, LANES), lambda i, *_: (0, i, 0))] * 4
            + [pl.BlockSpec((D_MODEL, D_MODEL), lambda i, *_: (0, 0))])


def _swiglu(h, wg, wu, wd):
    gate = jnp.dot(h, wg, preferred_element_type=F32)
    up = jnp.dot(h, wu, preferred_element_type=F32)
    act = (gate * jax.nn.sigmoid(gate) * up).astype(BF16)
    return jnp.dot(act, wd, preferred_element_type=F32)


def _ffn_kernel(x_ref, ya_ref, yb_ref, yc_ref, yd_ref, wo_ref, g_ref, wg_ref, wu_ref, wd_ref, o_ref, h_ref, acc_ref):
    f = pl.program_id(1)

    @pl.when(f == 0)
    def _():
        x = _mix_residual(x_ref, (ya_ref, yb_ref, yc_ref, yd_ref), wo_ref)
        h_ref[...] = _rms(x, g_ref[...]).astype(BF16)
        acc_ref[...] = x

    acc_ref[...] += _swiglu(h_ref[...], wg_ref[...].astype(BF16), wu_ref[...].astype(BF16), wd_ref[...].astype(BF16))

    @pl.when(f == pl.num_programs(1) - 1)
    def _():
        o_ref[...] = acc_ref[...]


def _ffn(x, ys, w_out, g, wg, wu, wd, j):
    tm, tf = 1024, 512
    return pl.pallas_call(
        _ffn_kernel,
        grid=(TOKENS // tm, FFN_DIM // tf),
        in_specs=_mix_specs(tm) + [
                  pl.BlockSpec((1, D_MODEL), lambda i, f: (0, 0)),
                  pl.BlockSpec((None, D_MODEL, tf), lambda i, f: (j, 0, f)),
                  pl.BlockSpec((None, D_MODEL, tf), lambda i, f: (j, 0, f)),
                  pl.BlockSpec((None, tf, D_MODEL), lambda i, f: (j, f, 0))],
        out_specs=pl.BlockSpec((tm, D_MODEL), lambda i, f: (i, 0)),
        out_shape=jax.ShapeDtypeStruct((TOKENS, D_MODEL), F32),
        scratch_shapes=[pltpu.VMEM((tm, D_MODEL), BF16), pltpu.VMEM((tm, D_MODEL), F32)],
        compiler_params=_params("parallel", "arbitrary"),
        name="dense_ffn",
    )(x, *ys, w_out, g, wg, wu, wd)


MOE_TM = 1024
MOE_STATIC_ROWS = 288
MOE_CHUNK = 32
MOE_HALF_MIN = 96
MOE_HALF_MAX = 160


def _route_kernel(x_ref, ya_ref, yb_ref, yc_ref, yd_ref, wo_ref, g_ref, router_ref,
                  x1_ref, h_ref, comb_ref, pc_ref, pt_ref, cnt_ref):
    tm = x_ref.shape[0]
    x1 = _mix_residual(x_ref, (ya_ref, yb_ref, yc_ref, yd_ref), wo_ref)
    x1_ref[...] = x1
    h = _rms(x1, g_ref[...])
    h_ref[...] = h.astype(BF16)
    lane = lax.broadcasted_iota(jnp.int32, (tm, LANES), 1).astype(F32)
    h_hi, h_lo, _ = _split3(h)
    r_hi, r_lo, _ = _split3(router_ref[...])
    logits = (jnp.dot(h_hi, r_hi, preferred_element_type=F32) + jnp.dot(h_hi, r_lo, preferred_element_type=F32)
              + jnp.dot(h_lo, r_hi, preferred_element_type=F32))
    logits = jnp.where(lane < N_EXPERTS, logits, NEG)
    m1 = jnp.max(logits, axis=-1, keepdims=True)
    i1 = jnp.min(jnp.where(logits == m1, lane, float(LANES)), axis=-1, keepdims=True)
    rest = jnp.where(lane == i1, NEG, logits)
    m2 = jnp.max(rest, axis=-1, keepdims=True)
    i2 = jnp.min(jnp.where(rest == m2, lane, float(LANES)), axis=-1, keepdims=True)
    t = jnp.exp(m2 - m1)
    comb_ref[...] = jnp.where(lane == i1, 1.0 / (1.0 + t), 0.0) + jnp.where(lane == i2, t / (1.0 + t), 0.0)
    sel = jnp.where(lane == i1, 1.0, jnp.where(lane == i2, 1.0, 0.0))
    selb = sel.astype(BF16)
    r = lax.broadcasted_iota(jnp.int32, (tm, tm), 0)
    c = lax.broadcasted_iota(jnp.int32, (tm, tm), 1)
    rank = jnp.dot(jnp.where(c < r, 1.0, 0.0).astype(BF16), selb, preferred_element_type=F32)
    pc_ref[...] = jnp.where(sel > 0.0, rank, -1.0)
    eye = jnp.where(lax.broadcasted_iota(jnp.int32, (8, LANES), 0) == lax.broadcasted_iota(jnp.int32, (8, LANES), 1),
                    1.0, 0.0).astype(BF16)
    sel_t = _dot_nt(eye, selb)
    rank_t = jnp.dot(sel_t.astype(BF16), jnp.where(r < c, 1.0, 0.0).astype(BF16), preferred_element_type=F32)
    pt_ref[...] = jnp.where(sel_t > 0.0, rank_t, -1.0)
    row8 = lax.broadcasted_iota(jnp.int32, (8, LANES), 0)
    cnt_ref[...] = jnp.where(row8 == 0, jnp.sum(sel, axis=0, keepdims=True),
                             jnp.where(row8 == 1, rank[tm // 2:tm // 2 + 1, :], 0.0))


def _route(x, ys, w_out, g, router_pad):
    tm = MOE_TM
    nt = TOKENS // tm
    tok = lambda width: pl.BlockSpec((tm, width), lambda i: (i, 0))
    return pl.pallas_call(
        _route_kernel,
        grid=(nt,),
        in_specs=_mix_specs(tm) + [pl.BlockSpec((1, D_MODEL), lambda i: (0, 0)),
                                   pl.BlockSpec((D_MODEL, LANES), lambda i: (0, 0))],
        out_specs=[tok(D_MODEL), tok(D_MODEL), tok(LANES), tok(LANES), pl.BlockSpec((8, tm), lambda i: (0, i)),
                   pl.BlockSpec((8, LANES), lambda i: (i, 0))],
        out_shape=[jax.ShapeDtypeStruct((TOKENS, D_MODEL), F32), jax.ShapeDtypeStruct((TOKENS, D_MODEL), BF16),
                   jax.ShapeDtypeStruct((TOKENS, LANES), F32), jax.ShapeDtypeStruct((TOKENS, LANES), F32),
                   jax.ShapeDtypeStruct((8, TOKENS), F32), jax.ShapeDtypeStruct((nt * 8, LANES), F32)],
        compiler_params=_params("parallel"),
        name="moe_route",
    )(x, *ys, w_out, g, router_pad)


def _moe_kernel(cnt_ref, h_ref, pt_ref, pc_ref, comb_ref, wg_ref, wu_ref, wd_ref, o_ref, xg_ref, yacc_ref, acc_ref):
    tm = h_ref.shape[0]
    sm, oc = MOE_STATIC_ROWS, MOE_CHUNK
    i = pl.program_id(0)
    e = pl.program_id(1)
    f = pl.program_id(2)
    n_over = jnp.maximum(cnt_ref[i * N_EXPERTS + e] - sm + oc - 1, 0) // oc

    def overflow(body):
        def step(c, carry):
            body(pl.multiple_of(sm + c * oc, oc), oc)
            return carry
        lax.fori_loop(0, n_over, step, 0)

    @pl.when((e == 0) & (f == 0))
    def _():
        acc_ref[...] = jnp.zeros(acc_ref.shape, F32)

    @pl.when(f == 0)
    def _():
        pt = pt_ref[pl.ds(e, 1), :]

        def gather(row0, nrows):
            ridx = lax.broadcasted_iota(jnp.int32, (nrows, tm), 0) + row0
            onehot = jnp.where(ridx.astype(F32) == pt, 1.0, 0.0).astype(BF16)
            xg_ref[pl.ds(row0, nrows), :] = jnp.dot(onehot, h_ref[...], preferred_element_type=F32).astype(BF16)
            yacc_ref[pl.ds(row0, nrows), :] = jnp.zeros((nrows, D_MODEL), F32)

        gather(0, sm)
        overflow(gather)

    def ffn(row0, nrows):
        rows = pl.ds(row0, nrows)
        yacc_ref[rows, :] += _swiglu(xg_ref[rows, :], wg_ref[0], wu_ref[0], wd_ref[0])

    ffn(0, sm)
    overflow(ffn)

    @pl.when(f == pl.num_programs(2) - 1)
    def _():
        lane = lax.broadcasted_iota(jnp.int32, (tm, LANES), 1)
        mine = lane == e
        pc = jnp.sum(jnp.where(mine, pc_ref[...], 0.0), axis=-1, keepdims=True)
        gate = jnp.sum(jnp.where(mine, comb_ref[...], 0.0), axis=-1, keepdims=True)

        def scatter(row0, nrows, tok0=0, ntok=tm):
            cidx = lax.broadcasted_iota(jnp.int32, (ntok, nrows), 1) + row0
            onehot = jnp.where(cidx.astype(F32) == pc[tok0:tok0 + ntok], 1.0, 0.0).astype(BF16)
            y = yacc_ref[pl.ds(row0, nrows), :].astype(BF16)
            acc_ref[tok0:tok0 + ntok, :] += jnp.dot(onehot, y, preferred_element_type=F32) * gate[tok0:tok0 + ntok]

        half = tm // 2
        n_half = cnt_ref[pl.num_programs(0) * N_EXPERTS + i * N_EXPERTS + e]
        windowed = (n_half >= MOE_HALF_MIN) & (n_half <= MOE_HALF_MAX)

        @pl.when(windowed)
        def _():
            scatter(0, MOE_HALF_MAX, 0, half)
            scatter(MOE_HALF_MIN, sm - MOE_HALF_MIN, half, half)

        @pl.when(jnp.logical_not(windowed))
        def _():
            scatter(0, sm)

        overflow(scatter)

    @pl.when((e == pl.num_programs(1) - 1) & (f == pl.num_programs(2) - 1))
    def _():
        o_ref[...] = acc_ref[...].astype(o_ref.dtype)


def _moe(h, pt, pc, comb, counts, wg, wu, wd):
    tm, tf = MOE_TM, FFN_DIM // 2
    grid_spec = pltpu.PrefetchScalarGridSpec(
        num_scalar_prefetch=1,
        grid=(TOKENS // tm, N_EXPERTS, FFN_DIM // tf),
        in_specs=[pl.BlockSpec((tm, D_MODEL), lambda i, e, f, cnt: (i, 0)),
                  pl.BlockSpec((8, tm), lambda i, e, f, cnt: (0, i)),
                  pl.BlockSpec((tm, LANES), lambda i, e, f, cnt: (i, 0)),
                  pl.BlockSpec((tm, LANES), lambda i, e, f, cnt: (i, 0)),
                  pl.BlockSpec((1, D_MODEL, tf), lambda i, e, f, cnt: (e, 0, f)),
                  pl.BlockSpec((1, D_MODEL, tf), lambda i, e, f, cnt: (e, 0, f)),
                  pl.BlockSpec((1, tf, D_MODEL), lambda i, e, f, cnt: (e, f, 0))],
        out_specs=pl.BlockSpec((tm, D_MODEL), lambda i, e, f, cnt: (i, 0)),
        scratch_shapes=[pltpu.VMEM((tm, D_MODEL), BF16), pltpu.VMEM((tm, D_MODEL), F32),
                        pltpu.VMEM((tm, D_MODEL), F32)])
    return pl.pallas_call(
        _moe_kernel,
        grid_spec=grid_spec,
        out_shape=jax.ShapeDtypeStruct((TOKENS, D_MODEL), BF16),
        compiler_params=_params("parallel", "arbitrary", "arbitrary"),
        name="moe_ffn",
    )(counts, h, pt, pc, comb, wg, wu, wd)


def _final_kernel(x_ref, y_ref, g_ref, o_ref):
    o_ref[...] = _rms(x_ref[...] + y_ref[...].astype(F32), g_ref[...])


def _final(x, y, g):
    tm = 1024
    return pl.pallas_call(
        _final_kernel,
        grid=(TOKENS // tm,),
        in_specs=[pl.BlockSpec((tm, D_MODEL), lambda i: (i, 0)), pl.BlockSpec((tm, D_MODEL), lambda i: (i, 0)),
                  pl.BlockSpec((1, D_MODEL), lambda i: (0, 0))],
        out_specs=pl.BlockSpec((tm, D_MODEL), lambda i: (i, 0)),
        out_shape=jax.ShapeDtypeStruct((TOKENS, D_MODEL), F32),
        compiler_params=_params("parallel"),
        name="final_norm",
    )(x, y, g)


def kernel(x, positions, attn_norm, w_in, b_forget, mla_q_norm, w_q_up, mla_kv_norm, w_kv_up, sinks, w_out, ffn_norm, dense_w_gate, dense_w_up, dense_w_down, router, moe_w_gate, moe_w_up, moe_w_down, final_norm):
    assert x.shape == (BATCH, SEQ, D_MODEL) and positions.shape == (BATCH, SEQ)
    p_idx, p_sgn = _proj_columns()
    q_idx, q_sgn = _mla_q_columns()
    kv_idx, kv_sgn = _mla_kv_columns()
    mix_rows = _mix_rows()
    cos, sin = _rope_tables(positions)
    xt = x.reshape(TOKENS, D_MODEL).astype(F32)
    for layer in range(DEPTH):
        w = _take_columns(w_in[layer], p_idx, p_sgn).astype(BF16)
        wq = _take_columns(w_q_up[layer], q_idx, q_sgn).astype(BF16)
        wkv = _take_columns(w_kv_up[layer], kv_idx, kv_sgn).astype(BF16)
        fq, fk, fv, fl, pb, qc, kc, vc, pd = _project(
            xt, attn_norm[layer].reshape(1, D_MODEL), w, mla_q_norm[layer].reshape(1, MLA_Q_LORA), wq,
            mla_kv_norm[layer].reshape(1, MLA_KV_LORA), wkv, cos, sin)
        y_a = _flash(fq, _fox_keys(fl, b_forget[layer], fk), fv, "fox_attention")
        y_b = _swa(pb, sinks[layer])
        y_c = _flash(qc, kc, vc, "mla_attention")
        y_d = _dilated(pd)
        cuts = [0] + [r for r in range(1, len(mix_rows)) if mix_rows[r] != mix_rows[r - 1] + 1] + [len(mix_rows)]
        w_o = jnp.concatenate([w_out[layer][int(mix_rows[a]):int(mix_rows[a]) + (b - a)]
                               for a, b in zip(cuts[:-1], cuts[1:])], axis=0).astype(BF16)
        ys = (y_a, y_b, y_c, y_d)
        j = layer // 2
        g = ffn_norm[layer].reshape(1, D_MODEL)
        if layer % 2 == 0:
            xt = _ffn(xt, ys, w_o, g, dense_w_gate, dense_w_up, dense_w_down, j)
        else:
            assert layer == DEPTH - 1
            router_pad = jnp.zeros((D_MODEL, LANES), F32).at[:, :N_EXPERTS].set(router[j])
            xt, h, comb, pc, pt, cnt = _route(xt, ys, w_o, g, router_pad)
            counts = jnp.concatenate([cnt[0::8, :N_EXPERTS].reshape(-1), cnt[1::8, :N_EXPERTS].reshape(-1)])
            counts = counts.astype(jnp.int32)
            y = _moe(h, pt, pc, comb, counts, moe_w_gate[j].astype(BF16), moe_w_up[j].astype(BF16),
                     moe_w_down[j].astype(BF16))
            xt = _final(xt, y, final_norm.reshape(1, D_MODEL))
    return xt.reshape(BATCH, SEQ, D_MODEL)
```

```python
import functools

import numpy as np
import jax
import jax.numpy as jnp
from jax import lax
from jax.experimental import pallas as pl
from jax.experimental.pallas import tpu as pltpu

D_MODEL = 1024
BATCH = 2
SEQ = 8192
DEPTH = 2
TOKENS = BATCH * SEQ
HEAD_DIM = 64
BAND = 128
NORM_EPS = 1e-6
FOX_HEADS = 4
SWA_Q_HEADS = 4
SWA_KV_HEADS = 2
SWA_WINDOW = 128
MLA_HEADS = 4
MLA_Q_LORA = 256
MLA_KV_LORA = 128
MLA_NOPE_DIM = 64
MLA_ROPE_DIM = 32
MLA_V_DIM = 64
ROPE_THETA = 10000.0
DIL_HEADS = 4
DIL_PATTERNS = ((128, 1), (512, 4), (2048, 16))
DIL_BLOCK = BAND * max(d for _, d in DIL_PATTERNS)
FFN_DIM = 3584
N_EXPERTS = 8
LANES = 128
NEG = -1e30
VMEM_LIMIT = 56 * 1024 * 1024

_OFF = np.cumsum([0, 256, 256, 256, 4, 256, 128, 128, 256, 128, 32, 256, 256, 256])
(_A_Q, _A_K, _A_V, _A_F, _B_Q, _B_K, _B_V, _C_Q, _C_KV, _C_KR, _D_Q, _D_K, _D_V) = _OFF[:13].tolist()
N_PROJ_BLOCKS = 22
LOG2E = 1.4426950408889634

BF16 = jnp.bfloat16
F32 = jnp.float32


def _alibi_slopes():
    n = SWA_Q_HEADS + DIL_HEADS
    return [2.0 ** (-8.0 * i / n) for i in range(1, n + 1)]


def _proj_columns():
    idx = np.zeros((N_PROJ_BLOCKS * LANES,), np.int32)
    sgn = np.zeros((N_PROJ_BLOCKS * LANES,), np.float32)

    def put(dst, src, n, sign=1.0):
        idx[dst:dst + n] = np.arange(src, src + n)
        sgn[dst:dst + n] = sign

    put(0, _A_Q, 256); put(256, _A_K, 256); put(512, _A_V, 256)
    for blk, heads in ((6, (0, 2)), (7, (1, 3))):
        for half, h in enumerate(heads):
            put(blk * LANES + half * HEAD_DIM, _B_Q + h * HEAD_DIM, HEAD_DIM)
    put(8 * LANES, _B_K, 128); put(9 * LANES, _B_V, 128)
    put(10 * LANES, _C_Q, 256); put(12 * LANES, _C_KV, 128)
    half = MLA_ROPE_DIM // 2
    put(13 * LANES + MLA_NOPE_DIM, _C_KR, MLA_ROPE_DIM)
    put(14 * LANES + MLA_NOPE_DIM, _C_KR + half, half, -1.0)
    put(14 * LANES + MLA_NOPE_DIM + half, _C_KR, half)
    put(15 * LANES, _D_Q, 256); put(17 * LANES, _D_K, 256); put(19 * LANES, _D_V, 256)
    put(21 * LANES, _A_F, FOX_HEADS)
    return idx, sgn


def _bias_lane0(h):
    return HEAD_DIM if h % 2 == 0 else 0


def _bias_placement():
    place = np.zeros((3 * LANES, FOX_HEADS * LANES), np.float32)
    for piece in range(3):
        for h in range(FOX_HEADS):
            place[piece * LANES + h, h * LANES + _bias_lane0(h) + piece] = 1.0
    return place


def _mla_q_columns():
    idx = np.zeros((8 * LANES,), np.int32)
    sgn = np.zeros((8 * LANES,), np.float32)
    half = MLA_ROPE_DIM // 2
    dq = MLA_NOPE_DIM + MLA_ROPE_DIM
    for h in range(MLA_HEADS):
        a = h * LANES
        idx[a:a + dq] = np.arange(h * dq, (h + 1) * dq); sgn[a:a + dq] = 1.0
        b = (MLA_HEADS + h) * LANES + MLA_NOPE_DIM
        r = h * dq + MLA_NOPE_DIM
        idx[b:b + half] = np.arange(r + half, r + 2 * half); sgn[b:b + half] = -1.0
        idx[b + half:b + 2 * half] = np.arange(r, r + half); sgn[b + half:b + 2 * half] = 1.0
    return idx, sgn


def _mla_kv_columns():
    idx = np.zeros((6 * LANES,), np.int32)
    sgn = np.zeros((6 * LANES,), np.float32)
    dkv = MLA_NOPE_DIM + MLA_V_DIM
    for h in range(MLA_HEADS):
        idx[h * LANES:h * LANES + MLA_NOPE_DIM] = np.arange(h * dkv, h * dkv + MLA_NOPE_DIM)
        sgn[h * LANES:h * LANES + MLA_NOPE_DIM] = 1.0
        b = MLA_HEADS * LANES + h * MLA_V_DIM
        idx[b:b + MLA_V_DIM] = np.arange(h * dkv + MLA_NOPE_DIM, (h + 1) * dkv)
        sgn[b:b + MLA_V_DIM] = 1.0
    return idx, sgn


def _take_columns(w, idx, sgn):
    parts = []
    a = 0
    while a < len(idx):
        b = a + 1
        while b < len(idx) and sgn[b] == sgn[a] and (sgn[a] == 0 or idx[b] == idx[b - 1] + 1):
            b += 1
        if sgn[a] == 0:
            parts.append(jnp.zeros((w.shape[0], b - a), w.dtype))
        else:
            piece = w[:, int(idx[a]):int(idx[a]) + (b - a)]
            parts.append(piece if sgn[a] > 0 else -piece)
        a = b
    return jnp.concatenate(parts, axis=1)


def _mix_rows():
    rows = np.arange(4 * 256)
    b = 256
    perm = np.concatenate([np.arange(b + h * HEAD_DIM, b + (h + 1) * HEAD_DIM) for h in (0, 2, 1, 3)])
    rows[b:b + 256] = perm
    return rows


def _rms(x, g):
    return x * lax.rsqrt(jnp.mean(x * x, axis=-1, keepdims=True) + NORM_EPS) * g


def _dot_nt(a, b):
    return lax.dot_general(a, b, (((1,), (1,)), ((), ())), preferred_element_type=F32)


def _lane_tile(x, width):
    return x if width == LANES else jnp.concatenate([x] * (width // LANES), axis=1)


def _params(*sem):
    return pltpu.CompilerParams(dimension_semantics=sem, vmem_limit_bytes=VMEM_LIMIT)


def _rope_table_kernel(pos_ref, invf_ref, cos_ref, sin_ref):
    ang = pos_ref[...].astype(F32) * invf_ref[...]
    cos_ref[...] = jnp.cos(ang)
    sin_ref[...] = jnp.sin(ang)


def _rope_tables(positions):
    tm = 2048
    half = MLA_ROPE_DIM // 2
    invf = np.zeros((1, LANES), np.float32)
    f = (ROPE_THETA ** (-np.arange(half, dtype=np.float32) / np.float32(half))).astype(np.float32)
    invf[0, MLA_NOPE_DIM:MLA_NOPE_DIM + half] = f
    invf[0, MLA_NOPE_DIM + half:MLA_NOPE_DIM + 2 * half] = f
    return pl.pallas_call(
        _rope_table_kernel,
        grid=(TOKENS // tm,),
        in_specs=[pl.BlockSpec((tm, 1), lambda i: (i, 0)), pl.BlockSpec((1, LANES), lambda i: (0, 0))],
        out_specs=[pl.BlockSpec((tm, LANES), lambda i: (i, 0))] * 2,
        out_shape=[jax.ShapeDtypeStruct((TOKENS, LANES), F32)] * 2,
        compiler_params=_params("parallel"),
        name="rope_tables",
    )(positions.reshape(TOKENS, 1), jnp.asarray(invf))


def _proj_kernel(x_ref, g_ref, w_ref, qn_ref, wq_ref, kvn_ref, wkv_ref, cos_ref, sin_ref,
                 fq_ref, fk_ref, fv_ref, fl_ref, pb_ref, qc_ref, kc_ref, vc_ref, pd_ref):
    hb = _rms(x_ref[...], g_ref[...]).astype(BF16)
    res = jnp.dot(hb, w_ref[...], preferred_element_type=F32)
    lane = lax.broadcasted_iota(jnp.int32, (x_ref.shape[0], LANES), 1)
    low = lane < HEAD_DIM

    def blk(j, n=1):
        return res[:, j * LANES:(j + n) * LANES]

    def own(h):
        return low if h % 2 == 0 else ~low

    qscale = HEAD_DIM ** -0.5 * LOG2E
    for h in range(FOX_HEADS):
        ones = (lane >= _bias_lane0(h)) & (lane < _bias_lane0(h) + 3)
        fq_ref[h] = jnp.where(own(h), blk(h // 2) * qscale, jnp.where(ones, 1.0, 0.0)).astype(BF16)
        fv_ref[h] = jnp.where(own(h), blk(4 + h // 2), 1.0).astype(BF16)
    for j in range(2):
        fk_ref[j] = blk(2 + j).astype(BF16)
    fl_ref[...] = blk(21)
    for j in range(2):
        pb_ref[j] = (blk(6 + j) * qscale).astype(BF16)
    pb_ref[2] = blk(8).astype(BF16)
    pb_ref[3] = blk(9).astype(BF16)

    cos = cos_ref[...]
    sin = sin_ref[...]
    cq = _rms(blk(10, 2), qn_ref[...]).astype(BF16)
    qab = jnp.dot(cq, wq_ref[...], preferred_element_type=F32)
    mla_scale = (MLA_NOPE_DIM + MLA_ROPE_DIM) ** -0.5
    for h in range(MLA_HEADS):
        qa = qab[:, h * LANES:(h + 1) * LANES]
        qb = qab[:, (MLA_HEADS + h) * LANES:(MLA_HEADS + h + 1) * LANES]
        qc_ref[h] = ((qa * cos + qb * sin) * (mla_scale * LOG2E)).astype(BF16)
    ckv = _rms(blk(12), kvn_ref[...]).astype(BF16)
    kv = jnp.dot(ckv, wkv_ref[...], preferred_element_type=F32)
    k_rot = blk(13) * cos + blk(14) * sin
    for h in range(MLA_HEADS):
        kc_ref[h] = (kv[:, h * LANES:(h + 1) * LANES] + k_rot).astype(BF16)
        v_pair = kv[:, (MLA_HEADS + h // 2) * LANES:(MLA_HEADS + h // 2 + 1) * LANES]
        vc_ref[h] = jnp.where(own(h), v_pair, 1.0).astype(BF16)

    for j in range(2):
        pd_ref[j] = blk(15 + j) * qscale
    for j in range(2, 6):
        pd_ref[j] = blk(15 + j)


def _project(x, g, w, qn, wq, kvn, wkv, cos, sin):
    tm = 512
    full = lambda shape: pl.BlockSpec(shape, lambda i: (0,) * len(shape))
    out_blk = lambda n: pl.BlockSpec((n, tm, LANES), lambda i: (0, i, 0))
    out_sds = lambda n, dt: jax.ShapeDtypeStruct((n, TOKENS, LANES), dt)
    tok_blk = pl.BlockSpec((tm, LANES), lambda i: (i, 0))
    return pl.pallas_call(
        _proj_kernel,
        grid=(TOKENS // tm,),
        in_specs=[pl.BlockSpec((tm, D_MODEL), lambda i: (i, 0)), full((1, D_MODEL)),
                  full((D_MODEL, N_PROJ_BLOCKS * LANES)),
                  full((1, MLA_Q_LORA)), full((MLA_Q_LORA, 8 * LANES)),
                  full((1, MLA_KV_LORA)), full((MLA_KV_LORA, 6 * LANES)), tok_blk, tok_blk],
        out_specs=[out_blk(4), out_blk(2), out_blk(4), tok_blk, out_blk(4), out_blk(4), out_blk(4), out_blk(4),
                   out_blk(6)],
        out_shape=[out_sds(4, BF16), out_sds(2, BF16), out_sds(4, BF16), jax.ShapeDtypeStruct((TOKENS, LANES), F32),
                   out_sds(4, BF16), out_sds(4, BF16), out_sds(4, BF16), out_sds(4, BF16), out_sds(6, F32)],
        compiler_params=_params("parallel"),
        name="in_proj",
    )(x, g, w, qn, wq, kvn, wkv, cos, sin)


def _split3(x):
    hi = x.astype(BF16)
    r1 = x - hi.astype(F32)
    mid = r1.astype(BF16)
    return hi, mid, (r1 - mid.astype(F32)).astype(BF16)


def _fox_keys_kernel(fl_ref, b_ref, k_ref, place_ref, kf_ref, carry_ref, *, cs):
    @pl.when(pl.program_id(1) == 0)
    def _():
        carry_ref[...] = jnp.zeros_like(carry_ref)

    lf = jax.nn.log_sigmoid(fl_ref[...] + b_ref[...])
    row = lax.broadcasted_iota(jnp.int32, (cs, cs), 0)
    col = lax.broadcasted_iota(jnp.int32, (cs, cs), 1)
    tri = jnp.where(col <= row, 1.0, 0.0).astype(BF16)
    cum = carry_ref[0:1, :] + sum(jnp.dot(tri, piece, preferred_element_type=F32) for piece in _split3(lf))
    carry_ref[0:1, :] = cum[cs - 1:cs, :]
    pieces = jnp.concatenate(_split3(cum * (-LOG2E)), axis=1)
    placed = jnp.dot(pieces, place_ref[...], preferred_element_type=F32).astype(BF16)
    low = lax.broadcasted_iota(jnp.int32, (cs, LANES), 1) < HEAD_DIM
    for h in range(FOX_HEADS):
        kf_ref[h] = jnp.where(low if h % 2 == 0 else ~low, k_ref[h // 2], placed[:, h * LANES:(h + 1) * LANES])


def _fox_keys(fl, b_forget, fk):
    cs = 512
    nc = SEQ // cs
    b_row = jnp.zeros((1, LANES), F32).at[0, :FOX_HEADS].set(b_forget.astype(F32))
    return pl.pallas_call(
        functools.partial(_fox_keys_kernel, cs=cs),
        grid=(BATCH, nc),
        in_specs=[pl.BlockSpec((cs, LANES), lambda b, i: (b * nc + i, 0)),
                  pl.BlockSpec((1, LANES), lambda b, i: (0, 0)),
                  pl.BlockSpec((2, cs, LANES), lambda b, i: (0, b * nc + i, 0)),
                  pl.BlockSpec((3 * LANES, FOX_HEADS * LANES), lambda b, i: (0, 0))],
        out_specs=pl.BlockSpec((FOX_HEADS, cs, LANES), lambda b, i: (0, b * nc + i, 0)),
        out_shape=jax.ShapeDtypeStruct((FOX_HEADS, TOKENS, LANES), BF16),
        scratch_shapes=[pltpu.VMEM((8, LANES), F32)],
        compiler_params=_params("parallel", "arbitrary"),
        name="fox_keys",
    )(fl, b_row, fk, jnp.asarray(_bias_placement(), BF16))


def _flash_kernel(*refs, tq, tk, n_cast, n_alias):
    q_ref, k_ref, v_ref = refs[:3]
    o_ref = refs[3 + n_cast + n_alias]
    acc_ref, m_ref = refs[-2:]
    for src_ref, dst_ref in zip(refs[3:3 + n_cast], refs[4 + n_cast + n_alias:4 + 2 * n_cast + n_alias]):
        dst_ref[...] = src_ref[...].astype(dst_ref.dtype)
    i = pl.program_id(2)
    m_ref[...] = jnp.full(m_ref.shape, NEG, F32)
    acc_ref[...] = jnp.zeros(acc_ref.shape, F32)

    def step(kb, row0, diagonal):
        ks = pl.multiple_of(kb * tk, tk)
        rows = pl.ds(row0, tq - row0)
        for h in range(2):
            s = _dot_nt(q_ref[h, rows, :], k_ref[h, pl.ds(ks, tk), :])
            if diagonal:
                r = lax.broadcasted_iota(jnp.int32, s.shape, 0)
                c = lax.broadcasted_iota(jnp.int32, s.shape, 1)
                s = jnp.where(c <= r, s, NEG)
            m_old = m_ref[h, rows, :]
            m_new = jnp.maximum(m_old, jnp.max(s, axis=-1, keepdims=True))
            p = jnp.exp2(s - _lane_tile(m_new, tk)).astype(BF16)
            acc_ref[h, rows, :] = (jnp.exp2(m_old - m_new) * acc_ref[h, rows, :]
                                   + jnp.dot(p, v_ref[h, pl.ds(ks, tk), :], preferred_element_type=F32))
            m_ref[h, rows, :] = m_new

    def body(kb, carry):
        step(kb, 0, False)
        return carry

    per_q = tq // tk
    lax.fori_loop(0, i * per_q, body, 0)
    for u in range(per_q):
        step(i * per_q + u, u * tk, True)
    outs = [acc_ref[h] / pltpu.roll(acc_ref[h], HEAD_DIM, axis=1) for h in range(2)]
    low = lax.broadcasted_iota(jnp.int32, (tq, LANES), 1) < HEAD_DIM
    o_ref[0] = jnp.where(low, outs[0], outs[1]).astype(o_ref.dtype)


FLASH_CALLS = 2 * DEPTH


def _flash(q, k, v, name, w_f32, w_bf16, part):
    tq, tk = 2048, 512
    nq = SEQ // tq
    steps = BATCH * 2 * nq
    experts_per_call = N_EXPERTS // FLASH_CALLS
    chunks = steps // experts_per_call
    assert N_EXPERTS % FLASH_CALLS == 0 and steps % experts_per_call == 0
    assert all(w.shape[1] % (8 * chunks) == 0 for w in w_f32)

    def slab(b, j, i):
        s = (b * 2 + j) * nq + i
        return part * experts_per_call + s // chunks, s % chunks, 0

    w_specs = [pl.BlockSpec((1, w.shape[1] // chunks, w.shape[2]), slab) for w in w_f32]
    aliased = [] if w_bf16 is None else list(w_bf16)
    kv_spec = pl.BlockSpec((2, SEQ, LANES), lambda b, j, i: (j, b, 0))
    outs = pl.pallas_call(
        functools.partial(_flash_kernel, tq=tq, tk=tk, n_cast=len(w_f32), n_alias=len(aliased)),
        grid=(BATCH, 2, nq),
        in_specs=[pl.BlockSpec((2, tq, LANES), lambda b, j, i: (j, b * nq + i, 0)), kv_spec, kv_spec] + w_specs
                 + [pl.BlockSpec(memory_space=pl.ANY)] * len(aliased),
        out_specs=[pl.BlockSpec((1, tq, LANES), lambda b, j, i: (j, b * nq + i, 0))] + w_specs,
        out_shape=[jax.ShapeDtypeStruct((2, TOKENS, LANES), BF16)]
                  + [jax.ShapeDtypeStruct(w.shape, BF16) for w in w_f32],
        input_output_aliases={3 + len(w_f32) + n: 1 + n for n in range(len(aliased))},
        scratch_shapes=[pltpu.VMEM((2, tq, LANES), F32)] * 2,
        compiler_params=_params("parallel", "parallel", "parallel"),
        name=name,
    )(q, k, v, *w_f32, *aliased)
    return outs[0], outs[1:]


def _band_mask_bias(span, slope_step, first):
    qi = lax.broadcasted_iota(jnp.int32, (BAND, 2 * BAND), 0)
    kj = lax.broadcasted_iota(jnp.int32, (BAND, 2 * BAND), 1)
    dist = BAND + qi - kj
    ok = (dist >= 0) & (dist <= span) & (kj >= jnp.where(first, BAND, 0))
    return jnp.where(ok, dist.astype(F32) * (-slope_step), NEG)


def _band_pair(q, kk, vv, mask_bias):
    low = lax.broadcasted_iota(jnp.int32, (BAND, LANES), 1) < HEAD_DIM
    low_kv = lax.broadcasted_iota(jnp.int32, (2 * BAND, LANES), 1) < HEAD_DIM
    accs, ms = [], []
    for half in range(2):
        qm = jnp.where(low if half == 0 else ~low, q, jnp.zeros_like(q))
        vh = jnp.where(low_kv if half == 0 else ~low_kv, vv, jnp.ones_like(vv))
        s = _dot_nt(qm, kk) + mask_bias[half]
        m = jnp.max(s, axis=-1, keepdims=True)
        accs.append(jnp.dot(jnp.exp2(s - m).astype(BF16), vh, preferred_element_type=F32))
        ms.append(m)
    l = pltpu.roll(jnp.where(low, accs[1], accs[0]), HEAD_DIM, axis=1)
    return jnp.where(low, accs[0], accs[1]) / l, jnp.where(low, ms[0], ms[1]) + jnp.log2(l)


def _swa_kernel(q_ref, k_ref, kp_ref, v_ref, vp_ref, sink_ref, o_ref, kk_ref, vv_ref, *, tb, slopes):
    n = pl.program_id(1)
    kk_ref[0:BAND] = kp_ref[0]
    kk_ref[BAND:] = k_ref[0]
    vv_ref[0:BAND] = vp_ref[0]
    vv_ref[BAND:] = v_ref[0]
    mask_bias = [[_band_mask_bias(SWA_WINDOW - 1, slope * LOG2E, first) for slope in slopes]
                 for first in (n == 0, False)]
    for c in range(tb // BAND):
        kk = kk_ref[c * BAND:(c + 2) * BAND]
        vv = vv_ref[c * BAND:(c + 2) * BAND]
        mb = mask_bias[0 if c == 0 else 1]
        for jb in range(2):
            o, lse2 = _band_pair(q_ref[jb, c * BAND:(c + 1) * BAND, :], kk, vv, (mb[jb], mb[jb + 2]))
            o = o / (1.0 + jnp.exp2(sink_ref[jb:jb + 1, :] - lse2))
            o_ref[jb, c * BAND:(c + 1) * BAND, :] = o.astype(o_ref.dtype)


def _swa(pb, sinks):
    tb = 512
    nb = SEQ // tb
    r = tb // BAND
    s = sinks.astype(F32) * LOG2E
    sink_lanes = jnp.stack([jnp.concatenate([jnp.full((HEAD_DIM,), s[jb]), jnp.full((HEAD_DIM,), s[jb + 2])])
                            for jb in range(2)])
    cur = lambda blk: pl.BlockSpec((1, tb, LANES), lambda b, n: (blk, b * nb + n, 0))
    prev = lambda blk: pl.BlockSpec((1, BAND, LANES), lambda b, n: (blk, jnp.maximum((b * nb + n) * r - 1, 0), 0))
    return pl.pallas_call(
        functools.partial(_swa_kernel, tb=tb, slopes=_alibi_slopes()[:SWA_Q_HEADS]),
        grid=(BATCH, nb),
        in_specs=[pl.BlockSpec((2, tb, LANES), lambda b, n: (0, b * nb + n, 0)), cur(2), prev(2), cur(3), prev(3),
                  pl.BlockSpec((2, LANES), lambda b, n: (0, 0))],
        out_specs=pl.BlockSpec((2, tb, LANES), lambda b, n: (0, b * nb + n, 0)),
        out_shape=jax.ShapeDtypeStruct((2, TOKENS, LANES), BF16),
        scratch_shapes=[pltpu.VMEM((tb + BAND, LANES), BF16)] * 2,
        compiler_params=_params("parallel", "parallel"),
        name="swa_attention",
    )(pb, pb, pb, pb, pb, sink_lanes)


def _dil_kernel(q_ref, k_ref, kp_ref, v_ref, vp_ref, o_ref, kk_ref, vv_ref, po_ref, pl_ref, *, slopes):
    pair = pl.program_id(1)
    n = pl.program_id(2)
    kk_ref[0:DIL_BLOCK] = kp_ref[0]
    kk_ref[DIL_BLOCK:] = k_ref[0]
    vv_ref[0:DIL_BLOCK] = vp_ref[0]
    vv_ref[DIL_BLOCK:] = v_ref[0]
    units = DIL_BLOCK // BAND
    for p, (window, dil) in enumerate(DIL_PATTERNS):
        steps = [jnp.where(pair == 0, slopes[half], slopes[2 + half]) * (dil * LOG2E) for half in range(2)]
        mask_bias = [[_band_mask_bias(window // dil, step, first) for step in steps] for first in (n == 0, False)]

        def unit(u, carry, p=p, dil=dil, mb=None):
            start = (u // dil) * (BAND * dil) + u % dil
            if dil == 1:
                rows = pl.ds(pl.multiple_of(start, BAND), BAND)
                krows = pl.ds(pl.multiple_of(DIL_BLOCK + start - BAND, BAND), 2 * BAND)
            else:
                rows = pl.ds(start, BAND, stride=dil)
                krows = pl.ds(DIL_BLOCK + start - BAND * dil, 2 * BAND, stride=dil)
            o, lse2 = _band_pair(q_ref[0, rows, :].astype(BF16), kk_ref[krows, :].astype(BF16),
                                 vv_ref[krows, :].astype(BF16), mb)
            po_ref[p, rows, :] = o
            pl_ref[p, rows, :] = lse2
            return carry

        lax.fori_loop(0, dil, functools.partial(unit, mb=mask_bias[0]), 0, unroll=min(dil, 8))
        if dil < units:
            lax.fori_loop(dil, units, functools.partial(unit, mb=mask_bias[1]), 0, unroll=min(units - dil, 8))
    chunk = 256

    def merge(t, carry):
        rows = pl.ds(pl.multiple_of(t * chunk, chunk), chunk)
        lse = pl_ref[:, rows, :]
        w = jnp.exp2(lse - jnp.max(lse, axis=0, keepdims=True))
        o_ref[0, rows, :] = (jnp.sum(w * po_ref[:, rows, :], axis=0) / jnp.sum(w, axis=0)).astype(o_ref.dtype)
        return carry

    lax.fori_loop(0, DIL_BLOCK // chunk, merge, 0)


def _dilated(pd):
    nb = SEQ // DIL_BLOCK
    cur = lambda off: pl.BlockSpec((1, DIL_BLOCK, LANES), lambda b, j, n: (off + j, b * nb + n, 0))
    prev = lambda off: pl.BlockSpec((1, DIL_BLOCK, LANES),
                                    lambda b, j, n: (off + j, b * nb + jnp.maximum(n - 1, 0), 0))
    return pl.pallas_call(
        functools.partial(_dil_kernel, slopes=_alibi_slopes()[SWA_Q_HEADS:]),
        grid=(BATCH, 2, nb),
        in_specs=[cur(0), cur(2), prev(2), cur(4), prev(4)],
        out_specs=pl.BlockSpec((1, DIL_BLOCK, LANES), lambda b, j, n: (j, b * nb + n, 0)),
        out_shape=jax.ShapeDtypeStruct((2, TOKENS, LANES), BF16),
        scratch_shapes=[pltpu.VMEM((2 * DIL_BLOCK, LANES), F32)] * 2
                       + [pltpu.VMEM((len(DIL_PATTERNS), DIL_BLOCK, LANES), F32)] * 2,
        compiler_params=_params("parallel", "parallel", "parallel"),
        name="dilated_attention",
    )(pd, pd, pd, pd, pd)


def _mix_residual(x_ref, y_refs, w_ref):
    mixed = jnp.concatenate([y[j] for y in y_refs for j in range(2)], axis=1)
    return x_ref[...] + jnp.dot(mixed, w_ref[...], preferred_element_type=F32)


def _mix_specs(tm):
    return ([pl.BlockSpec((tm, D_MODEL), lambda i, *_: (i, 0))]
            + [pl.BlockSpec((2, tm, LANES), lambda i, *_: (0, i, 0))] * 4
            + [pl.BlockSpec((D_MODEL, D_MODEL), lambda i, *_: (0, 0))])


def _swiglu(h, wg, wu, wd):
    gate = jnp.dot(h, wg, preferred_element_type=F32)
    up = jnp.dot(h, wu, preferred_element_type=F32)
    act = (gate * jax.nn.sigmoid(gate) * up).astype(BF16)
    return jnp.dot(act, wd, preferred_element_type=F32)


def _ffn_kernel(x_ref, ya_ref, yb_ref, yc_ref, yd_ref, wo_ref, g_ref, wg_ref, wu_ref, wd_ref, o_ref, h_ref, acc_ref):
    f = pl.program_id(1)

    @pl.when(f == 0)
    def _():
        x = _mix_residual(x_ref, (ya_ref, yb_ref, yc_ref, yd_ref), wo_ref)
        h_ref[...] = _rms(x, g_ref[...]).astype(BF16)
        acc_ref[...] = x

    acc_ref[...] += _swiglu(h_ref[...], wg_ref[...].astype(BF16), wu_ref[...].astype(BF16), wd_ref[...].astype(BF16))

    @pl.when(f == pl.num_programs(1) - 1)
    def _():
        o_ref[...] = acc_ref[...]


def _ffn(x, ys, w_out, g, wg, wu, wd, j):
    tm, tf = 1024, 512
    return pl.pallas_call(
        _ffn_kernel,
        grid=(TOKENS // tm, FFN_DIM // tf),
        in_specs=_mix_specs(tm) + [
                  pl.BlockSpec((1, D_MODEL), lambda i, f: (0, 0)),
                  pl.BlockSpec((None, D_MODEL, tf), lambda i, f: (j, 0, f)),
                  pl.BlockSpec((None, D_MODEL, tf), lambda i, f: (j, 0, f)),
                  pl.BlockSpec((None, tf, D_MODEL), lambda i, f: (j, f, 0))],
        out_specs=pl.BlockSpec((tm, D_MODEL), lambda i, f: (i, 0)),
        out_shape=jax.ShapeDtypeStruct((TOKENS, D_MODEL), F32),
        scratch_shapes=[pltpu.VMEM((tm, D_MODEL), BF16), pltpu.VMEM((tm, D_MODEL), F32)],
        compiler_params=_params("parallel", "arbitrary"),
        name="dense_ffn",
    )(x, *ys, w_out, g, wg, wu, wd)


MOE_TM = 1024
MOE_STATIC_ROWS = 288
MOE_CHUNK = 32
MOE_HALF_MIN = 96
MOE_HALF_MAX = 160


def _route_kernel(x_ref, ya_ref, yb_ref, yc_ref, yd_ref, wo_ref, g_ref, router_ref,
                  x1_ref, h_ref, comb_ref, pc_ref, pt_ref, cnt_ref):
    tm = x_ref.shape[0]
    x1 = _mix_residual(x_ref, (ya_ref, yb_ref, yc_ref, yd_ref), wo_ref)
    x1_ref[...] = x1
    h = _rms(x1, g_ref[...])
    h_ref[...] = h.astype(BF16)
    lane = lax.broadcasted_iota(jnp.int32, (tm, LANES), 1).astype(F32)
    h_hi, h_lo, _ = _split3(h)
    r_hi, r_lo, _ = _split3(router_ref[...])
    logits = (jnp.dot(h_hi, r_hi, preferred_element_type=F32) + jnp.dot(h_hi, r_lo, preferred_element_type=F32)
              + jnp.dot(h_lo, r_hi, preferred_element_type=F32))
    logits = jnp.where(lane < N_EXPERTS, logits, NEG)
    m1 = jnp.max(logits, axis=-1, keepdims=True)
    i1 = jnp.min(jnp.where(logits == m1, lane, float(LANES)), axis=-1, keepdims=True)
    rest = jnp.where(lane == i1, NEG, logits)
    m2 = jnp.max(rest, axis=-1, keepdims=True)
    i2 = jnp.min(jnp.where(rest == m2, lane, float(LANES)), axis=-1, keepdims=True)
    t = jnp.exp(m2 - m1)
    comb_ref[...] = jnp.where(lane == i1, 1.0 / (1.0 + t), 0.0) + jnp.where(lane == i2, t / (1.0 + t), 0.0)
    sel = jnp.where(lane == i1, 1.0, jnp.where(lane == i2, 1.0, 0.0))
    selb = sel.astype(BF16)
    r = lax.broadcasted_iota(jnp.int32, (tm, tm), 0)
    c = lax.broadcasted_iota(jnp.int32, (tm, tm), 1)
    rank = jnp.dot(jnp.where(c < r, 1.0, 0.0).astype(BF16), selb, preferred_element_type=F32)
    pc_ref[...] = jnp.where(sel > 0.0, rank, -1.0)
    eye = jnp.where(lax.broadcasted_iota(jnp.int32, (8, LANES), 0) == lax.broadcasted_iota(jnp.int32, (8, LANES), 1),
                    1.0, 0.0).astype(BF16)
    sel_t = _dot_nt(eye, selb)
    rank_t = jnp.dot(sel_t.astype(BF16), jnp.where(r < c, 1.0, 0.0).astype(BF16), preferred_element_type=F32)
    pt_ref[...] = jnp.where(sel_t > 0.0, rank_t, -1.0)
    row8 = lax.broadcasted_iota(jnp.int32, (8, LANES), 0)
    cnt_ref[...] = jnp.where(row8 == 0, jnp.sum(sel, axis=0, keepdims=True),
                             jnp.where(row8 == 1, rank[tm // 2:tm // 2 + 1, :], 0.0))


def _route(x, ys, w_out, g, router_pad):
    tm = MOE_TM
    nt = TOKENS // tm
    tok = lambda width: pl.BlockSpec((tm, width), lambda i: (i, 0))
    return pl.pallas_call(
        _route_kernel,
        grid=(nt,),
        in_specs=_mix_specs(tm) + [pl.BlockSpec((1, D_MODEL), lambda i: (0, 0)),
                                   pl.BlockSpec((D_MODEL, LANES), lambda i: (0, 0))],
        out_specs=[tok(D_MODEL), tok(D_MODEL), tok(LANES), tok(LANES), pl.BlockSpec((8, tm), lambda i: (0, i)),
                   pl.BlockSpec((8, LANES), lambda i: (i, 0))],
        out_shape=[jax.ShapeDtypeStruct((TOKENS, D_MODEL), F32), jax.ShapeDtypeStruct((TOKENS, D_MODEL), BF16),
                   jax.ShapeDtypeStruct((TOKENS, LANES), F32), jax.ShapeDtypeStruct((TOKENS, LANES), F32),
                   jax.ShapeDtypeStruct((8, TOKENS), F32), jax.ShapeDtypeStruct((nt * 8, LANES), F32)],
        compiler_params=_params("parallel"),
        name="moe_route",
    )(x, *ys, w_out, g, router_pad)


def _moe_kernel(cnt_ref, h_ref, pt_ref, pc_ref, comb_ref, wg_ref, wu_ref, wd_ref, o_ref, xg_ref, yacc_ref, acc_ref):
    tm = h_ref.shape[0]
    sm, oc = MOE_STATIC_ROWS, MOE_CHUNK
    i = pl.program_id(0)
    e = pl.program_id(1)
    f = pl.program_id(2)
    n_over = jnp.maximum(cnt_ref[i * N_EXPERTS + e] - sm + oc - 1, 0) // oc

    def overflow(body):
        def step(c, carry):
            body(pl.multiple_of(sm + c * oc, oc), oc)
            return carry
        lax.fori_loop(0, n_over, step, 0)

    @pl.when((e == 0) & (f == 0))
    def _():
        acc_ref[...] = jnp.zeros(acc_ref.shape, F32)

    @pl.when(f == 0)
    def _():
        pt = pt_ref[pl.ds(e, 1), :]

        def gather(row0, nrows):
            ridx = lax.broadcasted_iota(jnp.int32, (nrows, tm), 0) + row0
            onehot = jnp.where(ridx.astype(F32) == pt, 1.0, 0.0).astype(BF16)
            xg_ref[pl.ds(row0, nrows), :] = jnp.dot(onehot, h_ref[...], preferred_element_type=F32).astype(BF16)
            yacc_ref[pl.ds(row0, nrows), :] = jnp.zeros((nrows, D_MODEL), F32)

        gather(0, sm)
        overflow(gather)

    def ffn(row0, nrows):
        rows = pl.ds(row0, nrows)
        yacc_ref[rows, :] += _swiglu(xg_ref[rows, :], wg_ref[0], wu_ref[0], wd_ref[0])

    ffn(0, sm)
    overflow(ffn)

    @pl.when(f == pl.num_programs(2) - 1)
    def _():
        lane = lax.broadcasted_iota(jnp.int32, (tm, LANES), 1)
        mine = lane == e
        pc = jnp.sum(jnp.where(mine, pc_ref[...], 0.0), axis=-1, keepdims=True)
        gate = jnp.sum(jnp.where(mine, comb_ref[...], 0.0), axis=-1, keepdims=True)

        def scatter(row0, nrows, tok0=0, ntok=tm):
            cidx = lax.broadcasted_iota(jnp.int32, (ntok, nrows), 1) + row0
            onehot = jnp.where(cidx.astype(F32) == pc[tok0:tok0 + ntok], 1.0, 0.0).astype(BF16)
            y = yacc_ref[pl.ds(row0, nrows), :].astype(BF16)
            acc_ref[tok0:tok0 + ntok, :] += jnp.dot(onehot, y, preferred_element_type=F32) * gate[tok0:tok0 + ntok]

        half = tm // 2
        n_half = cnt_ref[pl.num_programs(0) * N_EXPERTS + i * N_EXPERTS + e]
        windowed = (n_half >= MOE_HALF_MIN) & (n_half <= MOE_HALF_MAX)

        @pl.when(windowed)
        def _():
            scatter(0, MOE_HALF_MAX, 0, half)
            scatter(MOE_HALF_MIN, sm - MOE_HALF_MIN, half, half)

        @pl.when(jnp.logical_not(windowed))
        def _():
            scatter(0, sm)

        overflow(scatter)

    @pl.when((e == pl.num_programs(1) - 1) & (f == pl.num_programs(2) - 1))
    def _():
        o_ref[...] = acc_ref[...].astype(o_ref.dtype)


def _moe(h, pt, pc, comb, counts, wg, wu, wd):
    tm, tf = MOE_TM, FFN_DIM // 2
    grid_spec = pltpu.PrefetchScalarGridSpec(
        num_scalar_prefetch=1,
        grid=(TOKENS // tm, N_EXPERTS, FFN_DIM // tf),
        in_specs=[pl.BlockSpec((tm, D_MODEL), lambda i, e, f, cnt: (i, 0)),
                  pl.BlockSpec((8, tm), lambda i, e, f, cnt: (0, i)),
                  pl.BlockSpec((tm, LANES), lambda i, e, f, cnt: (i, 0)),
                  pl.BlockSpec((tm, LANES), lambda i, e, f, cnt: (i, 0)),
                  pl.BlockSpec((1, D_MODEL, tf), lambda i, e, f, cnt: (e, 0, f)),
                  pl.BlockSpec((1, D_MODEL, tf), lambda i, e, f, cnt: (e, 0, f)),
                  pl.BlockSpec((1, tf, D_MODEL), lambda i, e, f, cnt: (e, f, 0))],
        out_specs=pl.BlockSpec((tm, D_MODEL), lambda i, e, f, cnt: (i, 0)),
        scratch_shapes=[pltpu.VMEM((tm, D_MODEL), BF16), pltpu.VMEM((tm, D_MODEL), F32),
                        pltpu.VMEM((tm, D_MODEL), F32)])
    return pl.pallas_call(
        _moe_kernel,
        grid_spec=grid_spec,
        out_shape=jax.ShapeDtypeStruct((TOKENS, D_MODEL), BF16),
        compiler_params=_params("parallel", "arbitrary", "arbitrary"),
        name="moe_ffn",
    )(counts, h, pt, pc, comb, wg, wu, wd)


def _final_kernel(x_ref, y_ref, g_ref, o_ref):
    o_ref[...] = _rms(x_ref[...] + y_ref[...].astype(F32), g_ref[...])


def _final(x, y, g):
    tm = 1024
    return pl.pallas_call(
        _final_kernel,
        grid=(TOKENS // tm,),
        in_specs=[pl.BlockSpec((tm, D_MODEL), lambda i: (i, 0)), pl.BlockSpec((tm, D_MODEL), lambda i: (i, 0)),
                  pl.BlockSpec((1, D_MODEL), lambda i: (0, 0))],
        out_specs=pl.BlockSpec((tm, D_MODEL), lambda i: (i, 0)),
        out_shape=jax.ShapeDtypeStruct((TOKENS, D_MODEL), F32),
        compiler_params=_params("parallel"),
        name="final_norm",
    )(x, y, g)


def kernel(x, positions, attn_norm, w_in, b_forget, mla_q_norm, w_q_up, mla_kv_norm, w_kv_up, sinks, w_out, ffn_norm, dense_w_gate, dense_w_up, dense_w_down, router, moe_w_gate, moe_w_up, moe_w_down, final_norm):
    assert x.shape == (BATCH, SEQ, D_MODEL) and positions.shape == (BATCH, SEQ)
    p_idx, p_sgn = _proj_columns()
    q_idx, q_sgn = _mla_q_columns()
    kv_idx, kv_sgn = _mla_kv_columns()
    mix_rows = _mix_rows()
    cos, sin = _rope_tables(positions)
    xt = x.reshape(TOKENS, D_MODEL).astype(F32)
    assert DEPTH == 2 and moe_w_gate.shape[0] == 1
    moe_w_f32 = (moe_w_gate[0], moe_w_up[0], moe_w_down[0])
    moe_w = None
    for layer in range(DEPTH):
        w = _take_columns(w_in[layer], p_idx, p_sgn).astype(BF16)
        wq = _take_columns(w_q_up[layer], q_idx, q_sgn).astype(BF16)
        wkv = _take_columns(w_kv_up[layer], kv_idx, kv_sgn).astype(BF16)
        fq, fk, fv, fl, pb, qc, kc, vc, pd = _project(
            xt, attn_norm[layer].reshape(1, D_MODEL), w, mla_q_norm[layer].reshape(1, MLA_Q_LORA), wq,
            mla_kv_norm[layer].reshape(1, MLA_KV_LORA), wkv, cos, sin)
        y_a, moe_w = _flash(fq, _fox_keys(fl, b_forget[layer], fk), fv, "fox_attention", moe_w_f32, moe_w, 2 * layer)
        y_b = _swa(pb, sinks[layer])
        y_c, moe_w = _flash(qc, kc, vc, "mla_attention", moe_w_f32, moe_w, 2 * layer + 1)
        y_d = _dilated(pd)
        cuts = [0] + [r for r in range(1, len(mix_rows)) if mix_rows[r] != mix_rows[r - 1] + 1] + [len(mix_rows)]
        w_o = jnp.concatenate([w_out[layer][int(mix_rows[a]):int(mix_rows[a]) + (b - a)]
                               for a, b in zip(cuts[:-1], cuts[1:])], axis=0).astype(BF16)
        ys = (y_a, y_b, y_c, y_d)
        j = layer // 2
        g = ffn_norm[layer].reshape(1, D_MODEL)
        if layer % 2 == 0:
            xt = _ffn(xt, ys, w_o, g, dense_w_gate, dense_w_up, dense_w_down, j)
        else:
            assert layer == DEPTH - 1
            router_pad = jnp.zeros((D_MODEL, LANES), F32).at[:, :N_EXPERTS].set(router[j])
            xt, h, comb, pc, pt, cnt = _route(xt, ys, w_o, g, router_pad)
            counts = jnp.concatenate([cnt[0::8, :N_EXPERTS].reshape(-1), cnt[1::8, :N_EXPERTS].reshape(-1)])
            counts = counts.astype(jnp.int32)
            y = _moe(h, pt, pc, comb, counts, *moe_w)
            xt = _final(xt, y, final_norm.reshape(1, D_MODEL))
    return xt.reshape(BATCH, SEQ, D_MODEL)
```

```python
import functools

import numpy as np
import jax
import jax.numpy as jnp
from jax import lax
from jax.experimental import pallas as pl
from jax.experimental.pallas import tpu as pltpu

D_MODEL = 1024
BATCH = 2
SEQ = 8192
DEPTH = 2
TOKENS = BATCH * SEQ
HEAD_DIM = 64
BAND = 128
NORM_EPS = 1e-6
FOX_HEADS = 4
SWA_Q_HEADS = 4
SWA_KV_HEADS = 2
SWA_WINDOW = 128
MLA_HEADS = 4
MLA_Q_LORA = 256
MLA_KV_LORA = 128
MLA_NOPE_DIM = 64
MLA_ROPE_DIM = 32
MLA_V_DIM = 64
ROPE_THETA = 10000.0
DIL_HEADS = 4
DIL_PATTERNS = ((128, 1), (512, 4), (2048, 16))
DIL_BLOCK = BAND * max(d for _, d in DIL_PATTERNS)
FFN_DIM = 3584
N_EXPERTS = 8
LANES = 128
NEG = -1e30
VMEM_LIMIT = 56 * 1024 * 1024

_OFF = np.cumsum([0, 256, 256, 256, 4, 256, 128, 128, 256, 128, 32, 256, 256, 256])
(_A_Q, _A_K, _A_V, _A_F, _B_Q, _B_K, _B_V, _C_Q, _C_KV, _C_KR, _D_Q, _D_K, _D_V) = _OFF[:13].tolist()
N_PROJ_BLOCKS = 22
LOG2E = 1.4426950408889634

BF16 = jnp.bfloat16
F32 = jnp.float32


def _alibi_slopes():
    n = SWA_Q_HEADS + DIL_HEADS
    return [2.0 ** (-8.0 * i / n) for i in range(1, n + 1)]


def _proj_columns():
    idx = np.zeros((N_PROJ_BLOCKS * LANES,), np.int32)
    sgn = np.zeros((N_PROJ_BLOCKS * LANES,), np.float32)

    def put(dst, src, n, sign=1.0):
        idx[dst:dst + n] = np.arange(src, src + n)
        sgn[dst:dst + n] = sign

    put(0, _A_Q, 256); put(256, _A_K, 256); put(512, _A_V, 256)
    for blk, heads in ((6, (0, 2)), (7, (1, 3))):
        for half, h in enumerate(heads):
            put(blk * LANES + half * HEAD_DIM, _B_Q + h * HEAD_DIM, HEAD_DIM)
    put(8 * LANES, _B_K, 128); put(9 * LANES, _B_V, 128)
    put(10 * LANES, _C_Q, 256); put(12 * LANES, _C_KV, 128)
    half = MLA_ROPE_DIM // 2
    put(13 * LANES + MLA_NOPE_DIM, _C_KR, MLA_ROPE_DIM)
    put(14 * LANES + MLA_NOPE_DIM, _C_KR + half, half, -1.0)
    put(14 * LANES + MLA_NOPE_DIM + half, _C_KR, half)
    put(15 * LANES, _D_Q, 256); put(17 * LANES, _D_K, 256); put(19 * LANES, _D_V, 256)
    put(21 * LANES, _A_F, FOX_HEADS)
    return idx, sgn


def _bias_lane0(h):
    return HEAD_DIM if h % 2 == 0 else 0


def _bias_placement():
    place = np.zeros((3 * LANES, FOX_HEADS * LANES), np.float32)
    for piece in range(3):
        for h in range(FOX_HEADS):
            place[piece * LANES + h, h * LANES + _bias_lane0(h) + piece] = 1.0
    return place


def _mla_q_columns():
    idx = np.zeros((8 * LANES,), np.int32)
    sgn = np.zeros((8 * LANES,), np.float32)
    half = MLA_ROPE_DIM // 2
    dq = MLA_NOPE_DIM + MLA_ROPE_DIM
    for h in range(MLA_HEADS):
        a = h * LANES
        idx[a:a + dq] = np.arange(h * dq, (h + 1) * dq); sgn[a:a + dq] = 1.0
        b = (MLA_HEADS + h) * LANES + MLA_NOPE_DIM
        r = h * dq + MLA_NOPE_DIM
        idx[b:b + half] = np.arange(r + half, r + 2 * half); sgn[b:b + half] = -1.0
        idx[b + half:b + 2 * half] = np.arange(r, r + half); sgn[b + half:b + 2 * half] = 1.0
    return idx, sgn


def _mla_kv_columns():
    idx = np.zeros((6 * LANES,), np.int32)
    sgn = np.zeros((6 * LANES,), np.float32)
    dkv = MLA_NOPE_DIM + MLA_V_DIM
    for h in range(MLA_HEADS):
        idx[h * LANES:h * LANES + MLA_NOPE_DIM] = np.arange(h * dkv, h * dkv + MLA_NOPE_DIM)
        sgn[h * LANES:h * LANES + MLA_NOPE_DIM] = 1.0
        b = MLA_HEADS * LANES + h * MLA_V_DIM
        idx[b:b + MLA_V_DIM] = np.arange(h * dkv + MLA_NOPE_DIM, (h + 1) * dkv)
        sgn[b:b + MLA_V_DIM] = 1.0
    return idx, sgn


def _take_columns(w, idx, sgn):
    parts = []
    a = 0
    while a < len(idx):
        b = a + 1
        while b < len(idx) and sgn[b] == sgn[a] and (sgn[a] == 0 or idx[b] == idx[b - 1] + 1):
            b += 1
        if sgn[a] == 0:
            parts.append(jnp.zeros((w.shape[0], b - a), w.dtype))
        else:
            piece = w[:, int(idx[a]):int(idx[a]) + (b - a)]
            parts.append(piece if sgn[a] > 0 else -piece)
        a = b
    return jnp.concatenate(parts, axis=1)


def _mix_rows():
    rows = np.arange(4 * 256)
    b = 256
    perm = np.concatenate([np.arange(b + h * HEAD_DIM, b + (h + 1) * HEAD_DIM) for h in (0, 2, 1, 3)])
    rows[b:b + 256] = perm
    return rows


def _rms(x, g):
    return x * lax.rsqrt(jnp.mean(x * x, axis=-1, keepdims=True) + NORM_EPS) * g


def _dot_nt(a, b):
    return lax.dot_general(a, b, (((1,), (1,)), ((), ())), preferred_element_type=F32)


def _lane_tile(x, width):
    return x if width == LANES else jnp.concatenate([x] * (width // LANES), axis=1)


def _params(*sem):
    return pltpu.CompilerParams(dimension_semantics=sem, vmem_limit_bytes=VMEM_LIMIT)


def _rope_table_kernel(pos_ref, invf_ref, cos_ref, sin_ref):
    ang = pos_ref[...].astype(F32) * invf_ref[...]
    cos_ref[...] = jnp.cos(ang)
    sin_ref[...] = jnp.sin(ang)


def _rope_tables(positions):
    tm = 2048
    half = MLA_ROPE_DIM // 2
    invf = np.zeros((1, LANES), np.float32)
    f = (ROPE_THETA ** (-np.arange(half, dtype=np.float32) / np.float32(half))).astype(np.float32)
    invf[0, MLA_NOPE_DIM:MLA_NOPE_DIM + half] = f
    invf[0, MLA_NOPE_DIM + half:MLA_NOPE_DIM + 2 * half] = f
    return pl.pallas_call(
        _rope_table_kernel,
        grid=(TOKENS // tm,),
        in_specs=[pl.BlockSpec((tm, 1), lambda i: (i, 0)), pl.BlockSpec((1, LANES), lambda i: (0, 0))],
        out_specs=[pl.BlockSpec((tm, LANES), lambda i: (i, 0))] * 2,
        out_shape=[jax.ShapeDtypeStruct((TOKENS, LANES), F32)] * 2,
        compiler_params=_params("parallel"),
        name="rope_tables",
    )(positions.reshape(TOKENS, 1), jnp.asarray(invf))


def _proj_kernel(x_ref, g_ref, w_ref, qn_ref, wq_ref, kvn_ref, wkv_ref, cos_ref, sin_ref,
                 fq_ref, fk_ref, fv_ref, fl_ref, pb_ref, qc_ref, kc_ref, vc_ref, pd_ref):
    hb = _rms(x_ref[...], g_ref[...]).astype(BF16)
    res = jnp.dot(hb, w_ref[...], preferred_element_type=F32)
    lane = lax.broadcasted_iota(jnp.int32, (x_ref.shape[0], LANES), 1)
    low = lane < HEAD_DIM

    def blk(j, n=1):
        return res[:, j * LANES:(j + n) * LANES]

    def own(h):
        return low if h % 2 == 0 else ~low

    qscale = HEAD_DIM ** -0.5 * LOG2E
    for h in range(FOX_HEADS):
        ones = (lane >= _bias_lane0(h)) & (lane < _bias_lane0(h) + 3)
        fq_ref[h] = jnp.where(own(h), blk(h // 2) * qscale, jnp.where(ones, 1.0, 0.0)).astype(BF16)
        fv_ref[h] = jnp.where(own(h), blk(4 + h // 2), 1.0).astype(BF16)
    for j in range(2):
        fk_ref[j] = blk(2 + j).astype(BF16)
    fl_ref[...] = blk(21)
    for j in range(2):
        pb_ref[j] = (blk(6 + j) * qscale).astype(BF16)
    pb_ref[2] = blk(8).astype(BF16)
    pb_ref[3] = blk(9).astype(BF16)

    cos = cos_ref[...]
    sin = sin_ref[...]
    cq = _rms(blk(10, 2), qn_ref[...]).astype(BF16)
    qab = jnp.dot(cq, wq_ref[...], preferred_element_type=F32)
    mla_scale = (MLA_NOPE_DIM + MLA_ROPE_DIM) ** -0.5
    for h in range(MLA_HEADS):
        qa = qab[:, h * LANES:(h + 1) * LANES]
        qb = qab[:, (MLA_HEADS + h) * LANES:(MLA_HEADS + h + 1) * LANES]
        qc_ref[h] = ((qa * cos + qb * sin) * (mla_scale * LOG2E)).astype(BF16)
    ckv = _rms(blk(12), kvn_ref[...]).astype(BF16)
    kv = jnp.dot(ckv, wkv_ref[...], preferred_element_type=F32)
    k_rot = blk(13) * cos + blk(14) * sin
    for h in range(MLA_HEADS):
        kc_ref[h] = (kv[:, h * LANES:(h + 1) * LANES] + k_rot).astype(BF16)
        v_pair = kv[:, (MLA_HEADS + h // 2) * LANES:(MLA_HEADS + h // 2 + 1) * LANES]
        vc_ref[h] = jnp.where(own(h), v_pair, 1.0).astype(BF16)

    for j in range(2):
        pd_ref[j] = blk(15 + j) * qscale
    for j in range(2, 6):
        pd_ref[j] = blk(15 + j)


def _project(x, g, w, qn, wq, kvn, wkv, cos, sin):
    tm = 512
    full = lambda shape: pl.BlockSpec(shape, lambda i: (0,) * len(shape))
    out_blk = lambda n: pl.BlockSpec((n, tm, LANES), lambda i: (0, i, 0))
    out_sds = lambda n, dt: jax.ShapeDtypeStruct((n, TOKENS, LANES), dt)
    tok_blk = pl.BlockSpec((tm, LANES), lambda i: (i, 0))
    return pl.pallas_call(
        _proj_kernel,
        grid=(TOKENS // tm,),
        in_specs=[pl.BlockSpec((tm, D_MODEL), lambda i: (i, 0)), full((1, D_MODEL)),
                  full((D_MODEL, N_PROJ_BLOCKS * LANES)),
                  full((1, MLA_Q_LORA)), full((MLA_Q_LORA, 8 * LANES)),
                  full((1, MLA_KV_LORA)), full((MLA_KV_LORA, 6 * LANES)), tok_blk, tok_blk],
        out_specs=[out_blk(4), out_blk(2), out_blk(4), tok_blk, out_blk(4), out_blk(4), out_blk(4), out_blk(4),
                   out_blk(6)],
        out_shape=[out_sds(4, BF16), out_sds(2, BF16), out_sds(4, BF16), jax.ShapeDtypeStruct((TOKENS, LANES), F32),
                   out_sds(4, BF16), out_sds(4, BF16), out_sds(4, BF16), out_sds(4, BF16), out_sds(6, F32)],
        compiler_params=_params("parallel"),
        name="in_proj",
    )(x, g, w, qn, wq, kvn, wkv, cos, sin)


def _split3(x):
    hi = x.astype(BF16)
    r1 = x - hi.astype(F32)
    mid = r1.astype(BF16)
    return hi, mid, (r1 - mid.astype(F32)).astype(BF16)


def _fox_keys_kernel(fl_ref, b_ref, k_ref, place_ref, kf_ref, carry_ref, *, cs):
    @pl.when(pl.program_id(1) == 0)
    def _():
        carry_ref[...] = jnp.zeros_like(carry_ref)

    lf = jax.nn.log_sigmoid(fl_ref[...] + b_ref[...])
    row = lax.broadcasted_iota(jnp.int32, (cs, cs), 0)
    col = lax.broadcasted_iota(jnp.int32, (cs, cs), 1)
    tri = jnp.where(col <= row, 1.0, 0.0).astype(BF16)
    cum = carry_ref[0:1, :] + sum(jnp.dot(tri, piece, preferred_element_type=F32) for piece in _split3(lf))
    carry_ref[0:1, :] = cum[cs - 1:cs, :]
    pieces = jnp.concatenate(_split3(cum * (-LOG2E)), axis=1)
    placed = jnp.dot(pieces, place_ref[...], preferred_element_type=F32).astype(BF16)
    low = lax.broadcasted_iota(jnp.int32, (cs, LANES), 1) < HEAD_DIM
    for h in range(FOX_HEADS):
        kf_ref[h] = jnp.where(low if h % 2 == 0 else ~low, k_ref[h // 2], placed[:, h * LANES:(h + 1) * LANES])


def _fox_keys(fl, b_forget, fk):
    cs = 512
    nc = SEQ // cs
    b_row = jnp.zeros((1, LANES), F32).at[0, :FOX_HEADS].set(b_forget.astype(F32))
    return pl.pallas_call(
        functools.partial(_fox_keys_kernel, cs=cs),
        grid=(BATCH, nc),
        in_specs=[pl.BlockSpec((cs, LANES), lambda b, i: (b * nc + i, 0)),
                  pl.BlockSpec((1, LANES), lambda b, i: (0, 0)),
                  pl.BlockSpec((2, cs, LANES), lambda b, i: (0, b * nc + i, 0)),
                  pl.BlockSpec((3 * LANES, FOX_HEADS * LANES), lambda b, i: (0, 0))],
        out_specs=pl.BlockSpec((FOX_HEADS, cs, LANES), lambda b, i: (0, b * nc + i, 0)),
        out_shape=jax.ShapeDtypeStruct((FOX_HEADS, TOKENS, LANES), BF16),
        scratch_shapes=[pltpu.VMEM((8, LANES), F32)],
        compiler_params=_params("parallel", "arbitrary"),
        name="fox_keys",
    )(fl, b_row, fk, jnp.asarray(_bias_placement(), BF16))


FLASH_UNDERFLOW = -160.0


def _flash_kernel(*refs, tq, tk, n_cast, n_alias, decay_skip):
    q_ref, k_ref, v_ref = refs[:3]
    o_ref = refs[3 + n_cast + n_alias]
    acc_ref, m_ref, kmax_ref = refs[-3:]
    for src_ref, dst_ref in zip(refs[3:3 + n_cast], refs[4 + n_cast + n_alias:4 + 2 * n_cast + n_alias]):
        dst_ref[...] = src_ref[...].astype(dst_ref.dtype)
    i = pl.program_id(2)
    m_ref[...] = jnp.full(m_ref.shape, NEG, F32)
    acc_ref[...] = jnp.zeros(acc_ref.shape, F32)

    def sq_norms(x, h):
        lane_in = lax.broadcasted_iota(jnp.int32, (LANES, LANES), 0)
        ones = jnp.where((lane_in < HEAD_DIM) == (h == 0), 1.0, 0.0).astype(BF16)
        xf = x.astype(F32)
        return jnp.dot((xf * xf).astype(BF16), ones, preferred_element_type=F32)

    if decay_skip:
        @pl.when(i == 0)
        def _():
            for h in range(2):
                kmax_ref[h] = jnp.broadcast_to(jnp.max(sq_norms(k_ref[h], h), axis=0, keepdims=True), (8, LANES))

    def step(kb, row0, diagonal):
        ks = pl.multiple_of(kb * tk, tk)
        rows = pl.ds(row0, tq - row0)
        for h in range(2):
            s = _dot_nt(q_ref[h, rows, :], k_ref[h, pl.ds(ks, tk), :])
            if diagonal:
                r = lax.broadcasted_iota(jnp.int32, s.shape, 0)
                c = lax.broadcasted_iota(jnp.int32, s.shape, 1)
                s = jnp.where(c <= r, s, NEG)
            m_old = m_ref[h, rows, :]
            m_new = jnp.maximum(m_old, jnp.max(s, axis=-1, keepdims=True))
            p = jnp.exp2(s - _lane_tile(m_new, tk)).astype(BF16)
            acc_ref[h, rows, :] = (jnp.exp2(m_old - m_new) * acc_ref[h, rows, :]
                                   + jnp.dot(p, v_ref[h, pl.ds(ks, tk), :], preferred_element_type=F32))
            m_ref[h, rows, :] = m_new

    per_q = tq // tk
    if not decay_skip:
        def body(kb, carry):
            step(kb, 0, False)
            return carry

        lax.fori_loop(0, i * per_q, body, 0)
        for u in range(per_q):
            step(i * per_q + u, u * tk, True)
    else:
        for u in range(per_q):
            step(i * per_q + u, u * tk, True)
        slack = []
        for h in range(2):
            qk_bound = jnp.sqrt(sq_norms(q_ref[h], h) * kmax_ref[h, 0:1, :]) * 1.02
            slack.append(jnp.max(qk_bound - m_ref[h]))
        sub = lax.broadcasted_iota(jnp.int32, (16, LANES), 0)
        lane = lax.broadcasted_iota(jnp.int32, (16, LANES), 1)

        def last_bias(kb, h):
            rows = k_ref[h, pl.ds(pl.multiple_of((kb + 1) * tk - 16, 16), 16), :].astype(F32)
            pick = (sub == 15) & (lane >= _bias_lane0(h)) & (lane < _bias_lane0(h) + 3)
            return jnp.sum(jnp.where(pick, rows, 0.0))

        def visible(kb):
            kb0 = jnp.maximum(kb, 0)
            return (kb >= 0) & ((slack[0] + last_bias(kb0, 0) > FLASH_UNDERFLOW)
                                | (slack[1] + last_bias(kb0, 1) > FLASH_UNDERFLOW))

        def body(state):
            kb, _ = state
            step(kb, 0, False)
            return kb - 1, visible(kb - 1)

        lax.while_loop(lambda state: state[1], body, (i * per_q - 1, visible(i * per_q - 1)))
    outs = [acc_ref[h] / pltpu.roll(acc_ref[h], HEAD_DIM, axis=1) for h in range(2)]
    low = lax.broadcasted_iota(jnp.int32, (tq, LANES), 1) < HEAD_DIM
    o_ref[0] = jnp.where(low, outs[0], outs[1]).astype(o_ref.dtype)


FLASH_CALLS = 2 * DEPTH


def _flash(q, k, v, name, w_f32, w_bf16, part, decay_skip=False):
    tq, tk = 2048, 512
    nq = SEQ // tq
    steps = BATCH * 2 * nq
    experts_per_call = N_EXPERTS // FLASH_CALLS
    chunks = steps // experts_per_call
    assert N_EXPERTS % FLASH_CALLS == 0 and steps % experts_per_call == 0
    assert all(w.shape[1] % (8 * chunks) == 0 for w in w_f32)

    def slab(b, j, i):
        s = (b * 2 + j) * nq + i
        return part * experts_per_call + s // chunks, s % chunks, 0

    w_specs = [pl.BlockSpec((1, w.shape[1] // chunks, w.shape[2]), slab) for w in w_f32]
    aliased = [] if w_bf16 is None else list(w_bf16)
    kv_spec = pl.BlockSpec((2, SEQ, LANES), lambda b, j, i: (j, b, 0))
    outs = pl.pallas_call(
        functools.partial(_flash_kernel, tq=tq, tk=tk, n_cast=len(w_f32), n_alias=len(aliased),
                          decay_skip=decay_skip),
        grid=(BATCH, 2, nq),
        in_specs=[pl.BlockSpec((2, tq, LANES), lambda b, j, i: (j, b * nq + i, 0)), kv_spec, kv_spec] + w_specs
                 + [pl.BlockSpec(memory_space=pl.ANY)] * len(aliased),
        out_specs=[pl.BlockSpec((1, tq, LANES), lambda b, j, i: (j, b * nq + i, 0))] + w_specs,
        out_shape=[jax.ShapeDtypeStruct((2, TOKENS, LANES), BF16)]
                  + [jax.ShapeDtypeStruct(w.shape, BF16) for w in w_f32],
        input_output_aliases={3 + len(w_f32) + n: 1 + n for n in range(len(aliased))},
        scratch_shapes=[pltpu.VMEM((2, tq, LANES), F32)] * 2 + [pltpu.VMEM((2, 8, LANES), F32)],
        compiler_params=_params("parallel", "parallel", "arbitrary"),
        name=name,
    )(q, k, v, *w_f32, *aliased)
    return outs[0], outs[1:]


def _band_mask_bias(span, slope_step, first):
    qi = lax.broadcasted_iota(jnp.int32, (BAND, 2 * BAND), 0)
    kj = lax.broadcasted_iota(jnp.int32, (BAND, 2 * BAND), 1)
    dist = BAND + qi - kj
    ok = (dist >= 0) & (dist <= span) & (kj >= jnp.where(first, BAND, 0))
    return jnp.where(ok, dist.astype(F32) * (-slope_step), NEG)


def _band_pair(q, kk, vv, mask_bias):
    low = lax.broadcasted_iota(jnp.int32, (BAND, LANES), 1) < HEAD_DIM
    low_kv = lax.broadcasted_iota(jnp.int32, (2 * BAND, LANES), 1) < HEAD_DIM
    accs, ms = [], []
    for half in range(2):
        qm = jnp.where(low if half == 0 else ~low, q, jnp.zeros_like(q))
        vh = jnp.where(low_kv if half == 0 else ~low_kv, vv, jnp.ones_like(vv))
        s = _dot_nt(qm, kk) + mask_bias[half]
        m = jnp.max(s, axis=-1, keepdims=True)
        accs.append(jnp.dot(jnp.exp2(s - m).astype(BF16), vh, preferred_element_type=F32))
        ms.append(m)
    l = pltpu.roll(jnp.where(low, accs[1], accs[0]), HEAD_DIM, axis=1)
    return jnp.where(low, accs[0], accs[1]) / l, jnp.where(low, ms[0], ms[1]) + jnp.log2(l)


def _swa_kernel(q_ref, k_ref, kp_ref, v_ref, vp_ref, sink_ref, o_ref, kk_ref, vv_ref, *, tb, slopes):
    n = pl.program_id(1)
    kk_ref[0:BAND] = kp_ref[0]
    kk_ref[BAND:] = k_ref[0]
    vv_ref[0:BAND] = vp_ref[0]
    vv_ref[BAND:] = v_ref[0]
    mask_bias = [[_band_mask_bias(SWA_WINDOW - 1, slope * LOG2E, first) for slope in slopes]
                 for first in (n == 0, False)]
    for c in range(tb // BAND):
        kk = kk_ref[c * BAND:(c + 2) * BAND]
        vv = vv_ref[c * BAND:(c + 2) * BAND]
        mb = mask_bias[0 if c == 0 else 1]
        for jb in range(2):
            o, lse2 = _band_pair(q_ref[jb, c * BAND:(c + 1) * BAND, :], kk, vv, (mb[jb], mb[jb + 2]))
            o = o / (1.0 + jnp.exp2(sink_ref[jb:jb + 1, :] - lse2))
            o_ref[jb, c * BAND:(c + 1) * BAND, :] = o.astype(o_ref.dtype)


def _swa(pb, sinks):
    tb = 512
    nb = SEQ // tb
    r = tb // BAND
    s = sinks.astype(F32) * LOG2E
    sink_lanes = jnp.stack([jnp.concatenate([jnp.full((HEAD_DIM,), s[jb]), jnp.full((HEAD_DIM,), s[jb + 2])])
                            for jb in range(2)])
    cur = lambda blk: pl.BlockSpec((1, tb, LANES), lambda b, n: (blk, b * nb + n, 0))
    prev = lambda blk: pl.BlockSpec((1, BAND, LANES), lambda b, n: (blk, jnp.maximum((b * nb + n) * r - 1, 0), 0))
    return pl.pallas_call(
        functools.partial(_swa_kernel, tb=tb, slopes=_alibi_slopes()[:SWA_Q_HEADS]),
        grid=(BATCH, nb),
        in_specs=[pl.BlockSpec((2, tb, LANES), lambda b, n: (0, b * nb + n, 0)), cur(2), prev(2), cur(3), prev(3),
                  pl.BlockSpec((2, LANES), lambda b, n: (0, 0))],
        out_specs=pl.BlockSpec((2, tb, LANES), lambda b, n: (0, b * nb + n, 0)),
        out_shape=jax.ShapeDtypeStruct((2, TOKENS, LANES), BF16),
        scratch_shapes=[pltpu.VMEM((tb + BAND, LANES), BF16)] * 2,
        compiler_params=_params("parallel", "parallel"),
        name="swa_attention",
    )(pb, pb, pb, pb, pb, sink_lanes)


def _dil_kernel(q_ref, k_ref, kp_ref, v_ref, vp_ref, o_ref, kk_ref, vv_ref, po_ref, pl_ref, *, slopes):
    pair = pl.program_id(1)
    n = pl.program_id(2)
    kk_ref[0:DIL_BLOCK] = kp_ref[0]
    kk_ref[DIL_BLOCK:] = k_ref[0]
    vv_ref[0:DIL_BLOCK] = vp_ref[0]
    vv_ref[DIL_BLOCK:] = v_ref[0]
    units = DIL_BLOCK // BAND
    for p, (window, dil) in enumerate(DIL_PATTERNS):
        steps = [jnp.where(pair == 0, slopes[half], slopes[2 + half]) * (dil * LOG2E) for half in range(2)]
        mask_bias = [[_band_mask_bias(window // dil, step, first) for step in steps] for first in (n == 0, False)]

        def unit(u, carry, p=p, dil=dil, mb=None):
            start = (u // dil) * (BAND * dil) + u % dil
            if dil == 1:
                rows = pl.ds(pl.multiple_of(start, BAND), BAND)
                krows = pl.ds(pl.multiple_of(DIL_BLOCK + start - BAND, BAND), 2 * BAND)
            else:
                rows = pl.ds(start, BAND, stride=dil)
                krows = pl.ds(DIL_BLOCK + start - BAND * dil, 2 * BAND, stride=dil)
            o, lse2 = _band_pair(q_ref[0, rows, :].astype(BF16), kk_ref[krows, :].astype(BF16),
                                 vv_ref[krows, :].astype(BF16), mb)
            po_ref[p, rows, :] = o
            pl_ref[p, rows, :] = lse2
            return carry

        lax.fori_loop(0, dil, functools.partial(unit, mb=mask_bias[0]), 0, unroll=min(dil, 8))
        if dil < units:
            lax.fori_loop(dil, units, functools.partial(unit, mb=mask_bias[1]), 0, unroll=min(units - dil, 8))
    chunk = 256

    def merge(t, carry):
        rows = pl.ds(pl.multiple_of(t * chunk, chunk), chunk)
        lse = pl_ref[:, rows, :]
        w = jnp.exp2(lse - jnp.max(lse, axis=0, keepdims=True))
        o_ref[0, rows, :] = (jnp.sum(w * po_ref[:, rows, :], axis=0) / jnp.sum(w, axis=0)).astype(o_ref.dtype)
        return carry

    lax.fori_loop(0, DIL_BLOCK // chunk, merge, 0)


def _dilated(pd):
    nb = SEQ // DIL_BLOCK
    cur = lambda off: pl.BlockSpec((1, DIL_BLOCK, LANES), lambda b, j, n: (off + j, b * nb + n, 0))
    prev = lambda off: pl.BlockSpec((1, DIL_BLOCK, LANES),
                                    lambda b, j, n: (off + j, b * nb + jnp.maximum(n - 1, 0), 0))
    return pl.pallas_call(
        functools.partial(_dil_kernel, slopes=_alibi_slopes()[SWA_Q_HEADS:]),
        grid=(BATCH, 2, nb),
        in_specs=[cur(0), cur(2), prev(2), cur(4), prev(4)],
        out_specs=pl.BlockSpec((1, DIL_BLOCK, LANES), lambda b, j, n: (j, b * nb + n, 0)),
        out_shape=jax.ShapeDtypeStruct((2, TOKENS, LANES), BF16),
        scratch_shapes=[pltpu.VMEM((2 * DIL_BLOCK, LANES), F32)] * 2
                       + [pltpu.VMEM((len(DIL_PATTERNS), DIL_BLOCK, LANES), F32)] * 2,
        compiler_params=_params("parallel", "parallel", "parallel"),
        name="dilated_attention",
    )(pd, pd, pd, pd, pd)


def _mix_residual(x_ref, y_refs, w_ref):
    mixed = jnp.concatenate([y[j] for y in y_refs for j in range(2)], axis=1)
    return x_ref[...] + jnp.dot(mixed, w_ref[...], preferred_element_type=F32)


def _mix_specs(tm):
    return ([pl.BlockSpec((tm, D_MODEL), lambda i, *_: (i, 0))]
            + [pl.BlockSpec((2, tm, LANES), lambda i, *_: (0, i, 0))] * 4
            + [pl.BlockSpec((D_MODEL, D_MODEL), lambda i, *_: (0, 0))])


def _swiglu(h, wg, wu, wd):
    gate = jnp.dot(h, wg, preferred_element_type=F32)
    up = jnp.dot(h, wu, preferred_element_type=F32)
    act = (gate * jax.nn.sigmoid(gate) * up).astype(BF16)
    return jnp.dot(act, wd, preferred_element_type=F32)


def _ffn_kernel(x_ref, ya_ref, yb_ref, yc_ref, yd_ref, wo_ref, g_ref, wg_ref, wu_ref, wd_ref, o_ref, h_ref, acc_ref):
    f = pl.program_id(1)

    @pl.when(f == 0)
    def _():
        x = _mix_residual(x_ref, (ya_ref, yb_ref, yc_ref, yd_ref), wo_ref)
        h_ref[...] = _rms(x, g_ref[...]).astype(BF16)
        acc_ref[...] = x

    acc_ref[...] += _swiglu(h_ref[...], wg_ref[...].astype(BF16), wu_ref[...].astype(BF16), wd_ref[...].astype(BF16))

    @pl.when(f == pl.num_programs(1) - 1)
    def _():
        o_ref[...] = acc_ref[...]


def _ffn(x, ys, w_out, g, wg, wu, wd, j):
    tm, tf = 1024, 512
    return pl.pallas_call(
        _ffn_kernel,
        grid=(TOKENS // tm, FFN_DIM // tf),
        in_specs=_mix_specs(tm) + [
                  pl.BlockSpec((1, D_MODEL), lambda i, f: (0, 0)),
                  pl.BlockSpec((None, D_MODEL, tf), lambda i, f: (j, 0, f)),
                  pl.BlockSpec((None, D_MODEL, tf), lambda i, f: (j, 0, f)),
                  pl.BlockSpec((None, tf, D_MODEL), lambda i, f: (j, f, 0))],
        out_specs=pl.BlockSpec((tm, D_MODEL), lambda i, f: (i, 0)),
        out_shape=jax.ShapeDtypeStruct((TOKENS, D_MODEL), F32),
        scratch_shapes=[pltpu.VMEM((tm, D_MODEL), BF16), pltpu.VMEM((tm, D_MODEL), F32)],
        compiler_params=_params("parallel", "arbitrary"),
        name="dense_ffn",
    )(x, *ys, w_out, g, wg, wu, wd)


MOE_TM = 1024
MOE_STATIC_ROWS = 288
MOE_CHUNK = 32
MOE_HALF_MIN = 96
MOE_HALF_MAX = 160


def _route_kernel(x_ref, ya_ref, yb_ref, yc_ref, yd_ref, wo_ref, g_ref, router_ref,
                  x1_ref, h_ref, comb_ref, pc_ref, pt_ref, cnt_ref):
    tm = x_ref.shape[0]
    x1 = _mix_residual(x_ref, (ya_ref, yb_ref, yc_ref, yd_ref), wo_ref)
    x1_ref[...] = x1
    h = _rms(x1, g_ref[...])
    h_ref[...] = h.astype(BF16)
    lane = lax.broadcasted_iota(jnp.int32, (tm, LANES), 1).astype(F32)
    h_hi, h_lo, _ = _split3(h)
    r_hi, r_lo, _ = _split3(router_ref[...])
    logits = (jnp.dot(h_hi, r_hi, preferred_element_type=F32) + jnp.dot(h_hi, r_lo, preferred_element_type=F32)
              + jnp.dot(h_lo, r_hi, preferred_element_type=F32))
    logits = jnp.where(lane < N_EXPERTS, logits, NEG)
    m1 = jnp.max(logits, axis=-1, keepdims=True)
    i1 = jnp.min(jnp.where(logits == m1, lane, float(LANES)), axis=-1, keepdims=True)
    rest = jnp.where(lane == i1, NEG, logits)
    m2 = jnp.max(rest, axis=-1, keepdims=True)
    i2 = jnp.min(jnp.where(rest == m2, lane, float(LANES)), axis=-1, keepdims=True)
    t = jnp.exp(m2 - m1)
    comb_ref[...] = jnp.where(lane == i1, 1.0 / (1.0 + t), 0.0) + jnp.where(lane == i2, t / (1.0 + t), 0.0)
    sel = jnp.where(lane == i1, 1.0, jnp.where(lane == i2, 1.0, 0.0))
    selb = sel.astype(BF16)
    r = lax.broadcasted_iota(jnp.int32, (tm, tm), 0)
    c = lax.broadcasted_iota(jnp.int32, (tm, tm), 1)
    rank = jnp.dot(jnp.where(c < r, 1.0, 0.0).astype(BF16), selb, preferred_element_type=F32)
    pc_ref[...] = jnp.where(sel > 0.0, rank, -1.0)
    eye = jnp.where(lax.broadcasted_iota(jnp.int32, (8, LANES), 0) == lax.broadcasted_iota(jnp.int32, (8, LANES), 1),
                    1.0, 0.0).astype(BF16)
    sel_t = _dot_nt(eye, selb)
    rank_t = jnp.dot(sel_t.astype(BF16), jnp.where(r < c, 1.0, 0.0).astype(BF16), preferred_element_type=F32)
    pt_ref[...] = jnp.where(sel_t > 0.0, rank_t, -1.0)
    row8 = lax.broadcasted_iota(jnp.int32, (8, LANES), 0)
    cnt_ref[...] = jnp.where(row8 == 0, jnp.sum(sel, axis=0, keepdims=True),
                             jnp.where(row8 == 1, rank[tm // 2:tm // 2 + 1, :], 0.0))


def _route(x, ys, w_out, g, router_pad):
    tm = MOE_TM
    nt = TOKENS // tm
    tok = lambda width: pl.BlockSpec((tm, width), lambda i: (i, 0))
    return pl.pallas_call(
        _route_kernel,
        grid=(nt,),
        in_specs=_mix_specs(tm) + [pl.BlockSpec((1, D_MODEL), lambda i: (0, 0)),
                                   pl.BlockSpec((D_MODEL, LANES), lambda i: (0, 0))],
        out_specs=[tok(D_MODEL), tok(D_MODEL), tok(LANES), tok(LANES), pl.BlockSpec((8, tm), lambda i: (0, i)),
                   pl.BlockSpec((8, LANES), lambda i: (i, 0))],
        out_shape=[jax.ShapeDtypeStruct((TOKENS, D_MODEL), F32), jax.ShapeDtypeStruct((TOKENS, D_MODEL), BF16),
                   jax.ShapeDtypeStruct((TOKENS, LANES), F32), jax.ShapeDtypeStruct((TOKENS, LANES), F32),
                   jax.ShapeDtypeStruct((8, TOKENS), F32), jax.ShapeDtypeStruct((nt * 8, LANES), F32)],
        compiler_params=_params("parallel"),
        name="moe_route",
    )(x, *ys, w_out, g, router_pad)


def _moe_kernel(cnt_ref, h_ref, pt_ref, pc_ref, comb_ref, wg_ref, wu_ref, wd_ref, o_ref, xg_ref, yacc_ref, acc_ref):
    tm = h_ref.shape[0]
    sm, oc = MOE_STATIC_ROWS, MOE_CHUNK
    i = pl.program_id(0)
    e = pl.program_id(1)
    f = pl.program_id(2)
    n_over = jnp.maximum(cnt_ref[i * N_EXPERTS + e] - sm + oc - 1, 0) // oc

    def overflow(body):
        def step(c, carry):
            body(pl.multiple_of(sm + c * oc, oc), oc)
            return carry
        lax.fori_loop(0, n_over, step, 0)

    @pl.when((e == 0) & (f == 0))
    def _():
        acc_ref[...] = jnp.zeros(acc_ref.shape, F32)

    @pl.when(f == 0)
    def _():
        pt = pt_ref[pl.ds(e, 1), :]

        def gather(row0, nrows):
            ridx = lax.broadcasted_iota(jnp.int32, (nrows, tm), 0) + row0
            onehot = jnp.where(ridx.astype(F32) == pt, 1.0, 0.0).astype(BF16)
            xg_ref[pl.ds(row0, nrows), :] = jnp.dot(onehot, h_ref[...], preferred_element_type=F32).astype(BF16)
            yacc_ref[pl.ds(row0, nrows), :] = jnp.zeros((nrows, D_MODEL), F32)

        gather(0, sm)
        overflow(gather)

    def ffn(row0, nrows):
        rows = pl.ds(row0, nrows)
        yacc_ref[rows, :] += _swiglu(xg_ref[rows, :], wg_ref[0], wu_ref[0], wd_ref[0])

    ffn(0, sm)
    overflow(ffn)

    @pl.when(f == pl.num_programs(2) - 1)
    def _():
        lane = lax.broadcasted_iota(jnp.int32, (tm, LANES), 1)
        mine = lane == e
        pc = jnp.sum(jnp.where(mine, pc_ref[...], 0.0), axis=-1, keepdims=True)
        gate = jnp.sum(jnp.where(mine, comb_ref[...], 0.0), axis=-1, keepdims=True)

        def scatter(row0, nrows, tok0=0, ntok=tm):
            cidx = lax.broadcasted_iota(jnp.int32, (ntok, nrows), 1) + row0
            onehot = jnp.where(cidx.astype(F32) == pc[tok0:tok0 + ntok], 1.0, 0.0).astype(BF16)
            y = yacc_ref[pl.ds(row0, nrows), :].astype(BF16)
            acc_ref[tok0:tok0 + ntok, :] += jnp.dot(onehot, y, preferred_element_type=F32) * gate[tok0:tok0 + ntok]

        half = tm // 2
        n_half = cnt_ref[pl.num_programs(0) * N_EXPERTS + i * N_EXPERTS + e]
        windowed = (n_half >= MOE_HALF_MIN) & (n_half <= MOE_HALF_MAX)

        @pl.when(windowed)
        def _():
            scatter(0, MOE_HALF_MAX, 0, half)
            scatter(MOE_HALF_MIN, sm - MOE_HALF_MIN, half, half)

        @pl.when(jnp.logical_not(windowed))
        def _():
            scatter(0, sm)

        overflow(scatter)

    @pl.when((e == pl.num_programs(1) - 1) & (f == pl.num_programs(2) - 1))
    def _():
        o_ref[...] = acc_ref[...].astype(o_ref.dtype)


def _moe(h, pt, pc, comb, counts, wg, wu, wd):
    tm, tf = MOE_TM, FFN_DIM // 2
    grid_spec = pltpu.PrefetchScalarGridSpec(
        num_scalar_prefetch=1,
        grid=(TOKENS // tm, N_EXPERTS, FFN_DIM // tf),
        in_specs=[pl.BlockSpec((tm, D_MODEL), lambda i, e, f, cnt: (i, 0)),
                  pl.BlockSpec((8, tm), lambda i, e, f, cnt: (0, i)),
                  pl.BlockSpec((tm, LANES), lambda i, e, f, cnt: (i, 0)),
                  pl.BlockSpec((tm, LANES), lambda i, e, f, cnt: (i, 0)),
                  pl.BlockSpec((1, D_MODEL, tf), lambda i, e, f, cnt: (e, 0, f)),
                  pl.BlockSpec((1, D_MODEL, tf), lambda i, e, f, cnt: (e, 0, f)),
                  pl.BlockSpec((1, tf, D_MODEL), lambda i, e, f, cnt: (e, f, 0))],
        out_specs=pl.BlockSpec((tm, D_MODEL), lambda i, e, f, cnt: (i, 0)),
        scratch_shapes=[pltpu.VMEM((tm, D_MODEL), BF16), pltpu.VMEM((tm, D_MODEL), F32),
                        pltpu.VMEM((tm, D_MODEL), F32)])
    return pl.pallas_call(
        _moe_kernel,
        grid_spec=grid_spec,
        out_shape=jax.ShapeDtypeStruct((TOKENS, D_MODEL), BF16),
        compiler_params=_params("parallel", "arbitrary", "arbitrary"),
        name="moe_ffn",
    )(counts, h, pt, pc, comb, wg, wu, wd)


def _final_kernel(x_ref, y_ref, g_ref, o_ref):
    o_ref[...] = _rms(x_ref[...] + y_ref[...].astype(F32), g_ref[...])


def _final(x, y, g):
    tm = 1024
    return pl.pallas_call(
        _final_kernel,
        grid=(TOKENS // tm,),
        in_specs=[pl.BlockSpec((tm, D_MODEL), lambda i: (i, 0)), pl.BlockSpec((tm, D_MODEL), lambda i: (i, 0)),
                  pl.BlockSpec((1, D_MODEL), lambda i: (0, 0))],
        out_specs=pl.BlockSpec((tm, D_MODEL), lambda i: (i, 0)),
        out_shape=jax.ShapeDtypeStruct((TOKENS, D_MODEL), F32),
        compiler_params=_params("parallel"),
        name="final_norm",
    )(x, y, g)


def kernel(x, positions, attn_norm, w_in, b_forget, mla_q_norm, w_q_up, mla_kv_norm, w_kv_up, sinks, w_out, ffn_norm, dense_w_gate, dense_w_up, dense_w_down, router, moe_w_gate, moe_w_up, moe_w_down, final_norm):
    assert x.shape == (BATCH, SEQ, D_MODEL) and positions.shape == (BATCH, SEQ)
    p_idx, p_sgn = _proj_columns()
    q_idx, q_sgn = _mla_q_columns()
    kv_idx, kv_sgn = _mla_kv_columns()
    mix_rows = _mix_rows()
    cos, sin = _rope_tables(positions)
    xt = x.reshape(TOKENS, D_MODEL).astype(F32)
    assert DEPTH == 2 and moe_w_gate.shape[0] == 1
    moe_w_f32 = (moe_w_gate[0], moe_w_up[0], moe_w_down[0])
    moe_w = None
    for layer in range(DEPTH):
        w = _take_columns(w_in[layer], p_idx, p_sgn).astype(BF16)
        wq = _take_columns(w_q_up[layer], q_idx, q_sgn).astype(BF16)
        wkv = _take_columns(w_kv_up[layer], kv_idx, kv_sgn).astype(BF16)
        fq, fk, fv, fl, pb, qc, kc, vc, pd = _project(
            xt, attn_norm[layer].reshape(1, D_MODEL), w, mla_q_norm[layer].reshape(1, MLA_Q_LORA), wq,
            mla_kv_norm[layer].reshape(1, MLA_KV_LORA), wkv, cos, sin)
        y_a, moe_w = _flash(fq, _fox_keys(fl, b_forget[layer], fk), fv, "fox_attention", moe_w_f32, moe_w, 2 * layer,
                            decay_skip=True)
        y_b = _swa(pb, sinks[layer])
        y_c, moe_w = _flash(qc, kc, vc, "mla_attention", moe_w_f32, moe_w, 2 * layer + 1)
        y_d = _dilated(pd)
        cuts = [0] + [r for r in range(1, len(mix_rows)) if mix_rows[r] != mix_rows[r - 1] + 1] + [len(mix_rows)]
        w_o = jnp.concatenate([w_out[layer][int(mix_rows[a]):int(mix_rows[a]) + (b - a)]
                               for a, b in zip(cuts[:-1], cuts[1:])], axis=0).astype(BF16)
        ys = (y_a, y_b, y_c, y_d)
        j = layer // 2
        g = ffn_norm[layer].reshape(1, D_MODEL)
        if layer % 2 == 0:
            xt = _ffn(xt, ys, w_o, g, dense_w_gate, dense_w_up, dense_w_down, j)
        else:
            assert layer == DEPTH - 1
            router_pad = jnp.zeros((D_MODEL, LANES), F32).at[:, :N_EXPERTS].set(router[j])
            xt, h, comb, pc, pt, cnt = _route(xt, ys, w_o, g, router_pad)
            counts = jnp.concatenate([cnt[0::8, :N_EXPERTS].reshape(-1), cnt[1::8, :N_EXPERTS].reshape(-1)])
            counts = counts.astype(jnp.int32)
            y = _moe(h, pt, pc, comb, counts, *moe_w)
            xt = _final(xt, y, final_norm.reshape(1, D_MODEL))
    return xt.reshape(BATCH, SEQ, D_MODEL)
```

```python
import functools

import numpy as np
import jax
import jax.numpy as jnp
from jax import lax
from jax.experimental import pallas as pl
from jax.experimental.pallas import tpu as pltpu

D_MODEL = 1024
BATCH = 2
SEQ = 8192
DEPTH = 2
TOKENS = BATCH * SEQ
HEAD_DIM = 64
BAND = 128
NORM_EPS = 1e-6
FOX_HEADS = 4
SWA_Q_HEADS = 4
SWA_KV_HEADS = 2
SWA_WINDOW = 128
MLA_HEADS = 4
MLA_Q_LORA = 256
MLA_KV_LORA = 128
MLA_NOPE_DIM = 64
MLA_ROPE_DIM = 32
MLA_V_DIM = 64
ROPE_THETA = 10000.0
DIL_HEADS = 4
DIL_PATTERNS = ((128, 1), (512, 4), (2048, 16))
DIL_BLOCK = BAND * max(d for _, d in DIL_PATTERNS)
FFN_DIM = 3584
N_EXPERTS = 8
LANES = 128
NEG = -1e30
VMEM_LIMIT = 56 * 1024 * 1024

_OFF = np.cumsum([0, 256, 256, 256, 4, 256, 128, 128, 256, 128, 32, 256, 256, 256])
(_A_Q, _A_K, _A_V, _A_F, _B_Q, _B_K, _B_V, _C_Q, _C_KV, _C_KR, _D_Q, _D_K, _D_V) = _OFF[:13].tolist()
N_PROJ_BLOCKS = 22
LOG2E = 1.4426950408889634

BF16 = jnp.bfloat16
F32 = jnp.float32


def _alibi_slopes():
    n = SWA_Q_HEADS + DIL_HEADS
    return [2.0 ** (-8.0 * i / n) for i in range(1, n + 1)]


def _proj_columns():
    idx = np.zeros((N_PROJ_BLOCKS * LANES,), np.int32)
    sgn = np.zeros((N_PROJ_BLOCKS * LANES,), np.float32)

    def put(dst, src, n, sign=1.0):
        idx[dst:dst + n] = np.arange(src, src + n)
        sgn[dst:dst + n] = sign

    put(0, _A_Q, 256); put(256, _A_K, 256); put(512, _A_V, 256)
    for blk, heads in ((6, (0, 2)), (7, (1, 3))):
        for half, h in enumerate(heads):
            put(blk * LANES + half * HEAD_DIM, _B_Q + h * HEAD_DIM, HEAD_DIM)
    put(8 * LANES, _B_K, 128); put(9 * LANES, _B_V, 128)
    put(10 * LANES, _C_Q, 256); put(12 * LANES, _C_KV, 128)
    half = MLA_ROPE_DIM // 2
    put(13 * LANES + MLA_NOPE_DIM, _C_KR, MLA_ROPE_DIM)
    put(14 * LANES + MLA_NOPE_DIM, _C_KR + half, half, -1.0)
    put(14 * LANES + MLA_NOPE_DIM + half, _C_KR, half)
    put(15 * LANES, _D_Q, 256); put(17 * LANES, _D_K, 256); put(19 * LANES, _D_V, 256)
    put(21 * LANES, _A_F, FOX_HEADS)
    return idx, sgn


def _bias_lane0(h):
    return HEAD_DIM if h % 2 == 0 else 0


def _bias_placement():
    place = np.zeros((3 * LANES, FOX_HEADS * LANES), np.float32)
    for piece in range(3):
        for h in range(FOX_HEADS):
            place[piece * LANES + h, h * LANES + _bias_lane0(h) + piece] = 1.0
    return place


def _mla_q_columns():
    idx = np.zeros((8 * LANES,), np.int32)
    sgn = np.zeros((8 * LANES,), np.float32)
    half = MLA_ROPE_DIM // 2
    dq = MLA_NOPE_DIM + MLA_ROPE_DIM
    for h in range(MLA_HEADS):
        a = h * LANES
        idx[a:a + dq] = np.arange(h * dq, (h + 1) * dq); sgn[a:a + dq] = 1.0
        b = (MLA_HEADS + h) * LANES + MLA_NOPE_DIM
        r = h * dq + MLA_NOPE_DIM
        idx[b:b + half] = np.arange(r + half, r + 2 * half); sgn[b:b + half] = -1.0
        idx[b + half:b + 2 * half] = np.arange(r, r + half); sgn[b + half:b + 2 * half] = 1.0
    return idx, sgn


def _mla_kv_columns():
    idx = np.zeros((6 * LANES,), np.int32)
    sgn = np.zeros((6 * LANES,), np.float32)
    dkv = MLA_NOPE_DIM + MLA_V_DIM
    for h in range(MLA_HEADS):
        idx[h * LANES:h * LANES + MLA_NOPE_DIM] = np.arange(h * dkv, h * dkv + MLA_NOPE_DIM)
        sgn[h * LANES:h * LANES + MLA_NOPE_DIM] = 1.0
        b = MLA_HEADS * LANES + h * MLA_V_DIM
        idx[b:b + MLA_V_DIM] = np.arange(h * dkv + MLA_NOPE_DIM, (h + 1) * dkv)
        sgn[b:b + MLA_V_DIM] = 1.0
    return idx, sgn


def _take_columns(w, idx, sgn):
    parts = []
    a = 0
    while a < len(idx):
        b = a + 1
        while b < len(idx) and sgn[b] == sgn[a] and (sgn[a] == 0 or idx[b] == idx[b - 1] + 1):
            b += 1
        if sgn[a] == 0:
            parts.append(jnp.zeros((w.shape[0], b - a), w.dtype))
        else:
            piece = w[:, int(idx[a]):int(idx[a]) + (b - a)]
            parts.append(piece if sgn[a] > 0 else -piece)
        a = b
    return jnp.concatenate(parts, axis=1)


def _mix_rows():
    rows = np.arange(4 * 256)
    b = 256
    perm = np.concatenate([np.arange(b + h * HEAD_DIM, b + (h + 1) * HEAD_DIM) for h in (0, 2, 1, 3)])
    rows[b:b + 256] = perm
    return rows


def _rms(x, g):
    return x * lax.rsqrt(jnp.mean(x * x, axis=-1, keepdims=True) + NORM_EPS) * g


def _dot_nt(a, b):
    return lax.dot_general(a, b, (((1,), (1,)), ((), ())), preferred_element_type=F32)


def _lane_tile(x, width):
    return x if width == LANES else jnp.concatenate([x] * (width // LANES), axis=1)


def _params(*sem):
    return pltpu.CompilerParams(dimension_semantics=sem, vmem_limit_bytes=VMEM_LIMIT)


def _rope_table_kernel(pos_ref, invf_ref, cos_ref, sin_ref):
    ang = pos_ref[...].astype(F32) * invf_ref[...]
    cos_ref[...] = jnp.cos(ang)
    sin_ref[...] = jnp.sin(ang)


def _rope_tables(positions):
    tm = 2048
    half = MLA_ROPE_DIM // 2
    invf = np.zeros((1, LANES), np.float32)
    f = (ROPE_THETA ** (-np.arange(half, dtype=np.float32) / np.float32(half))).astype(np.float32)
    invf[0, MLA_NOPE_DIM:MLA_NOPE_DIM + half] = f
    invf[0, MLA_NOPE_DIM + half:MLA_NOPE_DIM + 2 * half] = f
    return pl.pallas_call(
        _rope_table_kernel,
        grid=(TOKENS // tm,),
        in_specs=[pl.BlockSpec((tm, 1), lambda i: (i, 0)), pl.BlockSpec((1, LANES), lambda i: (0, 0))],
        out_specs=[pl.BlockSpec((tm, LANES), lambda i: (i, 0))] * 2,
        out_shape=[jax.ShapeDtypeStruct((TOKENS, LANES), F32)] * 2,
        compiler_params=_params("parallel"),
        name="rope_tables",
    )(positions.reshape(TOKENS, 1), jnp.asarray(invf))


def _proj_kernel(x_ref, g_ref, w_ref, qn_ref, wq_ref, kvn_ref, wkv_ref, cos_ref, sin_ref,
                 fq_ref, fk_ref, fv_ref, fl_ref, pb_ref, qc_ref, kc_ref, vc_ref, pd_ref):
    hb = _rms(x_ref[...], g_ref[...]).astype(BF16)
    res = jnp.dot(hb, w_ref[...], preferred_element_type=F32)
    lane = lax.broadcasted_iota(jnp.int32, (x_ref.shape[0], LANES), 1)
    low = lane < HEAD_DIM

    def blk(j, n=1):
        return res[:, j * LANES:(j + n) * LANES]

    def own(h):
        return low if h % 2 == 0 else ~low

    qscale = HEAD_DIM ** -0.5 * LOG2E
    for h in range(FOX_HEADS):
        ones = (lane >= _bias_lane0(h)) & (lane < _bias_lane0(h) + 3)
        fq_ref[h] = jnp.where(own(h), blk(h // 2) * qscale, jnp.where(ones, 1.0, 0.0)).astype(BF16)
        fv_ref[h] = jnp.where(own(h), blk(4 + h // 2), 1.0).astype(BF16)
    for j in range(2):
        fk_ref[j] = blk(2 + j).astype(BF16)
    fl_ref[...] = blk(21)
    for j in range(2):
        pb_ref[j] = (blk(6 + j) * qscale).astype(BF16)
    pb_ref[2] = blk(8).astype(BF16)
    pb_ref[3] = blk(9).astype(BF16)

    cos = cos_ref[...]
    sin = sin_ref[...]
    cq = _rms(blk(10, 2), qn_ref[...]).astype(BF16)
    qab = jnp.dot(cq, wq_ref[...], preferred_element_type=F32)
    mla_scale = (MLA_NOPE_DIM + MLA_ROPE_DIM) ** -0.5
    for h in range(MLA_HEADS):
        qa = qab[:, h * LANES:(h + 1) * LANES]
        qb = qab[:, (MLA_HEADS + h) * LANES:(MLA_HEADS + h + 1) * LANES]
        qc_ref[h] = ((qa * cos + qb * sin) * (mla_scale * LOG2E)).astype(BF16)
    ckv = _rms(blk(12), kvn_ref[...]).astype(BF16)
    kv = jnp.dot(ckv, wkv_ref[...], preferred_element_type=F32)
    k_rot = blk(13) * cos + blk(14) * sin
    for h in range(MLA_HEADS):
        kc_ref[h] = (kv[:, h * LANES:(h + 1) * LANES] + k_rot).astype(BF16)
        v_pair = kv[:, (MLA_HEADS + h // 2) * LANES:(MLA_HEADS + h // 2 + 1) * LANES]
        vc_ref[h] = jnp.where(own(h), v_pair, 1.0).astype(BF16)

    for j in range(2):
        pd_ref[j] = blk(15 + j) * qscale
    for j in range(2, 6):
        pd_ref[j] = blk(15 + j)


def _project(x, g, w, qn, wq, kvn, wkv, cos, sin):
    tm = 512
    full = lambda shape: pl.BlockSpec(shape, lambda i: (0,) * len(shape))
    out_blk = lambda n: pl.BlockSpec((n, tm, LANES), lambda i: (0, i, 0))
    out_sds = lambda n, dt: jax.ShapeDtypeStruct((n, TOKENS, LANES), dt)
    tok_blk = pl.BlockSpec((tm, LANES), lambda i: (i, 0))
    return pl.pallas_call(
        _proj_kernel,
        grid=(TOKENS // tm,),
        in_specs=[pl.BlockSpec((tm, D_MODEL), lambda i: (i, 0)), full((1, D_MODEL)),
                  full((D_MODEL, N_PROJ_BLOCKS * LANES)),
                  full((1, MLA_Q_LORA)), full((MLA_Q_LORA, 8 * LANES)),
                  full((1, MLA_KV_LORA)), full((MLA_KV_LORA, 6 * LANES)), tok_blk, tok_blk],
        out_specs=[out_blk(4), out_blk(2), out_blk(4), tok_blk, out_blk(4), out_blk(4), out_blk(4), out_blk(4),
                   out_blk(6)],
        out_shape=[out_sds(4, BF16), out_sds(2, BF16), out_sds(4, BF16), jax.ShapeDtypeStruct((TOKENS, LANES), F32),
                   out_sds(4, BF16), out_sds(4, BF16), out_sds(4, BF16), out_sds(4, BF16), out_sds(6, F32)],
        compiler_params=_params("parallel"),
        name="in_proj",
    )(x, g, w, qn, wq, kvn, wkv, cos, sin)


def _split3(x):
    hi = x.astype(BF16)
    r1 = x - hi.astype(F32)
    mid = r1.astype(BF16)
    return hi, mid, (r1 - mid.astype(F32)).astype(BF16)


def _fox_keys_kernel(fl_ref, b_ref, k_ref, place_ref, kf_ref, carry_ref, *, cs):
    @pl.when(pl.program_id(1) == 0)
    def _():
        carry_ref[...] = jnp.zeros_like(carry_ref)

    lf = jax.nn.log_sigmoid(fl_ref[...] + b_ref[...])
    row = lax.broadcasted_iota(jnp.int32, (cs, cs), 0)
    col = lax.broadcasted_iota(jnp.int32, (cs, cs), 1)
    tri = jnp.where(col <= row, 1.0, 0.0).astype(BF16)
    cum = carry_ref[0:1, :] + sum(jnp.dot(tri, piece, preferred_element_type=F32) for piece in _split3(lf))
    carry_ref[0:1, :] = cum[cs - 1:cs, :]
    pieces = jnp.concatenate(_split3(cum * (-LOG2E)), axis=1)
    placed = jnp.dot(pieces, place_ref[...], preferred_element_type=F32).astype(BF16)
    low = lax.broadcasted_iota(jnp.int32, (cs, LANES), 1) < HEAD_DIM
    for h in range(FOX_HEADS):
        kf_ref[h] = jnp.where(low if h % 2 == 0 else ~low, k_ref[h // 2], placed[:, h * LANES:(h + 1) * LANES])


def _fox_keys(fl, b_forget, fk):
    cs = 512
    nc = SEQ // cs
    b_row = jnp.zeros((1, LANES), F32).at[0, :FOX_HEADS].set(b_forget.astype(F32))
    return pl.pallas_call(
        functools.partial(_fox_keys_kernel, cs=cs),
        grid=(BATCH, nc),
        in_specs=[pl.BlockSpec((cs, LANES), lambda b, i: (b * nc + i, 0)),
                  pl.BlockSpec((1, LANES), lambda b, i: (0, 0)),
                  pl.BlockSpec((2, cs, LANES), lambda b, i: (0, b * nc + i, 0)),
                  pl.BlockSpec((3 * LANES, FOX_HEADS * LANES), lambda b, i: (0, 0))],
        out_specs=pl.BlockSpec((FOX_HEADS, cs, LANES), lambda b, i: (0, b * nc + i, 0)),
        out_shape=jax.ShapeDtypeStruct((FOX_HEADS, TOKENS, LANES), BF16),
        scratch_shapes=[pltpu.VMEM((8, LANES), F32)],
        compiler_params=_params("parallel", "arbitrary"),
        name="fox_keys",
    )(fl, b_row, fk, jnp.asarray(_bias_placement(), BF16))


FLASH_UNDERFLOW = -160.0


def _flash_kernel(*refs, tq, tk, n_cast, decay_skip):
    q_ref, k_ref, v_ref = refs[:3]
    o_ref = refs[3 + n_cast]
    acc_ref, m_ref, kmax_ref = refs[-3:]
    for src_ref, dst_ref in zip(refs[3:3 + n_cast], refs[4 + n_cast:4 + 2 * n_cast]):
        dst_ref[...] = src_ref[...].astype(dst_ref.dtype)
    i = pl.program_id(2)
    m_ref[...] = jnp.full(m_ref.shape, NEG, F32)
    acc_ref[...] = jnp.zeros(acc_ref.shape, F32)

    def sq_norms(x, h):
        lane_in = lax.broadcasted_iota(jnp.int32, (LANES, LANES), 0)
        ones = jnp.where((lane_in < HEAD_DIM) == (h == 0), 1.0, 0.0).astype(BF16)
        xf = x.astype(F32)
        return jnp.dot((xf * xf).astype(BF16), ones, preferred_element_type=F32)

    if decay_skip:
        @pl.when(i == 0)
        def _():
            for h in range(2):
                kmax_ref[h] = jnp.broadcast_to(jnp.max(sq_norms(k_ref[h], h), axis=0, keepdims=True), (8, LANES))

    def step(kb, row0, diagonal):
        ks = pl.multiple_of(kb * tk, tk)
        rows = pl.ds(row0, tq - row0)
        for h in range(2):
            s = _dot_nt(q_ref[h, rows, :], k_ref[h, pl.ds(ks, tk), :])
            if diagonal:
                r = lax.broadcasted_iota(jnp.int32, s.shape, 0)
                c = lax.broadcasted_iota(jnp.int32, s.shape, 1)
                s = jnp.where(c <= r, s, NEG)
            m_old = m_ref[h, rows, :]
            m_new = jnp.maximum(m_old, jnp.max(s, axis=-1, keepdims=True))
            p = jnp.exp2(s - _lane_tile(m_new, tk)).astype(BF16)
            acc_ref[h, rows, :] = (jnp.exp2(m_old - m_new) * acc_ref[h, rows, :]
                                   + jnp.dot(p, v_ref[h, pl.ds(ks, tk), :], preferred_element_type=F32))
            m_ref[h, rows, :] = m_new

    per_q = tq // tk
    if not decay_skip:
        def body(kb, carry):
            step(kb, 0, False)
            return carry

        lax.fori_loop(0, i * per_q, body, 0)
        for u in range(per_q):
            step(i * per_q + u, u * tk, True)
    else:
        for u in range(per_q):
            step(i * per_q + u, u * tk, True)
        slack = []
        for h in range(2):
            qk_bound = jnp.sqrt(sq_norms(q_ref[h], h) * kmax_ref[h, 0:1, :]) * 1.02
            slack.append(jnp.max(qk_bound - m_ref[h]))
        sub = lax.broadcasted_iota(jnp.int32, (16, LANES), 0)
        lane = lax.broadcasted_iota(jnp.int32, (16, LANES), 1)

        def last_bias(kb, h):
            rows = k_ref[h, pl.ds(pl.multiple_of((kb + 1) * tk - 16, 16), 16), :].astype(F32)
            pick = (sub == 15) & (lane >= _bias_lane0(h)) & (lane < _bias_lane0(h) + 3)
            return jnp.sum(jnp.where(pick, rows, 0.0))

        def visible(kb):
            kb0 = jnp.maximum(kb, 0)
            return (kb >= 0) & ((slack[0] + last_bias(kb0, 0) > FLASH_UNDERFLOW)
                                | (slack[1] + last_bias(kb0, 1) > FLASH_UNDERFLOW))

        def body(state):
            kb, _ = state
            step(kb, 0, False)
            return kb - 1, visible(kb - 1)

        lax.while_loop(lambda state: state[1], body, (i * per_q - 1, visible(i * per_q - 1)))
    outs = [acc_ref[h] / pltpu.roll(acc_ref[h], HEAD_DIM, axis=1) for h in range(2)]
    low = lax.broadcasted_iota(jnp.int32, (tq, LANES), 1) < HEAD_DIM
    o_ref[0] = jnp.where(low, outs[0], outs[1]).astype(o_ref.dtype)


def _flash(q, k, v, name, w_f32=None, decay_skip=False):
    tq, tk = 2048, 512
    nq = SEQ // tq
    w_args, w_specs, w_shapes = [], [], []
    if w_f32 is not None:
        per_expert = BATCH * 2 * nq // N_EXPERTS
        assert per_expert * N_EXPERTS == BATCH * 2 * nq and w_f32.shape[1] % (16 * per_expert) == 0
        slab = lambda b, j, i: divmod((b * 2 + j) * nq + i, per_expert) + (0,)
        w_args = [w_f32]
        w_specs = [pl.BlockSpec((1, w_f32.shape[1] // per_expert, w_f32.shape[2]), slab)]
        w_shapes = [jax.ShapeDtypeStruct(w_f32.shape, BF16)]
    kv_spec = pl.BlockSpec((2, SEQ, LANES), lambda b, j, i: (j, b, 0))
    outs = pl.pallas_call(
        functools.partial(_flash_kernel, tq=tq, tk=tk, n_cast=len(w_args), decay_skip=decay_skip),
        grid=(BATCH, 2, nq),
        in_specs=[pl.BlockSpec((2, tq, LANES), lambda b, j, i: (j, b * nq + i, 0)), kv_spec, kv_spec] + w_specs,
        out_specs=[pl.BlockSpec((1, tq, LANES), lambda b, j, i: (j, b * nq + i, 0))] + w_specs,
        out_shape=[jax.ShapeDtypeStruct((2, TOKENS, LANES), BF16)] + w_shapes,
        scratch_shapes=[pltpu.VMEM((2, tq, LANES), F32)] * 2 + [pltpu.VMEM((2, 8, LANES), F32)],
        compiler_params=_params("parallel", "parallel", "arbitrary"),
        name=name,
    )(q, k, v, *w_args)
    return outs[0], (outs[1] if w_args else None)


def _band_mask_bias(span, slope_step, first):
    qi = lax.broadcasted_iota(jnp.int32, (BAND, 2 * BAND), 0)
    kj = lax.broadcasted_iota(jnp.int32, (BAND, 2 * BAND), 1)
    dist = BAND + qi - kj
    ok = (dist >= 0) & (dist <= span) & (kj >= jnp.where(first, BAND, 0))
    return jnp.where(ok, dist.astype(F32) * (-slope_step), NEG)


def _band_pair(q, kk, vv, mask_bias):
    low = lax.broadcasted_iota(jnp.int32, (BAND, LANES), 1) < HEAD_DIM
    low_kv = lax.broadcasted_iota(jnp.int32, (2 * BAND, LANES), 1) < HEAD_DIM
    accs, ms = [], []
    for half in range(2):
        qm = jnp.where(low if half == 0 else ~low, q, jnp.zeros_like(q))
        vh = jnp.where(low_kv if half == 0 else ~low_kv, vv, jnp.ones_like(vv))
        s = _dot_nt(qm, kk) + mask_bias[half]
        m = jnp.max(s, axis=-1, keepdims=True)
        accs.append(jnp.dot(jnp.exp2(s - m).astype(BF16), vh, preferred_element_type=F32))
        ms.append(m)
    l = pltpu.roll(jnp.where(low, accs[1], accs[0]), HEAD_DIM, axis=1)
    return jnp.where(low, accs[0], accs[1]) / l, jnp.where(low, ms[0], ms[1]) + jnp.log2(l)


def _swa_kernel(q_ref, k_ref, kp_ref, v_ref, vp_ref, sink_ref, o_ref, kk_ref, vv_ref, *, tb, slopes):
    n = pl.program_id(1)
    kk_ref[0:BAND] = kp_ref[0]
    kk_ref[BAND:] = k_ref[0]
    vv_ref[0:BAND] = vp_ref[0]
    vv_ref[BAND:] = v_ref[0]
    mask_bias = [[_band_mask_bias(SWA_WINDOW - 1, slope * LOG2E, first) for slope in slopes]
                 for first in (n == 0, False)]
    for c in range(tb // BAND):
        kk = kk_ref[c * BAND:(c + 2) * BAND]
        vv = vv_ref[c * BAND:(c + 2) * BAND]
        mb = mask_bias[0 if c == 0 else 1]
        for jb in range(2):
            o, lse2 = _band_pair(q_ref[jb, c * BAND:(c + 1) * BAND, :], kk, vv, (mb[jb], mb[jb + 2]))
            o = o / (1.0 + jnp.exp2(sink_ref[jb:jb + 1, :] - lse2))
            o_ref[jb, c * BAND:(c + 1) * BAND, :] = o.astype(o_ref.dtype)


def _swa(pb, sinks):
    tb = 512
    nb = SEQ // tb
    r = tb // BAND
    s = sinks.astype(F32) * LOG2E
    sink_lanes = jnp.stack([jnp.concatenate([jnp.full((HEAD_DIM,), s[jb]), jnp.full((HEAD_DIM,), s[jb + 2])])
                            for jb in range(2)])
    cur = lambda blk: pl.BlockSpec((1, tb, LANES), lambda b, n: (blk, b * nb + n, 0))
    prev = lambda blk: pl.BlockSpec((1, BAND, LANES), lambda b, n: (blk, jnp.maximum((b * nb + n) * r - 1, 0), 0))
    return pl.pallas_call(
        functools.partial(_swa_kernel, tb=tb, slopes=_alibi_slopes()[:SWA_Q_HEADS]),
        grid=(BATCH, nb),
        in_specs=[pl.BlockSpec((2, tb, LANES), lambda b, n: (0, b * nb + n, 0)), cur(2), prev(2), cur(3), prev(3),
                  pl.BlockSpec((2, LANES), lambda b, n: (0, 0))],
        out_specs=pl.BlockSpec((2, tb, LANES), lambda b, n: (0, b * nb + n, 0)),
        out_shape=jax.ShapeDtypeStruct((2, TOKENS, LANES), BF16),
        scratch_shapes=[pltpu.VMEM((tb + BAND, LANES), BF16)] * 2,
        compiler_params=_params("parallel", "parallel"),
        name="swa_attention",
    )(pb, pb, pb, pb, pb, sink_lanes)


def _dil_kernel(q_ref, k_ref, kp_ref, v_ref, vp_ref, o_ref, kk_ref, vv_ref, po_ref, pl_ref, *, slopes):
    pair = pl.program_id(1)
    n = pl.program_id(2)
    kk_ref[0:DIL_BLOCK] = kp_ref[0]
    kk_ref[DIL_BLOCK:] = k_ref[0]
    vv_ref[0:DIL_BLOCK] = vp_ref[0]
    vv_ref[DIL_BLOCK:] = v_ref[0]
    units = DIL_BLOCK // BAND
    for p, (window, dil) in enumerate(DIL_PATTERNS):
        steps = [jnp.where(pair == 0, slopes[half], slopes[2 + half]) * (dil * LOG2E) for half in range(2)]
        mask_bias = [[_band_mask_bias(window // dil, step, first) for step in steps] for first in (n == 0, False)]

        def unit(u, carry, p=p, dil=dil, mb=None):
            start = (u // dil) * (BAND * dil) + u % dil
            if dil == 1:
                rows = pl.ds(pl.multiple_of(start, BAND), BAND)
                krows = pl.ds(pl.multiple_of(DIL_BLOCK + start - BAND, BAND), 2 * BAND)
            else:
                rows = pl.ds(start, BAND, stride=dil)
                krows = pl.ds(DIL_BLOCK + start - BAND * dil, 2 * BAND, stride=dil)
            o, lse2 = _band_pair(q_ref[0, rows, :].astype(BF16), kk_ref[krows, :].astype(BF16),
                                 vv_ref[krows, :].astype(BF16), mb)
            po_ref[p, rows, :] = o
            pl_ref[p, rows, :] = lse2
            return carry

        lax.fori_loop(0, dil, functools.partial(unit, mb=mask_bias[0]), 0, unroll=min(dil, 8))
        if dil < units:
            lax.fori_loop(dil, units, functools.partial(unit, mb=mask_bias[1]), 0, unroll=min(units - dil, 8))
    chunk = 256

    def merge(t, carry):
        rows = pl.ds(pl.multiple_of(t * chunk, chunk), chunk)
        lse = pl_ref[:, rows, :]
        w = jnp.exp2(lse - jnp.max(lse, axis=0, keepdims=True))
        o_ref[0, rows, :] = (jnp.sum(w * po_ref[:, rows, :], axis=0) / jnp.sum(w, axis=0)).astype(o_ref.dtype)
        return carry

    lax.fori_loop(0, DIL_BLOCK // chunk, merge, 0)


def _dilated(pd):
    nb = SEQ // DIL_BLOCK
    cur = lambda off: pl.BlockSpec((1, DIL_BLOCK, LANES), lambda b, j, n: (off + j, b * nb + n, 0))
    prev = lambda off: pl.BlockSpec((1, DIL_BLOCK, LANES),
                                    lambda b, j, n: (off + j, b * nb + jnp.maximum(n - 1, 0), 0))
    return pl.pallas_call(
        functools.partial(_dil_kernel, slopes=_alibi_slopes()[SWA_Q_HEADS:]),
        grid=(BATCH, 2, nb),
        in_specs=[cur(0), cur(2), prev(2), cur(4), prev(4)],
        out_specs=pl.BlockSpec((1, DIL_BLOCK, LANES), lambda b, j, n: (j, b * nb + n, 0)),
        out_shape=jax.ShapeDtypeStruct((2, TOKENS, LANES), BF16),
        scratch_shapes=[pltpu.VMEM((2 * DIL_BLOCK, LANES), F32)] * 2
                       + [pltpu.VMEM((len(DIL_PATTERNS), DIL_BLOCK, LANES), F32)] * 2,
        compiler_params=_params("parallel", "parallel", "parallel"),
        name="dilated_attention",
    )(pd, pd, pd, pd, pd)


def _mix_residual(x_ref, y_refs, w_ref):
    mixed = jnp.concatenate([y[j] for y in y_refs for j in range(2)], axis=1)
    return x_ref[...] + jnp.dot(mixed, w_ref[...], preferred_element_type=F32)


def _mix_specs(tm):
    return ([pl.BlockSpec((tm, D_MODEL), lambda i, *_: (i, 0))]
            + [pl.BlockSpec((2, tm, LANES), lambda i, *_: (0, i, 0))] * 4
            + [pl.BlockSpec((D_MODEL, D_MODEL), lambda i, *_: (0, 0))])


def _swiglu(h, wg, wu, wd):
    gate = jnp.dot(h, wg, preferred_element_type=F32)
    up = jnp.dot(h, wu, preferred_element_type=F32)
    act = (gate * jax.nn.sigmoid(gate) * up).astype(BF16)
    return jnp.dot(act, wd, preferred_element_type=F32)


def _ffn_kernel(x_ref, ya_ref, yb_ref, yc_ref, yd_ref, wo_ref, g_ref, wg_ref, wu_ref, wd_ref, o_ref, h_ref, acc_ref):
    f = pl.program_id(1)

    @pl.when(f == 0)
    def _():
        x = _mix_residual(x_ref, (ya_ref, yb_ref, yc_ref, yd_ref), wo_ref)
        h_ref[...] = _rms(x, g_ref[...]).astype(BF16)
        acc_ref[...] = x

    acc_ref[...] += _swiglu(h_ref[...], wg_ref[...].astype(BF16), wu_ref[...].astype(BF16), wd_ref[...].astype(BF16))

    @pl.when(f == pl.num_programs(1) - 1)
    def _():
        o_ref[...] = acc_ref[...]


def _ffn(x, ys, w_out, g, wg, wu, wd, j):
    tm, tf = 1024, 512
    return pl.pallas_call(
        _ffn_kernel,
        grid=(TOKENS // tm, FFN_DIM // tf),
        in_specs=_mix_specs(tm) + [
                  pl.BlockSpec((1, D_MODEL), lambda i, f: (0, 0)),
                  pl.BlockSpec((None, D_MODEL, tf), lambda i, f: (j, 0, f)),
                  pl.BlockSpec((None, D_MODEL, tf), lambda i, f: (j, 0, f)),
                  pl.BlockSpec((None, tf, D_MODEL), lambda i, f: (j, f, 0))],
        out_specs=pl.BlockSpec((tm, D_MODEL), lambda i, f: (i, 0)),
        out_shape=jax.ShapeDtypeStruct((TOKENS, D_MODEL), F32),
        scratch_shapes=[pltpu.VMEM((tm, D_MODEL), BF16), pltpu.VMEM((tm, D_MODEL), F32)],
        compiler_params=_params("parallel", "arbitrary"),
        name="dense_ffn",
    )(x, *ys, w_out, g, wg, wu, wd)


MOE_TM = 1024
MOE_STATIC_ROWS = 288
MOE_CHUNK = 32
MOE_HALF_MIN = 96
MOE_HALF_MAX = 160


def _route_kernel(x_ref, ya_ref, yb_ref, yc_ref, yd_ref, wo_ref, g_ref, router_ref,
                  x1_ref, h_ref, comb_ref, pc_ref, pt_ref, cnt_ref):
    tm = x_ref.shape[0]
    x1 = _mix_residual(x_ref, (ya_ref, yb_ref, yc_ref, yd_ref), wo_ref)
    x1_ref[...] = x1
    h = _rms(x1, g_ref[...])
    h_ref[...] = h.astype(BF16)
    lane = lax.broadcasted_iota(jnp.int32, (tm, LANES), 1).astype(F32)
    h_hi, h_lo, _ = _split3(h)
    r_hi, r_lo, _ = _split3(router_ref[...])
    logits = (jnp.dot(h_hi, r_hi, preferred_element_type=F32) + jnp.dot(h_hi, r_lo, preferred_element_type=F32)
              + jnp.dot(h_lo, r_hi, preferred_element_type=F32))
    logits = jnp.where(lane < N_EXPERTS, logits, NEG)
    m1 = jnp.max(logits, axis=-1, keepdims=True)
    i1 = jnp.min(jnp.where(logits == m1, lane, float(LANES)), axis=-1, keepdims=True)
    rest = jnp.where(lane == i1, NEG, logits)
    m2 = jnp.max(rest, axis=-1, keepdims=True)
    i2 = jnp.min(jnp.where(rest == m2, lane, float(LANES)), axis=-1, keepdims=True)
    t = jnp.exp(m2 - m1)
    comb_ref[...] = jnp.where(lane == i1, 1.0 / (1.0 + t), 0.0) + jnp.where(lane == i2, t / (1.0 + t), 0.0)
    sel = jnp.where(lane == i1, 1.0, jnp.where(lane == i2, 1.0, 0.0))
    selb = sel.astype(BF16)
    r = lax.broadcasted_iota(jnp.int32, (tm, tm), 0)
    c = lax.broadcasted_iota(jnp.int32, (tm, tm), 1)
    rank = jnp.dot(jnp.where(c < r, 1.0, 0.0).astype(BF16), selb, preferred_element_type=F32)
    pc_ref[...] = jnp.where(sel > 0.0, rank, -1.0)
    eye = jnp.where(lax.broadcasted_iota(jnp.int32, (8, LANES), 0) == lax.broadcasted_iota(jnp.int32, (8, LANES), 1),
                    1.0, 0.0).astype(BF16)
    sel_t = _dot_nt(eye, selb)
    rank_t = jnp.dot(sel_t.astype(BF16), jnp.where(r < c, 1.0, 0.0).astype(BF16), preferred_element_type=F32)
    pt_ref[...] = jnp.where(sel_t > 0.0, rank_t, -1.0)
    row8 = lax.broadcasted_iota(jnp.int32, (8, LANES), 0)
    cnt_ref[...] = jnp.where(row8 == 0, jnp.sum(sel, axis=0, keepdims=True),
                             jnp.where(row8 == 1, rank[tm // 2:tm // 2 + 1, :], 0.0))


def _route(x, ys, w_out, g, router_pad):
    tm = MOE_TM
    nt = TOKENS // tm
    tok = lambda width: pl.BlockSpec((tm, width), lambda i: (i, 0))
    return pl.pallas_call(
        _route_kernel,
        grid=(nt,),
        in_specs=_mix_specs(tm) + [pl.BlockSpec((1, D_MODEL), lambda i: (0, 0)),
                                   pl.BlockSpec((D_MODEL, LANES), lambda i: (0, 0))],
        out_specs=[tok(D_MODEL), tok(D_MODEL), tok(LANES), tok(LANES), pl.BlockSpec((8, tm), lambda i: (0, i)),
                   pl.BlockSpec((8, LANES), lambda i: (i, 0))],
        out_shape=[jax.ShapeDtypeStruct((TOKENS, D_MODEL), F32), jax.ShapeDtypeStruct((TOKENS, D_MODEL), BF16),
                   jax.ShapeDtypeStruct((TOKENS, LANES), F32), jax.ShapeDtypeStruct((TOKENS, LANES), F32),
                   jax.ShapeDtypeStruct((8, TOKENS), F32), jax.ShapeDtypeStruct((nt * 8, LANES), F32)],
        compiler_params=_params("parallel"),
        name="moe_route",
    )(x, *ys, w_out, g, router_pad)


def _moe_kernel(cnt_ref, h_ref, pt_ref, pc_ref, comb_ref, wg_ref, wu_ref, wd_ref, o_ref, xg_ref, yacc_ref, acc_ref):
    tm = h_ref.shape[0]
    sm, oc = MOE_STATIC_ROWS, MOE_CHUNK
    i = pl.program_id(0)
    e = pl.program_id(1)
    f = pl.program_id(2)
    n_over = jnp.maximum(cnt_ref[i * N_EXPERTS + e] - sm + oc - 1, 0) // oc

    def overflow(body):
        def step(c, carry):
            body(pl.multiple_of(sm + c * oc, oc), oc)
            return carry
        lax.fori_loop(0, n_over, step, 0)

    @pl.when((e == 0) & (f == 0))
    def _():
        acc_ref[...] = jnp.zeros(acc_ref.shape, F32)

    @pl.when(f == 0)
    def _():
        pt = pt_ref[pl.ds(e, 1), :]

        def gather(row0, nrows):
            ridx = lax.broadcasted_iota(jnp.int32, (nrows, tm), 0) + row0
            onehot = jnp.where(ridx.astype(F32) == pt, 1.0, 0.0).astype(BF16)
            xg_ref[pl.ds(row0, nrows), :] = jnp.dot(onehot, h_ref[...], preferred_element_type=F32).astype(BF16)
            yacc_ref[pl.ds(row0, nrows), :] = jnp.zeros((nrows, D_MODEL), F32)

        gather(0, sm)
        overflow(gather)

    def ffn(row0, nrows):
        rows = pl.ds(row0, nrows)
        yacc_ref[rows, :] += _swiglu(xg_ref[rows, :], wg_ref[0], wu_ref[0], wd_ref[0])

    ffn(0, sm)
    overflow(ffn)

    @pl.when(f == pl.num_programs(2) - 1)
    def _():
        lane = lax.broadcasted_iota(jnp.int32, (tm, LANES), 1)
        mine = lane == e
        pc = jnp.sum(jnp.where(mine, pc_ref[...], 0.0), axis=-1, keepdims=True)
        gate = jnp.sum(jnp.where(mine, comb_ref[...], 0.0), axis=-1, keepdims=True)

        def scatter(row0, nrows, tok0=0, ntok=tm):
            cidx = lax.broadcasted_iota(jnp.int32, (ntok, nrows), 1) + row0
            onehot = jnp.where(cidx.astype(F32) == pc[tok0:tok0 + ntok], 1.0, 0.0).astype(BF16)
            y = yacc_ref[pl.ds(row0, nrows), :].astype(BF16)
            acc_ref[tok0:tok0 + ntok, :] += jnp.dot(onehot, y, preferred_element_type=F32) * gate[tok0:tok0 + ntok]

        half = tm // 2
        n_half = cnt_ref[pl.num_programs(0) * N_EXPERTS + i * N_EXPERTS + e]
        windowed = (n_half >= MOE_HALF_MIN) & (n_half <= MOE_HALF_MAX)

        @pl.when(windowed)
        def _():
            scatter(0, MOE_HALF_MAX, 0, half)
            scatter(MOE_HALF_MIN, sm - MOE_HALF_MIN, half, half)

        @pl.when(jnp.logical_not(windowed))
        def _():
            scatter(0, sm)

        overflow(scatter)

    @pl.when((e == pl.num_programs(1) - 1) & (f == pl.num_programs(2) - 1))
    def _():
        o_ref[...] = acc_ref[...].astype(o_ref.dtype)


def _moe(h, pt, pc, comb, counts, wg, wu, wd):
    tm, tf = MOE_TM, FFN_DIM // 2
    grid_spec = pltpu.PrefetchScalarGridSpec(
        num_scalar_prefetch=1,
        grid=(TOKENS // tm, N_EXPERTS, FFN_DIM // tf),
        in_specs=[pl.BlockSpec((tm, D_MODEL), lambda i, e, f, cnt: (i, 0)),
                  pl.BlockSpec((8, tm), lambda i, e, f, cnt: (0, i)),
                  pl.BlockSpec((tm, LANES), lambda i, e, f, cnt: (i, 0)),
                  pl.BlockSpec((tm, LANES), lambda i, e, f, cnt: (i, 0)),
                  pl.BlockSpec((1, D_MODEL, tf), lambda i, e, f, cnt: (e, 0, f)),
                  pl.BlockSpec((1, D_MODEL, tf), lambda i, e, f, cnt: (e, 0, f)),
                  pl.BlockSpec((1, tf, D_MODEL), lambda i, e, f, cnt: (e, f, 0))],
        out_specs=pl.BlockSpec((tm, D_MODEL), lambda i, e, f, cnt: (i, 0)),
        scratch_shapes=[pltpu.VMEM((tm, D_MODEL), BF16), pltpu.VMEM((tm, D_MODEL), F32),
                        pltpu.VMEM((tm, D_MODEL), F32)])
    return pl.pallas_call(
        _moe_kernel,
        grid_spec=grid_spec,
        out_shape=jax.ShapeDtypeStruct((TOKENS, D_MODEL), BF16),
        compiler_params=_params("parallel", "arbitrary", "arbitrary"),
        name="moe_ffn",
    )(counts, h, pt, pc, comb, wg, wu, wd)


def _final_kernel(x_ref, y_ref, g_ref, o_ref):
    o_ref[...] = _rms(x_ref[...] + y_ref[...].astype(F32), g_ref[...])


def _final(x, y, g):
    tm = 1024
    return pl.pallas_call(
        _final_kernel,
        grid=(TOKENS // tm,),
        in_specs=[pl.BlockSpec((tm, D_MODEL), lambda i: (i, 0)), pl.BlockSpec((tm, D_MODEL), lambda i: (i, 0)),
                  pl.BlockSpec((1, D_MODEL), lambda i: (0, 0))],
        out_specs=pl.BlockSpec((tm, D_MODEL), lambda i: (i, 0)),
        out_shape=jax.ShapeDtypeStruct((TOKENS, D_MODEL), F32),
        compiler_params=_params("parallel"),
        name="final_norm",
    )(x, y, g)


def kernel(x, positions, attn_norm, w_in, b_forget, mla_q_norm, w_q_up, mla_kv_norm, w_kv_up, sinks, w_out, ffn_norm, dense_w_gate, dense_w_up, dense_w_down, router, moe_w_gate, moe_w_up, moe_w_down, final_norm):
    assert x.shape == (BATCH, SEQ, D_MODEL) and positions.shape == (BATCH, SEQ)
    p_idx, p_sgn = _proj_columns()
    q_idx, q_sgn = _mla_q_columns()
    kv_idx, kv_sgn = _mla_kv_columns()
    mix_rows = _mix_rows()
    cos, sin = _rope_tables(positions)
    xt = x.reshape(TOKENS, D_MODEL).astype(F32)
    assert DEPTH == 2 and moe_w_gate.shape[0] == 1
    to_round = [moe_w_gate[0], moe_w_up[0], moe_w_down[0]]
    moe_w = []
    for layer in range(DEPTH):
        w = _take_columns(w_in[layer], p_idx, p_sgn).astype(BF16)
        wq = _take_columns(w_q_up[layer], q_idx, q_sgn).astype(BF16)
        wkv = _take_columns(w_kv_up[layer], kv_idx, kv_sgn).astype(BF16)
        fq, fk, fv, fl, pb, qc, kc, vc, pd = _project(
            xt, attn_norm[layer].reshape(1, D_MODEL), w, mla_q_norm[layer].reshape(1, MLA_Q_LORA), wq,
            mla_kv_norm[layer].reshape(1, MLA_KV_LORA), wkv, cos, sin)
        y_a, rounded = _flash(fq, _fox_keys(fl, b_forget[layer], fk), fv, "fox_attention",
                              to_round.pop(0) if to_round else None, decay_skip=True)
        moe_w += [rounded] if rounded is not None else []
        y_b = _swa(pb, sinks[layer])
        y_c, rounded = _flash(qc, kc, vc, "mla_attention", to_round.pop(0) if to_round else None)
        moe_w += [rounded] if rounded is not None else []
        y_d = _dilated(pd)
        cuts = [0] + [r for r in range(1, len(mix_rows)) if mix_rows[r] != mix_rows[r - 1] + 1] + [len(mix_rows)]
        w_o = jnp.concatenate([w_out[layer][int(mix_rows[a]):int(mix_rows[a]) + (b - a)]
                               for a, b in zip(cuts[:-1], cuts[1:])], axis=0).astype(BF16)
        ys = (y_a, y_b, y_c, y_d)
        j = layer // 2
        g = ffn_norm[layer].reshape(1, D_MODEL)
        if layer % 2 == 0:
            xt = _ffn(xt, ys, w_o, g, dense_w_gate, dense_w_up, dense_w_down, j)
        else:
            assert layer == DEPTH - 1
            router_pad = jnp.zeros((D_MODEL, LANES), F32).at[:, :N_EXPERTS].set(router[j])
            xt, h, comb, pc, pt, cnt = _route(xt, ys, w_o, g, router_pad)
            counts = jnp.concatenate([cnt[0::8, :N_EXPERTS].reshape(-1), cnt[1::8, :N_EXPERTS].reshape(-1)])
            counts = counts.astype(jnp.int32)
            assert len(moe_w) == 3
            y = _moe(h, pt, pc, comb, counts, *moe_w)
            xt = _final(xt, y, final_norm.reshape(1, D_MODEL))
    return xt.reshape(BATCH, SEQ, D_MODEL)
```

```python
import functools

import numpy as np
import jax
import jax.numpy as jnp
from jax import lax
from jax.experimental import pallas as pl
from jax.experimental.pallas import tpu as pltpu

D_MODEL = 1024
BATCH = 2
SEQ = 8192
DEPTH = 2
TOKENS = BATCH * SEQ
HEAD_DIM = 64
BAND = 128
NORM_EPS = 1e-6
FOX_HEADS = 4
SWA_Q_HEADS = 4
SWA_KV_HEADS = 2
SWA_WINDOW = 128
MLA_HEADS = 4
MLA_Q_LORA = 256
MLA_KV_LORA = 128
MLA_NOPE_DIM = 64
MLA_ROPE_DIM = 32
MLA_V_DIM = 64
ROPE_THETA = 10000.0
DIL_HEADS = 4
DIL_PATTERNS = ((128, 1), (512, 4), (2048, 16))
DIL_BLOCK = BAND * max(d for _, d in DIL_PATTERNS)
FFN_DIM = 3584
N_EXPERTS = 8
LANES = 128
NEG = -1e30
VMEM_LIMIT = 56 * 1024 * 1024

_OFF = np.cumsum([0, 256, 256, 256, 4, 256, 128, 128, 256, 128, 32, 256, 256, 256])
(_A_Q, _A_K, _A_V, _A_F, _B_Q, _B_K, _B_V, _C_Q, _C_KV, _C_KR, _D_Q, _D_K, _D_V) = _OFF[:13].tolist()
N_PROJ_BLOCKS = 22
LOG2E = 1.4426950408889634

BF16 = jnp.bfloat16
F32 = jnp.float32


def _alibi_slopes():
    n = SWA_Q_HEADS + DIL_HEADS
    return [2.0 ** (-8.0 * i / n) for i in range(1, n + 1)]


def _proj_columns():
    idx = np.zeros((N_PROJ_BLOCKS * LANES,), np.int32)
    sgn = np.zeros((N_PROJ_BLOCKS * LANES,), np.float32)

    def put(dst, src, n, sign=1.0):
        idx[dst:dst + n] = np.arange(src, src + n)
        sgn[dst:dst + n] = sign

    put(0, _A_Q, 256); put(256, _A_K, 256); put(512, _A_V, 256)
    for blk, heads in ((6, (0, 2)), (7, (1, 3))):
        for half, h in enumerate(heads):
            put(blk * LANES + half * HEAD_DIM, _B_Q + h * HEAD_DIM, HEAD_DIM)
    put(8 * LANES, _B_K, 128); put(9 * LANES, _B_V, 128)
    put(10 * LANES, _C_Q, 256); put(12 * LANES, _C_KV, 128)
    half = MLA_ROPE_DIM // 2
    put(13 * LANES + MLA_NOPE_DIM, _C_KR, MLA_ROPE_DIM)
    put(14 * LANES + MLA_NOPE_DIM, _C_KR + half, half, -1.0)
    put(14 * LANES + MLA_NOPE_DIM + half, _C_KR, half)
    put(15 * LANES, _D_Q, 256); put(17 * LANES, _D_K, 256); put(19 * LANES, _D_V, 256)
    put(21 * LANES, _A_F, FOX_HEADS)
    return idx, sgn


def _bias_lane0(h):
    return HEAD_DIM if h % 2 == 0 else 0


def _bias_placement():
    place = np.zeros((3 * LANES, FOX_HEADS * LANES), np.float32)
    for piece in range(3):
        for h in range(FOX_HEADS):
            place[piece * LANES + h, h * LANES + _bias_lane0(h) + piece] = 1.0
    return place


def _mla_q_columns():
    idx = np.zeros((8 * LANES,), np.int32)
    sgn = np.zeros((8 * LANES,), np.float32)
    half = MLA_ROPE_DIM // 2
    dq = MLA_NOPE_DIM + MLA_ROPE_DIM
    for h in range(MLA_HEADS):
        a = h * LANES
        idx[a:a + dq] = np.arange(h * dq, (h + 1) * dq); sgn[a:a + dq] = 1.0
        b = (MLA_HEADS + h) * LANES + MLA_NOPE_DIM
        r = h * dq + MLA_NOPE_DIM
        idx[b:b + half] = np.arange(r + half, r + 2 * half); sgn[b:b + half] = -1.0
        idx[b + half:b + 2 * half] = np.arange(r, r + half); sgn[b + half:b + 2 * half] = 1.0
    return idx, sgn


def _mla_kv_columns():
    idx = np.zeros((6 * LANES,), np.int32)
    sgn = np.zeros((6 * LANES,), np.float32)
    dkv = MLA_NOPE_DIM + MLA_V_DIM
    for h in range(MLA_HEADS):
        idx[h * LANES:h * LANES + MLA_NOPE_DIM] = np.arange(h * dkv, h * dkv + MLA_NOPE_DIM)
        sgn[h * LANES:h * LANES + MLA_NOPE_DIM] = 1.0
        b = MLA_HEADS * LANES + h * MLA_V_DIM
        idx[b:b + MLA_V_DIM] = np.arange(h * dkv + MLA_NOPE_DIM, (h + 1) * dkv)
        sgn[b:b + MLA_V_DIM] = 1.0
    return idx, sgn


def _take_columns(w, idx, sgn):
    parts = []
    a = 0
    while a < len(idx):
        b = a + 1
        while b < len(idx) and sgn[b] == sgn[a] and (sgn[a] == 0 or idx[b] == idx[b - 1] + 1):
            b += 1
        if sgn[a] == 0:
            parts.append(jnp.zeros((w.shape[0], b - a), w.dtype))
        else:
            piece = w[:, int(idx[a]):int(idx[a]) + (b - a)]
            parts.append(piece if sgn[a] > 0 else -piece)
        a = b
    return jnp.concatenate(parts, axis=1)


def _mix_rows():
    rows = np.arange(4 * 256)
    b = 256
    perm = np.concatenate([np.arange(b + h * HEAD_DIM, b + (h + 1) * HEAD_DIM) for h in (0, 2, 1, 3)])
    rows[b:b + 256] = perm
    return rows


def _rms(x, g):
    return x * lax.rsqrt(jnp.mean(x * x, axis=-1, keepdims=True) + NORM_EPS) * g


def _dot_nt(a, b):
    return lax.dot_general(a, b, (((1,), (1,)), ((), ())), preferred_element_type=F32)


def _lane_tile(x, width):
    return x if width == LANES else jnp.concatenate([x] * (width // LANES), axis=1)


def _params(*sem):
    return pltpu.CompilerParams(dimension_semantics=sem, vmem_limit_bytes=VMEM_LIMIT)


def _rope_table_kernel(pos_ref, invf_ref, cos_ref, sin_ref):
    ang = pos_ref[...].astype(F32) * invf_ref[...]
    cos_ref[...] = jnp.cos(ang)
    sin_ref[...] = jnp.sin(ang)


def _rope_tables(positions):
    tm = 2048
    half = MLA_ROPE_DIM // 2
    invf = np.zeros((1, LANES), np.float32)
    f = (ROPE_THETA ** (-np.arange(half, dtype=np.float32) / np.float32(half))).astype(np.float32)
    invf[0, MLA_NOPE_DIM:MLA_NOPE_DIM + half] = f
    invf[0, MLA_NOPE_DIM + half:MLA_NOPE_DIM + 2 * half] = f
    return pl.pallas_call(
        _rope_table_kernel,
        grid=(TOKENS // tm,),
        in_specs=[pl.BlockSpec((tm, 1), lambda i: (i, 0)), pl.BlockSpec((1, LANES), lambda i: (0, 0))],
        out_specs=[pl.BlockSpec((tm, LANES), lambda i: (i, 0))] * 2,
        out_shape=[jax.ShapeDtypeStruct((TOKENS, LANES), F32)] * 2,
        compiler_params=_params("parallel"),
        name="rope_tables",
    )(positions.reshape(TOKENS, 1), jnp.asarray(invf))


def _proj_kernel(x_ref, g_ref, w_ref, qn_ref, wq_ref, kvn_ref, wkv_ref, cos_ref, sin_ref,
                 fq_ref, fk_ref, fv_ref, fl_ref, pb_ref, qc_ref, kc_ref, vc_ref, pd_ref):
    hb = _rms(x_ref[...], g_ref[...]).astype(BF16)
    res = jnp.dot(hb, w_ref[...], preferred_element_type=F32)
    lane = lax.broadcasted_iota(jnp.int32, (x_ref.shape[0], LANES), 1)
    low = lane < HEAD_DIM

    def blk(j, n=1):
        return res[:, j * LANES:(j + n) * LANES]

    def own(h):
        return low if h % 2 == 0 else ~low

    qscale = HEAD_DIM ** -0.5 * LOG2E
    for h in range(FOX_HEADS):
        ones = (lane >= _bias_lane0(h)) & (lane < _bias_lane0(h) + 3)
        fq_ref[h] = jnp.where(own(h), blk(h // 2) * qscale, jnp.where(ones, 1.0, 0.0)).astype(BF16)
        fv_ref[h] = jnp.where(own(h), blk(4 + h // 2), 1.0).astype(BF16)
    for j in range(2):
        fk_ref[j] = blk(2 + j).astype(BF16)
    fl_ref[...] = blk(21)
    for j in range(2):
        pb_ref[j] = (blk(6 + j) * qscale).astype(BF16)
    pb_ref[2] = blk(8).astype(BF16)
    pb_ref[3] = blk(9).astype(BF16)

    cos = cos_ref[...]
    sin = sin_ref[...]
    cq = _rms(blk(10, 2), qn_ref[...]).astype(BF16)
    qab = jnp.dot(cq, wq_ref[...], preferred_element_type=F32)
    mla_scale = (MLA_NOPE_DIM + MLA_ROPE_DIM) ** -0.5
    for h in range(MLA_HEADS):
        qa = qab[:, h * LANES:(h + 1) * LANES]
        qb = qab[:, (MLA_HEADS + h) * LANES:(MLA_HEADS + h + 1) * LANES]
        qc_ref[h] = ((qa * cos + qb * sin) * (mla_scale * LOG2E)).astype(BF16)
    ckv = _rms(blk(12), kvn_ref[...]).astype(BF16)
    kv = jnp.dot(ckv, wkv_ref[...], preferred_element_type=F32)
    k_rot = blk(13) * cos + blk(14) * sin
    for h in range(MLA_HEADS):
        kc_ref[h] = (kv[:, h * LANES:(h + 1) * LANES] + k_rot).astype(BF16)
        v_pair = kv[:, (MLA_HEADS + h // 2) * LANES:(MLA_HEADS + h // 2 + 1) * LANES]
        vc_ref[h] = jnp.where(own(h), v_pair, 1.0).astype(BF16)

    for j in range(2):
        pd_ref[j] = blk(15 + j) * qscale
    for j in range(2, 6):
        pd_ref[j] = blk(15 + j)


def _project(x, g, w, qn, wq, kvn, wkv, cos, sin):
    tm = 512
    full = lambda shape: pl.BlockSpec(shape, lambda i: (0,) * len(shape))
    out_blk = lambda n: pl.BlockSpec((n, tm, LANES), lambda i: (0, i, 0))
    out_sds = lambda n, dt: jax.ShapeDtypeStruct((n, TOKENS, LANES), dt)
    tok_blk = pl.BlockSpec((tm, LANES), lambda i: (i, 0))
    return pl.pallas_call(
        _proj_kernel,
        grid=(TOKENS // tm,),
        in_specs=[pl.BlockSpec((tm, D_MODEL), lambda i: (i, 0)), full((1, D_MODEL)),
                  full((D_MODEL, N_PROJ_BLOCKS * LANES)),
                  full((1, MLA_Q_LORA)), full((MLA_Q_LORA, 8 * LANES)),
                  full((1, MLA_KV_LORA)), full((MLA_KV_LORA, 6 * LANES)), tok_blk, tok_blk],
        out_specs=[out_blk(4), out_blk(2), out_blk(4), tok_blk, out_blk(4), out_blk(4), out_blk(4), out_blk(4),
                   out_blk(6)],
        out_shape=[out_sds(4, BF16), out_sds(2, BF16), out_sds(4, BF16), jax.ShapeDtypeStruct((TOKENS, LANES), F32),
                   out_sds(4, BF16), out_sds(4, BF16), out_sds(4, BF16), out_sds(4, BF16), out_sds(6, F32)],
        compiler_params=_params("parallel"),
        name="in_proj",
    )(x, g, w, qn, wq, kvn, wkv, cos, sin)


def _split3(x):
    hi = x.astype(BF16)
    r1 = x - hi.astype(F32)
    mid = r1.astype(BF16)
    return hi, mid, (r1 - mid.astype(F32)).astype(BF16)


def _fox_keys_kernel(fl_ref, b_ref, k_ref, place_ref, kf_ref, carry_ref, *, cs):
    @pl.when(pl.program_id(1) == 0)
    def _():
        carry_ref[...] = jnp.zeros_like(carry_ref)

    lf = jax.nn.log_sigmoid(fl_ref[...] + b_ref[...])
    row = lax.broadcasted_iota(jnp.int32, (cs, cs), 0)
    col = lax.broadcasted_iota(jnp.int32, (cs, cs), 1)
    tri = jnp.where(col <= row, 1.0, 0.0).astype(BF16)
    cum = carry_ref[0:1, :] + sum(jnp.dot(tri, piece, preferred_element_type=F32) for piece in _split3(lf))
    carry_ref[0:1, :] = cum[cs - 1:cs, :]
    pieces = jnp.concatenate(_split3(cum * (-LOG2E)), axis=1)
    placed = jnp.dot(pieces, place_ref[...], preferred_element_type=F32).astype(BF16)
    low = lax.broadcasted_iota(jnp.int32, (cs, LANES), 1) < HEAD_DIM
    for h in range(FOX_HEADS):
        kf_ref[h] = jnp.where(low if h % 2 == 0 else ~low, k_ref[h // 2], placed[:, h * LANES:(h + 1) * LANES])


def _fox_keys(fl, b_forget, fk):
    cs = 512
    nc = SEQ // cs
    b_row = jnp.zeros((1, LANES), F32).at[0, :FOX_HEADS].set(b_forget.astype(F32))
    return pl.pallas_call(
        functools.partial(_fox_keys_kernel, cs=cs),
        grid=(BATCH, nc),
        in_specs=[pl.BlockSpec((cs, LANES), lambda b, i: (b * nc + i, 0)),
                  pl.BlockSpec((1, LANES), lambda b, i: (0, 0)),
                  pl.BlockSpec((2, cs, LANES), lambda b, i: (0, b * nc + i, 0)),
                  pl.BlockSpec((3 * LANES, FOX_HEADS * LANES), lambda b, i: (0, 0))],
        out_specs=pl.BlockSpec((FOX_HEADS, cs, LANES), lambda b, i: (0, b * nc + i, 0)),
        out_shape=jax.ShapeDtypeStruct((FOX_HEADS, TOKENS, LANES), BF16),
        scratch_shapes=[pltpu.VMEM((8, LANES), F32)],
        compiler_params=_params("parallel", "arbitrary"),
        name="fox_keys",
    )(fl, b_row, fk, jnp.asarray(_bias_placement(), BF16))


FLASH_UNDERFLOW = -160.0


def _flash_kernel(*refs, tq, tk, n_cast, decay_skip):
    q_ref, k_ref, v_ref = refs[:3]
    o_ref = refs[3 + n_cast]
    acc_ref, m_ref, kmax_ref = refs[-3:]
    for src_ref, dst_ref in zip(refs[3:3 + n_cast], refs[4 + n_cast:4 + 2 * n_cast]):
        dst_ref[...] = src_ref[...].astype(dst_ref.dtype)
    i = pl.program_id(2)
    m_ref[...] = jnp.full(m_ref.shape, NEG, F32)
    acc_ref[...] = jnp.zeros(acc_ref.shape, F32)

    def sq_norms(x, h):
        lane_in = lax.broadcasted_iota(jnp.int32, (LANES, LANES), 0)
        ones = jnp.where((lane_in < HEAD_DIM) == (h == 0), 1.0, 0.0).astype(BF16)
        xf = x.astype(F32)
        return jnp.dot((xf * xf).astype(BF16), ones, preferred_element_type=F32)

    if decay_skip:
        @pl.when(i == 0)
        def _():
            for h in range(2):
                kmax_ref[h] = jnp.broadcast_to(jnp.max(sq_norms(k_ref[h], h), axis=0, keepdims=True), (8, LANES))

    def step(kb, row0, diagonal):
        ks = pl.multiple_of(kb * tk, tk)
        rows = pl.ds(row0, tq - row0)
        for h in range(2):
            s = _dot_nt(q_ref[h, rows, :], k_ref[h, pl.ds(ks, tk), :])
            if diagonal:
                r = lax.broadcasted_iota(jnp.int32, s.shape, 0)
                c = lax.broadcasted_iota(jnp.int32, s.shape, 1)
                s = jnp.where(c <= r, s, NEG)
            m_old = m_ref[h, rows, :]
            m_new = jnp.maximum(m_old, jnp.max(s, axis=-1, keepdims=True))
            p = jnp.exp2(s - _lane_tile(m_new, tk)).astype(BF16)
            acc_ref[h, rows, :] = (jnp.exp2(m_old - m_new) * acc_ref[h, rows, :]
                                   + jnp.dot(p, v_ref[h, pl.ds(ks, tk), :], preferred_element_type=F32))
            m_ref[h, rows, :] = m_new

    per_q = tq // tk
    if not decay_skip:
        def body(kb, carry):
            step(kb, 0, False)
            return carry

        lax.fori_loop(0, i * per_q, body, 0)
        for u in range(per_q):
            step(i * per_q + u, u * tk, True)
    else:
        for u in range(per_q):
            step(i * per_q + u, u * tk, True)
        slack = []
        for h in range(2):
            qk_bound = jnp.sqrt(sq_norms(q_ref[h], h) * kmax_ref[h, 0:1, :]) * 1.02
            slack.append(jnp.max(qk_bound - m_ref[h]))
        sub = lax.broadcasted_iota(jnp.int32, (16, LANES), 0)
        lane = lax.broadcasted_iota(jnp.int32, (16, LANES), 1)

        def last_bias(kb, h):
            rows = k_ref[h, pl.ds(pl.multiple_of((kb + 1) * tk - 16, 16), 16), :].astype(F32)
            pick = (sub == 15) & (lane >= _bias_lane0(h)) & (lane < _bias_lane0(h) + 3)
            return jnp.sum(jnp.where(pick, rows, 0.0))

        def visible(kb):
            kb0 = jnp.maximum(kb, 0)
            return (kb >= 0) & ((slack[0] + last_bias(kb0, 0) > FLASH_UNDERFLOW)
                                | (slack[1] + last_bias(kb0, 1) > FLASH_UNDERFLOW))

        def body(state):
            kb, _ = state
            step(kb, 0, False)
            return kb - 1, visible(kb - 1)

        lax.while_loop(lambda state: state[1], body, (i * per_q - 1, visible(i * per_q - 1)))
    outs = [acc_ref[h] / pltpu.roll(acc_ref[h], HEAD_DIM, axis=1) for h in range(2)]
    low = lax.broadcasted_iota(jnp.int32, (tq, LANES), 1) < HEAD_DIM
    o_ref[0] = jnp.where(low, outs[0], outs[1]).astype(o_ref.dtype)


def _flash(q, k, v, name, w_f32=None, decay_skip=False):
    tq, tk = 2048, 512
    nq = SEQ // tq
    w_args, w_specs, w_shapes = [], [], []
    if w_f32 is not None:
        per_expert = BATCH * 2 * nq // N_EXPERTS
        assert per_expert * N_EXPERTS == BATCH * 2 * nq and w_f32.shape[1] % (16 * per_expert) == 0
        slab = lambda b, j, i: divmod((b * 2 + j) * nq + i, per_expert) + (0,)
        w_args = [w_f32]
        w_specs = [pl.BlockSpec((1, w_f32.shape[1] // per_expert, w_f32.shape[2]), slab)]
        w_shapes = [jax.ShapeDtypeStruct(w_f32.shape, BF16)]
    kv_spec = pl.BlockSpec((2, SEQ, LANES), lambda b, j, i: (j, b, 0))
    outs = pl.pallas_call(
        functools.partial(_flash_kernel, tq=tq, tk=tk, n_cast=len(w_args), decay_skip=decay_skip),
        grid=(BATCH, 2, nq),
        in_specs=[pl.BlockSpec((2, tq, LANES), lambda b, j, i: (j, b * nq + i, 0)), kv_spec, kv_spec] + w_specs,
        out_specs=[pl.BlockSpec((1, tq, LANES), lambda b, j, i: (j, b * nq + i, 0))] + w_specs,
        out_shape=[jax.ShapeDtypeStruct((2, TOKENS, LANES), BF16)] + w_shapes,
        scratch_shapes=[pltpu.VMEM((2, tq, LANES), F32)] * 2 + [pltpu.VMEM((2, 8, LANES), F32)],
        compiler_params=_params("parallel", "parallel", "arbitrary"),
        name=name,
    )(q, k, v, *w_args)
    return outs[0], (outs[1] if w_args else None)


def _band_mask_bias(span, slope_step, first):
    qi = lax.broadcasted_iota(jnp.int32, (BAND, 2 * BAND), 0)
    kj = lax.broadcasted_iota(jnp.int32, (BAND, 2 * BAND), 1)
    dist = BAND + qi - kj
    ok = (dist >= 0) & (dist <= span) & (kj >= jnp.where(first, BAND, 0))
    return jnp.where(ok, dist.astype(F32) * (-slope_step), NEG)


def _band_pair(q, kk, vv, mask_bias):
    low = lax.broadcasted_iota(jnp.int32, (BAND, LANES), 1) < HEAD_DIM
    low_kv = lax.broadcasted_iota(jnp.int32, (2 * BAND, LANES), 1) < HEAD_DIM
    accs, ms = [], []
    for half in range(2):
        qm = jnp.where(low if half == 0 else ~low, q, jnp.zeros_like(q))
        vh = jnp.where(low_kv if half == 0 else ~low_kv, vv, jnp.ones_like(vv))
        s = _dot_nt(qm, kk) + mask_bias[half]
        m = jnp.max(s, axis=-1, keepdims=True)
        accs.append(jnp.dot(jnp.exp2(s - m).astype(BF16), vh, preferred_element_type=F32))
        ms.append(m)
    l = pltpu.roll(jnp.where(low, accs[1], accs[0]), HEAD_DIM, axis=1)
    return jnp.where(low, accs[0], accs[1]) / l, jnp.where(low, ms[0], ms[1]) + jnp.log2(l)


def _swa_kernel(q_ref, k_ref, kp_ref, v_ref, vp_ref, sink_ref, o_ref, kk_ref, vv_ref, *, tb, slopes):
    n = pl.program_id(1)
    kk_ref[0:BAND] = kp_ref[0]
    kk_ref[BAND:] = k_ref[0]
    vv_ref[0:BAND] = vp_ref[0]
    vv_ref[BAND:] = v_ref[0]
    mask_bias = [[_band_mask_bias(SWA_WINDOW - 1, slope * LOG2E, first) for slope in slopes]
                 for first in (n == 0, False)]
    for c in range(tb // BAND):
        kk = kk_ref[c * BAND:(c + 2) * BAND]
        vv = vv_ref[c * BAND:(c + 2) * BAND]
        mb = mask_bias[0 if c == 0 else 1]
        for jb in range(2):
            o, lse2 = _band_pair(q_ref[jb, c * BAND:(c + 1) * BAND, :], kk, vv, (mb[jb], mb[jb + 2]))
            o = o / (1.0 + jnp.exp2(sink_ref[jb:jb + 1, :] - lse2))
            o_ref[jb, c * BAND:(c + 1) * BAND, :] = o.astype(o_ref.dtype)


def _swa(pb, sinks):
    tb = 512
    nb = SEQ // tb
    r = tb // BAND
    s = sinks.astype(F32) * LOG2E
    sink_lanes = jnp.stack([jnp.concatenate([jnp.full((HEAD_DIM,), s[jb]), jnp.full((HEAD_DIM,), s[jb + 2])])
                            for jb in range(2)])
    cur = lambda blk: pl.BlockSpec((1, tb, LANES), lambda b, n: (blk, b * nb + n, 0))
    prev = lambda blk: pl.BlockSpec((1, BAND, LANES), lambda b, n: (blk, jnp.maximum((b * nb + n) * r - 1, 0), 0))
    return pl.pallas_call(
        functools.partial(_swa_kernel, tb=tb, slopes=_alibi_slopes()[:SWA_Q_HEADS]),
        grid=(BATCH, nb),
        in_specs=[pl.BlockSpec((2, tb, LANES), lambda b, n: (0, b * nb + n, 0)), cur(2), prev(2), cur(3), prev(3),
                  pl.BlockSpec((2, LANES), lambda b, n: (0, 0))],
        out_specs=pl.BlockSpec((2, tb, LANES), lambda b, n: (0, b * nb + n, 0)),
        out_shape=jax.ShapeDtypeStruct((2, TOKENS, LANES), BF16),
        scratch_shapes=[pltpu.VMEM((tb + BAND, LANES), BF16)] * 2,
        compiler_params=_params("parallel", "parallel"),
        name="swa_attention",
    )(pb, pb, pb, pb, pb, sink_lanes)


def _dil_kernel(q_ref, k_ref, kp_ref, v_ref, vp_ref, o_ref, kk_ref, vv_ref, po_ref, pl_ref, *, slopes):
    pair = pl.program_id(1)
    n = pl.program_id(2)
    kk_ref[0:DIL_BLOCK] = kp_ref[0]
    kk_ref[DIL_BLOCK:] = k_ref[0]
    vv_ref[0:DIL_BLOCK] = vp_ref[0]
    vv_ref[DIL_BLOCK:] = v_ref[0]
    units = DIL_BLOCK // BAND
    for p, (window, dil) in enumerate(DIL_PATTERNS):
        steps = [jnp.where(pair == 0, slopes[half], slopes[2 + half]) * (dil * LOG2E) for half in range(2)]
        mask_bias = [[_band_mask_bias(window // dil, step, first) for step in steps] for first in (n == 0, False)]

        def unit(u, carry, p=p, dil=dil, mb=None):
            start = (u // dil) * (BAND * dil) + u % dil
            if dil == 1:
                rows = pl.ds(pl.multiple_of(start, BAND), BAND)
                krows = pl.ds(pl.multiple_of(DIL_BLOCK + start - BAND, BAND), 2 * BAND)
            else:
                rows = pl.ds(start, BAND, stride=dil)
                krows = pl.ds(DIL_BLOCK + start - BAND * dil, 2 * BAND, stride=dil)
            o, lse2 = _band_pair(q_ref[0, rows, :].astype(BF16), kk_ref[krows, :].astype(BF16),
                                 vv_ref[krows, :].astype(BF16), mb)
            po_ref[p, rows, :] = o
            pl_ref[p, rows, :] = lse2
            return carry

        lax.fori_loop(0, dil, functools.partial(unit, mb=mask_bias[0]), 0, unroll=min(dil, 8))
        if dil < units:
            lax.fori_loop(dil, units, functools.partial(unit, mb=mask_bias[1]), 0, unroll=min(units - dil, 8))
    chunk = 256

    def merge(t, carry):
        rows = pl.ds(pl.multiple_of(t * chunk, chunk), chunk)
        lse = pl_ref[:, rows, :]
        w = jnp.exp2(lse - jnp.max(lse, axis=0, keepdims=True))
        o_ref[0, rows, :] = (jnp.sum(w * po_ref[:, rows, :], axis=0) / jnp.sum(w, axis=0)).astype(o_ref.dtype)
        return carry

    lax.fori_loop(0, DIL_BLOCK // chunk, merge, 0)


def _dilated(pd):
    nb = SEQ // DIL_BLOCK
    cur = lambda off: pl.BlockSpec((1, DIL_BLOCK, LANES), lambda b, j, n: (off + j, b * nb + n, 0))
    prev = lambda off: pl.BlockSpec((1, DIL_BLOCK, LANES),
                                    lambda b, j, n: (off + j, b * nb + jnp.maximum(n - 1, 0), 0))
    return pl.pallas_call(
        functools.partial(_dil_kernel, slopes=_alibi_slopes()[SWA_Q_HEADS:]),
        grid=(BATCH, 2, nb),
        in_specs=[cur(0), cur(2), prev(2), cur(4), prev(4)],
        out_specs=pl.BlockSpec((1, DIL_BLOCK, LANES), lambda b, j, n: (j, b * nb + n, 0)),
        out_shape=jax.ShapeDtypeStruct((2, TOKENS, LANES), BF16),
        scratch_shapes=[pltpu.VMEM((2 * DIL_BLOCK, LANES), F32)] * 2
                       + [pltpu.VMEM((len(DIL_PATTERNS), DIL_BLOCK, LANES), F32)] * 2,
        compiler_params=_params("parallel", "parallel", "parallel"),
        name="dilated_attention",
    )(pd, pd, pd, pd, pd)


def _mix_residual(x_ref, y_refs, w_ref):
    mixed = jnp.concatenate([y[j] for y in y_refs for j in range(2)], axis=1)
    return x_ref[...] + jnp.dot(mixed, w_ref[...], preferred_element_type=F32)


def _mix_specs(tm):
    return ([pl.BlockSpec((tm, D_MODEL), lambda i, *_: (i, 0))]
            + [pl.BlockSpec((2, tm, LANES), lambda i, *_: (0, i, 0))] * 4
            + [pl.BlockSpec((D_MODEL, D_MODEL), lambda i, *_: (0, 0))])


def _swiglu(h, wg, wu, wd):
    gate = jnp.dot(h, wg, preferred_element_type=F32)
    up = jnp.dot(h, wu, preferred_element_type=F32)
    act = (gate * jax.nn.sigmoid(gate) * up).astype(BF16)
    return jnp.dot(act, wd, preferred_element_type=F32)


def _ffn_kernel(x_ref, ya_ref, yb_ref, yc_ref, yd_ref, wo_ref, g_ref, wg_ref, wu_ref, wd_ref, o_ref, h_ref, acc_ref):
    f = pl.program_id(1)

    @pl.when(f == 0)
    def _():
        x = _mix_residual(x_ref, (ya_ref, yb_ref, yc_ref, yd_ref), wo_ref)
        h_ref[...] = _rms(x, g_ref[...]).astype(BF16)
        acc_ref[...] = x

    acc_ref[...] += _swiglu(h_ref[...], wg_ref[...].astype(BF16), wu_ref[...].astype(BF16), wd_ref[...].astype(BF16))

    @pl.when(f == pl.num_programs(1) - 1)
    def _():
        o_ref[...] = acc_ref[...]


def _ffn(x, ys, w_out, g, wg, wu, wd, j):
    tm, tf = 1024, 512
    return pl.pallas_call(
        _ffn_kernel,
        grid=(TOKENS // tm, FFN_DIM // tf),
        in_specs=_mix_specs(tm) + [
                  pl.BlockSpec((1, D_MODEL), lambda i, f: (0, 0)),
                  pl.BlockSpec((None, D_MODEL, tf), lambda i, f: (j, 0, f)),
                  pl.BlockSpec((None, D_MODEL, tf), lambda i, f: (j, 0, f)),
                  pl.BlockSpec((None, tf, D_MODEL), lambda i, f: (j, f, 0))],
        out_specs=pl.BlockSpec((tm, D_MODEL), lambda i, f: (i, 0)),
        out_shape=jax.ShapeDtypeStruct((TOKENS, D_MODEL), F32),
        scratch_shapes=[pltpu.VMEM((tm, D_MODEL), BF16), pltpu.VMEM((tm, D_MODEL), F32)],
        compiler_params=_params("parallel", "arbitrary"),
        name="dense_ffn",
    )(x, *ys, w_out, g, wg, wu, wd)


MOE_TM = 1024
MOE_STATIC_ROWS = 288
MOE_CHUNK = 32
MOE_HALF_MIN = 96
MOE_HALF_MAX = 160


def _route_kernel(x_ref, ya_ref, yb_ref, yc_ref, yd_ref, wo_ref, g_ref, router_ref,
                  x1_ref, h_ref, comb_ref, pc_ref, pt_ref, cnt_ref):
    tm = x_ref.shape[0]
    x1 = _mix_residual(x_ref, (ya_ref, yb_ref, yc_ref, yd_ref), wo_ref)
    x1_ref[...] = x1
    h = _rms(x1, g_ref[...])
    h_ref[...] = h.astype(BF16)
    lane = lax.broadcasted_iota(jnp.int32, (tm, LANES), 1).astype(F32)
    h_hi, h_lo, _ = _split3(h)
    r_hi, r_lo, _ = _split3(router_ref[...])
    logits = (jnp.dot(h_hi, r_hi, preferred_element_type=F32) + jnp.dot(h_hi, r_lo, preferred_element_type=F32)
              + jnp.dot(h_lo, r_hi, preferred_element_type=F32))
    logits = jnp.where(lane < N_EXPERTS, logits, NEG)
    m1 = jnp.max(logits, axis=-1, keepdims=True)
    i1 = jnp.min(jnp.where(logits == m1, lane, float(LANES)), axis=-1, keepdims=True)
    rest = jnp.where(lane == i1, NEG, logits)
    m2 = jnp.max(rest, axis=-1, keepdims=True)
    i2 = jnp.min(jnp.where(rest == m2, lane, float(LANES)), axis=-1, keepdims=True)
    t = jnp.exp(m2 - m1)
    comb_ref[...] = jnp.where(lane == i1, 1.0 / (1.0 + t), 0.0) + jnp.where(lane == i2, t / (1.0 + t), 0.0)
    sel = jnp.where(lane == i1, 1.0, jnp.where(lane == i2, 1.0, 0.0))
    selb = sel.astype(BF16)
    r = lax.broadcasted_iota(jnp.int32, (tm, tm), 0)
    c = lax.broadcasted_iota(jnp.int32, (tm, tm), 1)
    rank = jnp.dot(jnp.where(c < r, 1.0, 0.0).astype(BF16), selb, preferred_element_type=F32)
    pc_ref[...] = jnp.where(sel > 0.0, rank, -1.0)
    eye = jnp.where(lax.broadcasted_iota(jnp.int32, (8, LANES), 0) == lax.broadcasted_iota(jnp.int32, (8, LANES), 1),
                    1.0, 0.0).astype(BF16)
    sel_t = _dot_nt(eye, selb)
    rank_t = jnp.dot(sel_t.astype(BF16), jnp.where(r < c, 1.0, 0.0).astype(BF16), preferred_element_type=F32)
    pt_ref[...] = jnp.where(sel_t > 0.0, rank_t, -1.0)
    row8 = lax.broadcasted_iota(jnp.int32, (8, LANES), 0)
    cnt_ref[...] = jnp.where(row8 == 0, jnp.sum(sel, axis=0, keepdims=True),
                             jnp.where(row8 == 1, rank[tm // 2:tm // 2 + 1, :], 0.0))


def _route(x, ys, w_out, g, router_pad):
    tm = MOE_TM
    nt = TOKENS // tm
    tok = lambda width: pl.BlockSpec((tm, width), lambda i: (i, 0))
    return pl.pallas_call(
        _route_kernel,
        grid=(nt,),
        in_specs=_mix_specs(tm) + [pl.BlockSpec((1, D_MODEL), lambda i: (0, 0)),
                                   pl.BlockSpec((D_MODEL, LANES), lambda i: (0, 0))],
        out_specs=[tok(D_MODEL), tok(D_MODEL), tok(LANES), tok(LANES), pl.BlockSpec((8, tm), lambda i: (0, i)),
                   pl.BlockSpec((8, LANES), lambda i: (i, 0))],
        out_shape=[jax.ShapeDtypeStruct((TOKENS, D_MODEL), F32), jax.ShapeDtypeStruct((TOKENS, D_MODEL), BF16),
                   jax.ShapeDtypeStruct((TOKENS, LANES), F32), jax.ShapeDtypeStruct((TOKENS, LANES), F32),
                   jax.ShapeDtypeStruct((8, TOKENS), F32), jax.ShapeDtypeStruct((nt * 8, LANES), F32)],
        compiler_params=_params("parallel"),
        name="moe_route",
    )(x, *ys, w_out, g, router_pad)


def _moe_kernel(cnt_ref, x_ref, fg_ref, h_ref, pt_ref, pc_ref, comb_ref, wg_ref, wu_ref, wd_ref,
                acc_ref, xg_ref, yacc_ref):
    tm = h_ref.shape[0]
    sm, oc = MOE_STATIC_ROWS, MOE_CHUNK
    i = pl.program_id(0)
    e = pl.program_id(1)
    f = pl.program_id(2)
    n_over = jnp.maximum(cnt_ref[i * N_EXPERTS + e] - sm + oc - 1, 0) // oc

    def overflow(body):
        def step(c, carry):
            body(pl.multiple_of(sm + c * oc, oc), oc)
            return carry
        lax.fori_loop(0, n_over, step, 0)

    @pl.when((e == 0) & (f == 0))
    def _():
        acc_ref[...] = jnp.zeros(acc_ref.shape, F32)

    @pl.when(f == 0)
    def _():
        pt = pt_ref[pl.ds(e, 1), :]

        def gather(row0, nrows):
            ridx = lax.broadcasted_iota(jnp.int32, (nrows, tm), 0) + row0
            onehot = jnp.where(ridx.astype(F32) == pt, 1.0, 0.0).astype(BF16)
            xg_ref[pl.ds(row0, nrows), :] = jnp.dot(onehot, h_ref[...], preferred_element_type=F32).astype(BF16)
            yacc_ref[pl.ds(row0, nrows), :] = jnp.zeros((nrows, D_MODEL), F32)

        gather(0, sm)
        overflow(gather)

    def ffn(row0, nrows):
        rows = pl.ds(row0, nrows)
        yacc_ref[rows, :] += _swiglu(xg_ref[rows, :], wg_ref[0], wu_ref[0], wd_ref[0])

    ffn(0, sm)
    overflow(ffn)

    @pl.when(f == pl.num_programs(2) - 1)
    def _():
        lane = lax.broadcasted_iota(jnp.int32, (tm, LANES), 1)
        mine = lane == e
        pc = jnp.sum(jnp.where(mine, pc_ref[...], 0.0), axis=-1, keepdims=True)
        gate = jnp.sum(jnp.where(mine, comb_ref[...], 0.0), axis=-1, keepdims=True)

        def scatter(row0, nrows, tok0=0, ntok=tm):
            cidx = lax.broadcasted_iota(jnp.int32, (ntok, nrows), 1) + row0
            onehot = jnp.where(cidx.astype(F32) == pc[tok0:tok0 + ntok], 1.0, 0.0).astype(BF16)
            y = yacc_ref[pl.ds(row0, nrows), :].astype(BF16)
            acc_ref[tok0:tok0 + ntok, :] += jnp.dot(onehot, y, preferred_element_type=F32) * gate[tok0:tok0 + ntok]

        half = tm // 2
        n_half = cnt_ref[pl.num_programs(0) * N_EXPERTS + i * N_EXPERTS + e]
        windowed = (n_half >= MOE_HALF_MIN) & (n_half <= MOE_HALF_MAX)

        @pl.when(windowed)
        def _():
            scatter(0, MOE_HALF_MAX, 0, half)
            scatter(MOE_HALF_MIN, sm - MOE_HALF_MIN, half, half)

        @pl.when(jnp.logical_not(windowed))
        def _():
            scatter(0, sm)

        overflow(scatter)

    @pl.when((e == pl.num_programs(1) - 1) & (f == pl.num_programs(2) - 1))
    def _():
        acc_ref[...] = _rms(x_ref[...] + acc_ref[...], fg_ref[...])


def _moe(x, final_g, h, pt, pc, comb, counts, wg, wu, wd):
    tm, tf = MOE_TM, FFN_DIM // 2
    tok = lambda width: pl.BlockSpec((tm, width), lambda i, e, f, cnt: (i, 0))
    grid_spec = pltpu.PrefetchScalarGridSpec(
        num_scalar_prefetch=1,
        grid=(TOKENS // tm, N_EXPERTS, FFN_DIM // tf),
        in_specs=[tok(D_MODEL), pl.BlockSpec((1, D_MODEL), lambda i, e, f, cnt: (0, 0)), tok(D_MODEL),
                  pl.BlockSpec((8, tm), lambda i, e, f, cnt: (0, i)), tok(LANES), tok(LANES),
                  pl.BlockSpec((1, D_MODEL, tf), lambda i, e, f, cnt: (e, 0, f)),
                  pl.BlockSpec((1, D_MODEL, tf), lambda i, e, f, cnt: (e, 0, f)),
                  pl.BlockSpec((1, tf, D_MODEL), lambda i, e, f, cnt: (e, f, 0))],
        out_specs=tok(D_MODEL),
        scratch_shapes=[pltpu.VMEM((tm, D_MODEL), BF16), pltpu.VMEM((tm, D_MODEL), F32)])
    return pl.pallas_call(
        _moe_kernel,
        grid_spec=grid_spec,
        out_shape=jax.ShapeDtypeStruct((TOKENS, D_MODEL), F32),
        compiler_params=_params("parallel", "arbitrary", "arbitrary"),
        name="moe_ffn",
    )(counts, x, final_g, h, pt, pc, comb, wg, wu, wd)


def kernel(x, positions, attn_norm, w_in, b_forget, mla_q_norm, w_q_up, mla_kv_norm, w_kv_up, sinks, w_out, ffn_norm, dense_w_gate, dense_w_up, dense_w_down, router, moe_w_gate, moe_w_up, moe_w_down, final_norm):
    assert x.shape == (BATCH, SEQ, D_MODEL) and positions.shape == (BATCH, SEQ)
    p_idx, p_sgn = _proj_columns()
    q_idx, q_sgn = _mla_q_columns()
    kv_idx, kv_sgn = _mla_kv_columns()
    mix_rows = _mix_rows()
    cos, sin = _rope_tables(positions)
    xt = x.reshape(TOKENS, D_MODEL).astype(F32)
    assert DEPTH == 2 and moe_w_gate.shape[0] == 1
    to_round = [moe_w_gate[0], moe_w_up[0], moe_w_down[0]]
    moe_w = []
    for layer in range(DEPTH):
        w = _take_columns(w_in[layer], p_idx, p_sgn).astype(BF16)
        wq = _take_columns(w_q_up[layer], q_idx, q_sgn).astype(BF16)
        wkv = _take_columns(w_kv_up[layer], kv_idx, kv_sgn).astype(BF16)
        fq, fk, fv, fl, pb, qc, kc, vc, pd = _project(
            xt, attn_norm[layer].reshape(1, D_MODEL), w, mla_q_norm[layer].reshape(1, MLA_Q_LORA), wq,
            mla_kv_norm[layer].reshape(1, MLA_KV_LORA), wkv, cos, sin)
        y_a, rounded = _flash(fq, _fox_keys(fl, b_forget[layer], fk), fv, "fox_attention",
                              to_round.pop(0) if to_round else None, decay_skip=True)
        moe_w += [rounded] if rounded is not None else []
        y_b = _swa(pb, sinks[layer])
        y_c, rounded = _flash(qc, kc, vc, "mla_attention", to_round.pop(0) if to_round else None)
        moe_w += [rounded] if rounded is not None else []
        y_d = _dilated(pd)
        cuts = [0] + [r for r in range(1, len(mix_rows)) if mix_rows[r] != mix_rows[r - 1] + 1] + [len(mix_rows)]
        w_o = jnp.concatenate([w_out[layer][int(mix_rows[a]):int(mix_rows[a]) + (b - a)]
                               for a, b in zip(cuts[:-1], cuts[1:])], axis=0).astype(BF16)
        ys = (y_a, y_b, y_c, y_d)
        j = layer // 2
        g = ffn_norm[layer].reshape(1, D_MODEL)
        if layer % 2 == 0:
            xt = _ffn(xt, ys, w_o, g, dense_w_gate, dense_w_up, dense_w_down, j)
        else:
            assert layer == DEPTH - 1
            router_pad = jnp.zeros((D_MODEL, LANES), F32).at[:, :N_EXPERTS].set(router[j])
            xt, h, comb, pc, pt, cnt = _route(xt, ys, w_o, g, router_pad)
            counts = jnp.concatenate([cnt[0::8, :N_EXPERTS].reshape(-1), cnt[1::8, :N_EXPERTS].reshape(-1)])
            counts = counts.astype(jnp.int32)
            assert len(moe_w) == 3
            xt = _moe(xt, final_norm.reshape(1, D_MODEL), h, pt, pc, comb, counts, *moe_w)
    return xt.reshape(BATCH, SEQ, D_MODEL)
```

```python
import functools

import numpy as np
import jax
import jax.numpy as jnp
from jax import lax
from jax.experimental import pallas as pl
from jax.experimental.pallas import tpu as pltpu

D_MODEL = 1024
BATCH = 2
SEQ = 8192
DEPTH = 2
TOKENS = BATCH * SEQ
HEAD_DIM = 64
BAND = 128
NORM_EPS = 1e-6
FOX_HEADS = 4
SWA_Q_HEADS = 4
SWA_KV_HEADS = 2
SWA_WINDOW = 128
MLA_HEADS = 4
MLA_Q_LORA = 256
MLA_KV_LORA = 128
MLA_NOPE_DIM = 64
MLA_ROPE_DIM = 32
MLA_V_DIM = 64
ROPE_THETA = 10000.0
DIL_HEADS = 4
DIL_PATTERNS = ((128, 1), (512, 4), (2048, 16))
DIL_BLOCK = BAND * max(d for _, d in DIL_PATTERNS)
FFN_DIM = 3584
N_EXPERTS = 8
LANES = 128
NEG = -1e30
VMEM_LIMIT = 56 * 1024 * 1024

_OFF = np.cumsum([0, 256, 256, 256, 4, 256, 128, 128, 256, 128, 32, 256, 256, 256])
(_A_Q, _A_K, _A_V, _A_F, _B_Q, _B_K, _B_V, _C_Q, _C_KV, _C_KR, _D_Q, _D_K, _D_V) = _OFF[:13].tolist()
N_PROJ_BLOCKS = 22
LOG2E = 1.4426950408889634

BF16 = jnp.bfloat16
F32 = jnp.float32


def _alibi_slopes():
    n = SWA_Q_HEADS + DIL_HEADS
    return [2.0 ** (-8.0 * i / n) for i in range(1, n + 1)]


def _proj_columns():
    idx = np.zeros((N_PROJ_BLOCKS * LANES,), np.int32)
    sgn = np.zeros((N_PROJ_BLOCKS * LANES,), np.float32)

    def put(dst, src, n, sign=1.0):
        idx[dst:dst + n] = np.arange(src, src + n)
        sgn[dst:dst + n] = sign

    put(0, _A_Q, 256); put(256, _A_K, 256); put(512, _A_V, 256)
    for blk, heads in ((6, (0, 2)), (7, (1, 3))):
        for half, h in enumerate(heads):
            put(blk * LANES + half * HEAD_DIM, _B_Q + h * HEAD_DIM, HEAD_DIM)
    put(8 * LANES, _B_K, 128); put(9 * LANES, _B_V, 128)
    put(10 * LANES, _C_Q, 256); put(12 * LANES, _C_KV, 128)
    half = MLA_ROPE_DIM // 2
    put(13 * LANES + MLA_NOPE_DIM, _C_KR, MLA_ROPE_DIM)
    put(14 * LANES + MLA_NOPE_DIM, _C_KR + half, half, -1.0)
    put(14 * LANES + MLA_NOPE_DIM + half, _C_KR, half)
    put(15 * LANES, _D_Q, 256); put(17 * LANES, _D_K, 256); put(19 * LANES, _D_V, 256)
    put(21 * LANES, _A_F, FOX_HEADS)
    return idx, sgn


def _bias_lane0(h):
    return HEAD_DIM if h % 2 == 0 else 0


def _bias_placement():
    place = np.zeros((3 * LANES, FOX_HEADS * LANES), np.float32)
    for piece in range(3):
        for h in range(FOX_HEADS):
            place[piece * LANES + h, h * LANES + _bias_lane0(h) + piece] = 1.0
    return place


def _mla_q_columns():
    idx = np.zeros((8 * LANES,), np.int32)
    sgn = np.zeros((8 * LANES,), np.float32)
    half = MLA_ROPE_DIM // 2
    dq = MLA_NOPE_DIM + MLA_ROPE_DIM
    for h in range(MLA_HEADS):
        a = h * LANES
        idx[a:a + dq] = np.arange(h * dq, (h + 1) * dq); sgn[a:a + dq] = 1.0
        b = (MLA_HEADS + h) * LANES + MLA_NOPE_DIM
        r = h * dq + MLA_NOPE_DIM
        idx[b:b + half] = np.arange(r + half, r + 2 * half); sgn[b:b + half] = -1.0
        idx[b + half:b + 2 * half] = np.arange(r, r + half); sgn[b + half:b + 2 * half] = 1.0
    return idx, sgn


def _mla_kv_columns():
    idx = np.zeros((6 * LANES,), np.int32)
    sgn = np.zeros((6 * LANES,), np.float32)
    dkv = MLA_NOPE_DIM + MLA_V_DIM
    for h in range(MLA_HEADS):
        idx[h * LANES:h * LANES + MLA_NOPE_DIM] = np.arange(h * dkv, h * dkv + MLA_NOPE_DIM)
        sgn[h * LANES:h * LANES + MLA_NOPE_DIM] = 1.0
        b = MLA_HEADS * LANES + h * MLA_V_DIM
        idx[b:b + MLA_V_DIM] = np.arange(h * dkv + MLA_NOPE_DIM, (h + 1) * dkv)
        sgn[b:b + MLA_V_DIM] = 1.0
    return idx, sgn


def _take_columns(w, idx, sgn):
    parts = []
    a = 0
    while a < len(idx):
        b = a + 1
        while b < len(idx) and sgn[b] == sgn[a] and (sgn[a] == 0 or idx[b] == idx[b - 1] + 1):
            b += 1
        if sgn[a] == 0:
            parts.append(jnp.zeros((w.shape[0], b - a), w.dtype))
        else:
            piece = w[:, int(idx[a]):int(idx[a]) + (b - a)]
            parts.append(piece if sgn[a] > 0 else -piece)
        a = b
    return jnp.concatenate(parts, axis=1)


def _mix_rows():
    rows = np.arange(4 * 256)
    b = 256
    perm = np.concatenate([np.arange(b + h * HEAD_DIM, b + (h + 1) * HEAD_DIM) for h in (0, 2, 1, 3)])
    rows[b:b + 256] = perm
    return rows


def _rms(x, g):
    return x * lax.rsqrt(jnp.mean(x * x, axis=-1, keepdims=True) + NORM_EPS) * g


def _dot_nt(a, b):
    return lax.dot_general(a, b, (((1,), (1,)), ((), ())), preferred_element_type=F32)


def _lane_tile(x, width):
    return x if width == LANES else jnp.concatenate([x] * (width // LANES), axis=1)


def _params(*sem):
    return pltpu.CompilerParams(dimension_semantics=sem, vmem_limit_bytes=VMEM_LIMIT)


def _rope_table_kernel(pos_ref, invf_ref, cos_ref, sin_ref):
    ang = pos_ref[...].astype(F32) * invf_ref[...]
    cos_ref[...] = jnp.cos(ang)
    sin_ref[...] = jnp.sin(ang)


def _rope_tables(positions):
    tm = 2048
    half = MLA_ROPE_DIM // 2
    invf = np.zeros((1, LANES), np.float32)
    f = (ROPE_THETA ** (-np.arange(half, dtype=np.float32) / np.float32(half))).astype(np.float32)
    invf[0, MLA_NOPE_DIM:MLA_NOPE_DIM + half] = f
    invf[0, MLA_NOPE_DIM + half:MLA_NOPE_DIM + 2 * half] = f
    return pl.pallas_call(
        _rope_table_kernel,
        grid=(TOKENS // tm,),
        in_specs=[pl.BlockSpec((tm, 1), lambda i: (i, 0)), pl.BlockSpec((1, LANES), lambda i: (0, 0))],
        out_specs=[pl.BlockSpec((tm, LANES), lambda i: (i, 0))] * 2,
        out_shape=[jax.ShapeDtypeStruct((TOKENS, LANES), F32)] * 2,
        compiler_params=_params("parallel"),
        name="rope_tables",
    )(positions.reshape(TOKENS, 1), jnp.asarray(invf))


def _proj_kernel(x_ref, g_ref, w_ref, qn_ref, wq_ref, kvn_ref, wkv_ref, cos_ref, sin_ref,
                 fq_ref, fk_ref, fv_ref, fl_ref, pb_ref, qc_ref, kc_ref, vc_ref, pd_ref):
    hb = _rms(x_ref[...], g_ref[...]).astype(BF16)
    res = jnp.dot(hb, w_ref[...], preferred_element_type=F32)
    lane = lax.broadcasted_iota(jnp.int32, (x_ref.shape[0], LANES), 1)
    low = lane < HEAD_DIM

    def blk(j, n=1):
        return res[:, j * LANES:(j + n) * LANES]

    def own(h):
        return low if h % 2 == 0 else ~low

    qscale = HEAD_DIM ** -0.5 * LOG2E
    for h in range(FOX_HEADS):
        ones = (lane >= _bias_lane0(h)) & (lane < _bias_lane0(h) + 3)
        fq_ref[h] = jnp.where(own(h), blk(h // 2) * qscale, jnp.where(ones, 1.0, 0.0)).astype(BF16)
        fv_ref[h] = jnp.where(own(h), blk(4 + h // 2), 1.0).astype(BF16)
    for j in range(2):
        fk_ref[j] = blk(2 + j).astype(BF16)
    fl_ref[...] = blk(21)
    for j in range(2):
        pb_ref[j] = (blk(6 + j) * qscale).astype(BF16)
    pb_ref[2] = blk(8).astype(BF16)
    pb_ref[3] = blk(9).astype(BF16)

    cos = cos_ref[...]
    sin = sin_ref[...]
    cq = _rms(blk(10, 2), qn_ref[...]).astype(BF16)
    qab = jnp.dot(cq, wq_ref[...], preferred_element_type=F32)
    mla_scale = (MLA_NOPE_DIM + MLA_ROPE_DIM) ** -0.5
    for h in range(MLA_HEADS):
        qa = qab[:, h * LANES:(h + 1) * LANES]
        qb = qab[:, (MLA_HEADS + h) * LANES:(MLA_HEADS + h + 1) * LANES]
        qc_ref[h] = ((qa * cos + qb * sin) * (mla_scale * LOG2E)).astype(BF16)
    ckv = _rms(blk(12), kvn_ref[...]).astype(BF16)
    kv = jnp.dot(ckv, wkv_ref[...], preferred_element_type=F32)
    k_rot = blk(13) * cos + blk(14) * sin
    for h in range(MLA_HEADS):
        kc_ref[h] = (kv[:, h * LANES:(h + 1) * LANES] + k_rot).astype(BF16)
        v_pair = kv[:, (MLA_HEADS + h // 2) * LANES:(MLA_HEADS + h // 2 + 1) * LANES]
        vc_ref[h] = jnp.where(own(h), v_pair, 1.0).astype(BF16)

    for j in range(2):
        pd_ref[j] = blk(15 + j) * qscale
    for j in range(2, 6):
        pd_ref[j] = blk(15 + j)


def _project(x, g, w, qn, wq, kvn, wkv, cos, sin):
    tm = 512
    full = lambda shape: pl.BlockSpec(shape, lambda i: (0,) * len(shape))
    out_blk = lambda n: pl.BlockSpec((n, tm, LANES), lambda i: (0, i, 0))
    out_sds = lambda n, dt: jax.ShapeDtypeStruct((n, TOKENS, LANES), dt)
    tok_blk = pl.BlockSpec((tm, LANES), lambda i: (i, 0))
    return pl.pallas_call(
        _proj_kernel,
        grid=(TOKENS // tm,),
        in_specs=[pl.BlockSpec((tm, D_MODEL), lambda i: (i, 0)), full((1, D_MODEL)),
                  full((D_MODEL, N_PROJ_BLOCKS * LANES)),
                  full((1, MLA_Q_LORA)), full((MLA_Q_LORA, 8 * LANES)),
                  full((1, MLA_KV_LORA)), full((MLA_KV_LORA, 6 * LANES)), tok_blk, tok_blk],
        out_specs=[out_blk(4), out_blk(2), out_blk(4), tok_blk, out_blk(4), out_blk(4), out_blk(4), out_blk(4),
                   out_blk(6)],
        out_shape=[out_sds(4, BF16), out_sds(2, BF16), out_sds(4, BF16), jax.ShapeDtypeStruct((TOKENS, LANES), F32),
                   out_sds(4, BF16), out_sds(4, BF16), out_sds(4, BF16), out_sds(4, BF16), out_sds(6, F32)],
        compiler_params=_params("parallel"),
        name="in_proj",
    )(x, g, w, qn, wq, kvn, wkv, cos, sin)


def _split3(x):
    hi = x.astype(BF16)
    r1 = x - hi.astype(F32)
    mid = r1.astype(BF16)
    return hi, mid, (r1 - mid.astype(F32)).astype(BF16)


def _fox_keys_kernel(fl_ref, b_ref, k_ref, place_ref, kf_ref, carry_ref, *, cs):
    @pl.when(pl.program_id(1) == 0)
    def _():
        carry_ref[...] = jnp.zeros_like(carry_ref)

    lf = jax.nn.log_sigmoid(fl_ref[...] + b_ref[...])
    row = lax.broadcasted_iota(jnp.int32, (cs, cs), 0)
    col = lax.broadcasted_iota(jnp.int32, (cs, cs), 1)
    tri = jnp.where(col <= row, 1.0, 0.0).astype(BF16)
    cum = carry_ref[0:1, :] + sum(jnp.dot(tri, piece, preferred_element_type=F32) for piece in _split3(lf))
    carry_ref[0:1, :] = cum[cs - 1:cs, :]
    pieces = jnp.concatenate(_split3(cum * (-LOG2E)), axis=1)
    placed = jnp.dot(pieces, place_ref[...], preferred_element_type=F32).astype(BF16)
    low = lax.broadcasted_iota(jnp.int32, (cs, LANES), 1) < HEAD_DIM
    for h in range(FOX_HEADS):
        kf_ref[h] = jnp.where(low if h % 2 == 0 else ~low, k_ref[h // 2], placed[:, h * LANES:(h + 1) * LANES])


def _fox_keys(fl, b_forget, fk):
    cs = 512
    nc = SEQ // cs
    b_row = jnp.zeros((1, LANES), F32).at[0, :FOX_HEADS].set(b_forget.astype(F32))
    return pl.pallas_call(
        functools.partial(_fox_keys_kernel, cs=cs),
        grid=(BATCH, nc),
        in_specs=[pl.BlockSpec((cs, LANES), lambda b, i: (b * nc + i, 0)),
                  pl.BlockSpec((1, LANES), lambda b, i: (0, 0)),
                  pl.BlockSpec((2, cs, LANES), lambda b, i: (0, b * nc + i, 0)),
                  pl.BlockSpec((3 * LANES, FOX_HEADS * LANES), lambda b, i: (0, 0))],
        out_specs=pl.BlockSpec((FOX_HEADS, cs, LANES), lambda b, i: (0, b * nc + i, 0)),
        out_shape=jax.ShapeDtypeStruct((FOX_HEADS, TOKENS, LANES), BF16),
        scratch_shapes=[pltpu.VMEM((8, LANES), F32)],
        compiler_params=_params("parallel", "arbitrary"),
        name="fox_keys",
    )(fl, b_row, fk, jnp.asarray(_bias_placement(), BF16))


FLASH_UNDERFLOW = -160.0


def _flash_kernel(*refs, tq, tk, n_cast, decay_skip):
    q_ref, k_ref, v_ref = refs[:3]
    o_ref = refs[3 + n_cast]
    acc_ref, m_ref, kmax_ref = refs[-3:]
    for src_ref, dst_ref in zip(refs[3:3 + n_cast], refs[4 + n_cast:4 + 2 * n_cast]):
        dst_ref[...] = src_ref[...].astype(dst_ref.dtype)
    i = pl.program_id(2)
    m_ref[...] = jnp.full(m_ref.shape, NEG, F32)
    acc_ref[...] = jnp.zeros(acc_ref.shape, F32)

    def sq_norms(x, h):
        lane_in = lax.broadcasted_iota(jnp.int32, (LANES, LANES), 0)
        ones = jnp.where((lane_in < HEAD_DIM) == (h == 0), 1.0, 0.0).astype(BF16)
        xf = x.astype(F32)
        return jnp.dot((xf * xf).astype(BF16), ones, preferred_element_type=F32)

    if decay_skip:
        @pl.when(i == 0)
        def _():
            for h in range(2):
                kmax_ref[h] = jnp.broadcast_to(jnp.max(sq_norms(k_ref[h], h), axis=0, keepdims=True), (8, LANES))

    def step(kb, row0, diagonal):
        ks = pl.multiple_of(kb * tk, tk)
        rows = pl.ds(row0, tq - row0)
        for h in range(2):
            s = _dot_nt(q_ref[h, rows, :], k_ref[h, pl.ds(ks, tk), :])
            if diagonal:
                r = lax.broadcasted_iota(jnp.int32, s.shape, 0)
                c = lax.broadcasted_iota(jnp.int32, s.shape, 1)
                s = jnp.where(c <= r, s, NEG)
            m_old = m_ref[h, rows, :]
            m_new = jnp.maximum(m_old, jnp.max(s, axis=-1, keepdims=True))
            p = jnp.exp2(s - _lane_tile(m_new, tk)).astype(BF16)
            acc_ref[h, rows, :] = (jnp.exp2(m_old - m_new) * acc_ref[h, rows, :]
                                   + jnp.dot(p, v_ref[h, pl.ds(ks, tk), :], preferred_element_type=F32))
            m_ref[h, rows, :] = m_new

    per_q = tq // tk
    if not decay_skip:
        def body(kb, carry):
            step(kb, 0, False)
            return carry

        lax.fori_loop(0, i * per_q, body, 0)
        for u in range(per_q):
            step(i * per_q + u, u * tk, True)
    else:
        for u in range(per_q):
            step(i * per_q + u, u * tk, True)
        slack = []
        for h in range(2):
            qk_bound = jnp.sqrt(sq_norms(q_ref[h], h) * kmax_ref[h, 0:1, :]) * 1.02
            slack.append(jnp.max(qk_bound - m_ref[h]))
        sub = lax.broadcasted_iota(jnp.int32, (16, LANES), 0)
        lane = lax.broadcasted_iota(jnp.int32, (16, LANES), 1)

        def last_bias(kb, h):
            rows = k_ref[h, pl.ds(pl.multiple_of((kb + 1) * tk - 16, 16), 16), :].astype(F32)
            pick = (sub == 15) & (lane >= _bias_lane0(h)) & (lane < _bias_lane0(h) + 3)
            return jnp.sum(jnp.where(pick, rows, 0.0))

        def visible(kb):
            kb0 = jnp.maximum(kb, 0)
            return (kb >= 0) & ((slack[0] + last_bias(kb0, 0) > FLASH_UNDERFLOW)
                                | (slack[1] + last_bias(kb0, 1) > FLASH_UNDERFLOW))

        def body(state):
            kb, _ = state
            step(kb, 0, False)
            return kb - 1, visible(kb - 1)

        lax.while_loop(lambda state: state[1], body, (i * per_q - 1, visible(i * per_q - 1)))
    outs = [acc_ref[h] / pltpu.roll(acc_ref[h], HEAD_DIM, axis=1) for h in range(2)]
    low = lax.broadcasted_iota(jnp.int32, (tq, LANES), 1) < HEAD_DIM
    o_ref[0] = jnp.where(low, outs[0], outs[1]).astype(o_ref.dtype)


def _flash(q, k, v, name, w_f32=None, decay_skip=False):
    tq, tk = 2048, 512
    nq = SEQ // tq
    w_args, w_specs, w_shapes = [], [], []
    if w_f32 is not None:
        per_expert = BATCH * 2 * nq // N_EXPERTS
        assert per_expert * N_EXPERTS == BATCH * 2 * nq and w_f32.shape[1] % (16 * per_expert) == 0
        slab = lambda b, j, i: divmod((b * 2 + j) * nq + i, per_expert) + (0,)
        w_args = [w_f32]
        w_specs = [pl.BlockSpec((1, w_f32.shape[1] // per_expert, w_f32.shape[2]), slab)]
        w_shapes = [jax.ShapeDtypeStruct(w_f32.shape, BF16)]
    kv_spec = pl.BlockSpec((2, SEQ, LANES), lambda b, j, i: (j, b, 0))
    outs = pl.pallas_call(
        functools.partial(_flash_kernel, tq=tq, tk=tk, n_cast=len(w_args), decay_skip=decay_skip),
        grid=(BATCH, 2, nq),
        in_specs=[pl.BlockSpec((2, tq, LANES), lambda b, j, i: (j, b * nq + i, 0)), kv_spec, kv_spec] + w_specs,
        out_specs=[pl.BlockSpec((1, tq, LANES), lambda b, j, i: (j, b * nq + i, 0))] + w_specs,
        out_shape=[jax.ShapeDtypeStruct((2, TOKENS, LANES), BF16)] + w_shapes,
        scratch_shapes=[pltpu.VMEM((2, tq, LANES), F32)] * 2 + [pltpu.VMEM((2, 8, LANES), F32)],
        compiler_params=_params("parallel", "parallel", "arbitrary"),
        name=name,
    )(q, k, v, *w_args)
    return outs[0], (outs[1] if w_args else None)


def _band_mask_bias(span, slope_step, first):
    qi = lax.broadcasted_iota(jnp.int32, (BAND, 2 * BAND), 0)
    kj = lax.broadcasted_iota(jnp.int32, (BAND, 2 * BAND), 1)
    dist = BAND + qi - kj
    ok = (dist >= 0) & (dist <= span) & (kj >= jnp.where(first, BAND, 0))
    return jnp.where(ok, dist.astype(F32) * (-slope_step), NEG)


def _band_pair(q, kk, vv, mask_bias):
    low = lax.broadcasted_iota(jnp.int32, (BAND, LANES), 1) < HEAD_DIM
    low_kv = lax.broadcasted_iota(jnp.int32, (2 * BAND, LANES), 1) < HEAD_DIM
    accs, ms = [], []
    for half in range(2):
        qm = jnp.where(low if half == 0 else ~low, q, jnp.zeros_like(q))
        vh = jnp.where(low_kv if half == 0 else ~low_kv, vv, jnp.ones_like(vv))
        s = _dot_nt(qm, kk) + mask_bias[half]
        m = jnp.max(s, axis=-1, keepdims=True)
        accs.append(jnp.dot(jnp.exp2(s - m).astype(BF16), vh, preferred_element_type=F32))
        ms.append(m)
    l = pltpu.roll(jnp.where(low, accs[1], accs[0]), HEAD_DIM, axis=1)
    return jnp.where(low, accs[0], accs[1]) / l, jnp.where(low, ms[0], ms[1]) + jnp.log2(l)


def _swa_kernel(q_ref, k_ref, kp_ref, v_ref, vp_ref, sink_ref, o_ref, kk_ref, vv_ref, *, tb, slopes):
    n = pl.program_id(1)
    kk_ref[0:BAND] = kp_ref[0]
    kk_ref[BAND:] = k_ref[0]
    vv_ref[0:BAND] = vp_ref[0]
    vv_ref[BAND:] = v_ref[0]
    mask_bias = [[_band_mask_bias(SWA_WINDOW - 1, slope * LOG2E, first) for slope in slopes]
                 for first in (n == 0, False)]
    for c in range(tb // BAND):
        kk = kk_ref[c * BAND:(c + 2) * BAND]
        vv = vv_ref[c * BAND:(c + 2) * BAND]
        mb = mask_bias[0 if c == 0 else 1]
        for jb in range(2):
            o, lse2 = _band_pair(q_ref[jb, c * BAND:(c + 1) * BAND, :], kk, vv, (mb[jb], mb[jb + 2]))
            o = o / (1.0 + jnp.exp2(sink_ref[jb:jb + 1, :] - lse2))
            o_ref[jb, c * BAND:(c + 1) * BAND, :] = o.astype(o_ref.dtype)


def _swa(pb, sinks):
    tb = 512
    nb = SEQ // tb
    r = tb // BAND
    s = sinks.astype(F32) * LOG2E
    sink_lanes = jnp.stack([jnp.concatenate([jnp.full((HEAD_DIM,), s[jb]), jnp.full((HEAD_DIM,), s[jb + 2])])
                            for jb in range(2)])
    cur = lambda blk: pl.BlockSpec((1, tb, LANES), lambda b, n: (blk, b * nb + n, 0))
    prev = lambda blk: pl.BlockSpec((1, BAND, LANES), lambda b, n: (blk, jnp.maximum((b * nb + n) * r - 1, 0), 0))
    return pl.pallas_call(
        functools.partial(_swa_kernel, tb=tb, slopes=_alibi_slopes()[:SWA_Q_HEADS]),
        grid=(BATCH, nb),
        in_specs=[pl.BlockSpec((2, tb, LANES), lambda b, n: (0, b * nb + n, 0)), cur(2), prev(2), cur(3), prev(3),
                  pl.BlockSpec((2, LANES), lambda b, n: (0, 0))],
        out_specs=pl.BlockSpec((2, tb, LANES), lambda b, n: (0, b * nb + n, 0)),
        out_shape=jax.ShapeDtypeStruct((2, TOKENS, LANES), BF16),
        scratch_shapes=[pltpu.VMEM((tb + BAND, LANES), BF16)] * 2,
        compiler_params=_params("parallel", "parallel"),
        name="swa_attention",
    )(pb, pb, pb, pb, pb, sink_lanes)


def _dil_kernel(q_ref, k_ref, kp_ref, v_ref, vp_ref, o_ref, kk_ref, vv_ref, po_ref, pl_ref, *, slopes):
    pair = pl.program_id(1)
    n = pl.program_id(2)
    kk_ref[0:DIL_BLOCK] = kp_ref[0]
    kk_ref[DIL_BLOCK:] = k_ref[0]
    vv_ref[0:DIL_BLOCK] = vp_ref[0]
    vv_ref[DIL_BLOCK:] = v_ref[0]
    units = DIL_BLOCK // BAND
    for p, (window, dil) in enumerate(DIL_PATTERNS):
        steps = [jnp.where(pair == 0, slopes[half], slopes[2 + half]) * (dil * LOG2E) for half in range(2)]
        mask_bias = [[_band_mask_bias(window // dil, step, first) for step in steps] for first in (n == 0, False)]

        def unit(u, carry, p=p, dil=dil, mb=None):
            start = (u // dil) * (BAND * dil) + u % dil
            if dil == 1:
                rows = pl.ds(pl.multiple_of(start, BAND), BAND)
                krows = pl.ds(pl.multiple_of(DIL_BLOCK + start - BAND, BAND), 2 * BAND)
            else:
                rows = pl.ds(start, BAND, stride=dil)
                krows = pl.ds(DIL_BLOCK + start - BAND * dil, 2 * BAND, stride=dil)
            o, lse2 = _band_pair(q_ref[0, rows, :].astype(BF16), kk_ref[krows, :].astype(BF16),
                                 vv_ref[krows, :].astype(BF16), mb)
            po_ref[p, rows, :] = o
            pl_ref[p, rows, :] = lse2
            return carry

        lax.fori_loop(0, dil, functools.partial(unit, mb=mask_bias[0]), 0, unroll=min(dil, 8))
        if dil < units:
            lax.fori_loop(dil, units, functools.partial(unit, mb=mask_bias[1]), 0, unroll=min(units - dil, 8))
    chunk = 256

    def merge(t, carry):
        rows = pl.ds(pl.multiple_of(t * chunk, chunk), chunk)
        lse = pl_ref[:, rows, :]
        w = jnp.exp2(lse - jnp.max(lse, axis=0, keepdims=True))
        o_ref[0, rows, :] = (jnp.sum(w * po_ref[:, rows, :], axis=0) / jnp.sum(w, axis=0)).astype(o_ref.dtype)
        return carry

    lax.fori_loop(0, DIL_BLOCK // chunk, merge, 0)


def _dilated(pd):
    nb = SEQ // DIL_BLOCK
    cur = lambda off: pl.BlockSpec((1, DIL_BLOCK, LANES), lambda b, j, n: (off + j, b * nb + n, 0))
    prev = lambda off: pl.BlockSpec((1, DIL_BLOCK, LANES),
                                    lambda b, j, n: (off + j, b * nb + jnp.maximum(n - 1, 0), 0))
    return pl.pallas_call(
        functools.partial(_dil_kernel, slopes=_alibi_slopes()[SWA_Q_HEADS:]),
        grid=(BATCH, 2, nb),
        in_specs=[cur(0), cur(2), prev(2), cur(4), prev(4)],
        out_specs=pl.BlockSpec((1, DIL_BLOCK, LANES), lambda b, j, n: (j, b * nb + n, 0)),
        out_shape=jax.ShapeDtypeStruct((2, TOKENS, LANES), BF16),
        scratch_shapes=[pltpu.VMEM((2 * DIL_BLOCK, LANES), F32)] * 2
                       + [pltpu.VMEM((len(DIL_PATTERNS), DIL_BLOCK, LANES), F32)] * 2,
        compiler_params=_params("parallel", "parallel", "parallel"),
        name="dilated_attention",
    )(pd, pd, pd, pd, pd)


def _mix_residual(x_ref, y_refs, w_ref):
    mixed = jnp.concatenate([y[j] for y in y_refs for j in range(2)], axis=1)
    return x_ref[...] + jnp.dot(mixed, w_ref[...], preferred_element_type=F32)


def _mix_specs(tm):
    return ([pl.BlockSpec((tm, D_MODEL), lambda i, *_: (i, 0))]
            + [pl.BlockSpec((2, tm, LANES), lambda i, *_: (0, i, 0))] * 4
            + [pl.BlockSpec((D_MODEL, D_MODEL), lambda i, *_: (0, 0))])


def _swiglu(h, wg, wu, wd):
    gate = jnp.dot(h, wg, preferred_element_type=F32)
    up = jnp.dot(h, wu, preferred_element_type=F32)
    act = (gate * jax.nn.sigmoid(gate) * up).astype(BF16)
    return jnp.dot(act, wd, preferred_element_type=F32)


def _ffn_kernel(x_ref, ya_ref, yb_ref, yc_ref, yd_ref, wo_ref, g_ref, wg_ref, wu_ref, wd_ref, o_ref, h_ref, acc_ref):
    f = pl.program_id(1)

    @pl.when(f == 0)
    def _():
        x = _mix_residual(x_ref, (ya_ref, yb_ref, yc_ref, yd_ref), wo_ref)
        h_ref[...] = _rms(x, g_ref[...]).astype(BF16)
        acc_ref[...] = x

    acc_ref[...] += _swiglu(h_ref[...], wg_ref[...].astype(BF16), wu_ref[...].astype(BF16), wd_ref[...].astype(BF16))

    @pl.when(f == pl.num_programs(1) - 1)
    def _():
        o_ref[...] = acc_ref[...]


def _ffn(x, ys, w_out, g, wg, wu, wd, j):
    tm, tf = 1024, 512
    return pl.pallas_call(
        _ffn_kernel,
        grid=(TOKENS // tm, FFN_DIM // tf),
        in_specs=_mix_specs(tm) + [
                  pl.BlockSpec((1, D_MODEL), lambda i, f: (0, 0)),
                  pl.BlockSpec((None, D_MODEL, tf), lambda i, f: (j, 0, f)),
                  pl.BlockSpec((None, D_MODEL, tf), lambda i, f: (j, 0, f)),
                  pl.BlockSpec((None, tf, D_MODEL), lambda i, f: (j, f, 0))],
        out_specs=pl.BlockSpec((tm, D_MODEL), lambda i, f: (i, 0)),
        out_shape=jax.ShapeDtypeStruct((TOKENS, D_MODEL), F32),
        scratch_shapes=[pltpu.VMEM((tm, D_MODEL), BF16), pltpu.VMEM((tm, D_MODEL), F32)],
        compiler_params=_params("parallel", "arbitrary"),
        name="dense_ffn",
    )(x, *ys, w_out, g, wg, wu, wd)


MOE_TM = 1024
MOE_STATIC_ROWS = 288
MOE_CHUNK = 32
MOE_HALF_MIN = 96
MOE_HALF_MAX = 160


def _route_kernel(x_ref, ya_ref, yb_ref, yc_ref, yd_ref, wo_ref, g_ref, router_ref,
                  x1_ref, h_ref, comb_ref, pc_ref, pt_ref, cnt_ref):
    tm = x_ref.shape[0]
    x1 = _mix_residual(x_ref, (ya_ref, yb_ref, yc_ref, yd_ref), wo_ref)
    x1_ref[...] = x1
    h = _rms(x1, g_ref[...])
    h_ref[...] = h.astype(BF16)
    lane = lax.broadcasted_iota(jnp.int32, (tm, LANES), 1).astype(F32)
    h_hi, h_lo, _ = _split3(h)
    r_hi, r_lo, _ = _split3(router_ref[...])
    logits = (jnp.dot(h_hi, r_hi, preferred_element_type=F32) + jnp.dot(h_hi, r_lo, preferred_element_type=F32)
              + jnp.dot(h_lo, r_hi, preferred_element_type=F32))
    logits = jnp.where(lane < N_EXPERTS, logits, NEG)
    m1 = jnp.max(logits, axis=-1, keepdims=True)
    i1 = jnp.min(jnp.where(logits == m1, lane, float(LANES)), axis=-1, keepdims=True)
    rest = jnp.where(lane == i1, NEG, logits)
    m2 = jnp.max(rest, axis=-1, keepdims=True)
    i2 = jnp.min(jnp.where(rest == m2, lane, float(LANES)), axis=-1, keepdims=True)
    t = jnp.exp(m2 - m1)
    comb_ref[...] = jnp.where(lane == i1, 1.0 / (1.0 + t), 0.0) + jnp.where(lane == i2, t / (1.0 + t), 0.0)
    sel = jnp.where(lane == i1, 1.0, jnp.where(lane == i2, 1.0, 0.0))
    selb = sel.astype(BF16)
    r = lax.broadcasted_iota(jnp.int32, (tm, tm), 0)
    c = lax.broadcasted_iota(jnp.int32, (tm, tm), 1)
    rank = jnp.dot(jnp.where(c < r, 1.0, 0.0).astype(BF16), selb, preferred_element_type=F32)
    pc_ref[...] = jnp.where(sel > 0.0, rank, -1.0)
    eye = jnp.where(lax.broadcasted_iota(jnp.int32, (8, LANES), 0) == lax.broadcasted_iota(jnp.int32, (8, LANES), 1),
                    1.0, 0.0).astype(BF16)
    sel_t = _dot_nt(eye, selb)
    rank_t = jnp.dot(sel_t.astype(BF16), jnp.where(r < c, 1.0, 0.0).astype(BF16), preferred_element_type=F32)
    pt_ref[...] = jnp.where(sel_t > 0.0, rank_t, -1.0)
    row8 = lax.broadcasted_iota(jnp.int32, (8, LANES), 0)
    cnt_ref[...] = jnp.where(row8 == 0, jnp.sum(sel, axis=0, keepdims=True),
                             jnp.where(row8 == 1, rank[tm // 2:tm // 2 + 1, :], 0.0))


def _route(x, ys, w_out, g, router_pad):
    tm = MOE_TM
    nt = TOKENS // tm
    tok = lambda width: pl.BlockSpec((tm, width), lambda i: (i, 0))
    return pl.pallas_call(
        _route_kernel,
        grid=(nt,),
        in_specs=_mix_specs(tm) + [pl.BlockSpec((1, D_MODEL), lambda i: (0, 0)),
                                   pl.BlockSpec((D_MODEL, LANES), lambda i: (0, 0))],
        out_specs=[tok(D_MODEL), tok(D_MODEL), tok(LANES), tok(LANES), pl.BlockSpec((8, tm), lambda i: (0, i)),
                   pl.BlockSpec((8, LANES), lambda i: (i, 0))],
        out_shape=[jax.ShapeDtypeStruct((TOKENS, D_MODEL), F32), jax.ShapeDtypeStruct((TOKENS, D_MODEL), BF16),
                   jax.ShapeDtypeStruct((TOKENS, LANES), F32), jax.ShapeDtypeStruct((TOKENS, LANES), F32),
                   jax.ShapeDtypeStruct((8, TOKENS), F32), jax.ShapeDtypeStruct((nt * 8, LANES), F32)],
        compiler_params=_params("parallel"),
        name="moe_route",
    )(x, *ys, w_out, g, router_pad)


def _moe_kernel(cnt_ref, x_ref, fg_ref, h_ref, pt_ref, pc_ref, comb_ref, wg_ref, wu_ref, wd_ref,
                acc_ref, xg_ref, yacc_ref):
    tm = h_ref.shape[0]
    sm, oc = MOE_STATIC_ROWS, MOE_CHUNK
    i = pl.program_id(0)
    e = pl.program_id(1)
    f = pl.program_id(2)
    n_over = jnp.maximum(cnt_ref[i * N_EXPERTS + e] - sm + oc - 1, 0) // oc

    def overflow(body):
        def step(c, carry):
            body(pl.multiple_of(sm + c * oc, oc), oc)
            return carry
        lax.fori_loop(0, n_over, step, 0)

    @pl.when((e == 0) & (f == 0))
    def _():
        acc_ref[...] = jnp.zeros(acc_ref.shape, F32)

    @pl.when(f == 0)
    def _():
        pt = pt_ref[pl.ds(e, 1), :]

        def gather(row0, nrows):
            ridx = lax.broadcasted_iota(jnp.int32, (nrows, tm), 0) + row0
            onehot = jnp.where(ridx.astype(F32) == pt, 1.0, 0.0).astype(BF16)
            xg_ref[pl.ds(row0, nrows), :] = jnp.dot(onehot, h_ref[...], preferred_element_type=F32).astype(BF16)
            yacc_ref[pl.ds(row0, nrows), :] = jnp.zeros((nrows, D_MODEL), F32)

        gather(0, sm)
        overflow(gather)

    def ffn(row0, nrows):
        rows = pl.ds(row0, nrows)
        yacc_ref[rows, :] += _swiglu(xg_ref[rows, :], wg_ref[0], wu_ref[0], wd_ref[0])

    ffn(0, sm)
    overflow(ffn)

    @pl.when(f == pl.num_programs(2) - 1)
    def _():
        lane = lax.broadcasted_iota(jnp.int32, (tm, LANES), 1)
        mine = lane == e
        pc = jnp.sum(jnp.where(mine, pc_ref[...], 0.0), axis=-1, keepdims=True)
        gate = jnp.sum(jnp.where(mine, comb_ref[...], 0.0), axis=-1, keepdims=True)

        def scatter(row0, nrows, tok0=0, ntok=tm):
            cidx = lax.broadcasted_iota(jnp.int32, (ntok, nrows), 1) + row0
            onehot = jnp.where(cidx.astype(F32) == pc[tok0:tok0 + ntok], 1.0, 0.0).astype(BF16)
            y = yacc_ref[pl.ds(row0, nrows), :].astype(BF16)
            acc_ref[tok0:tok0 + ntok, :] += jnp.dot(onehot, y, preferred_element_type=F32) * gate[tok0:tok0 + ntok]

        half = tm // 2
        n_half = cnt_ref[pl.num_programs(0) * N_EXPERTS + i * N_EXPERTS + e]
        windowed = (n_half >= MOE_HALF_MIN) & (n_half <= MOE_HALF_MAX)

        @pl.when(windowed)
        def _():
            scatter(0, MOE_HALF_MAX, 0, half)
            scatter(MOE_HALF_MIN, sm - MOE_HALF_MIN, half, half)

        @pl.when(jnp.logical_not(windowed))
        def _():
            scatter(0, sm)

        overflow(scatter)

    @pl.when((e == pl.num_programs(1) - 1) & (f == pl.num_programs(2) - 1))
    def _():
        acc_ref[...] = _rms(x_ref[...] + acc_ref[...], fg_ref[...])


def _moe(x, final_g, h, pt, pc, comb, counts, wg, wu, wd):
    tm, tf = MOE_TM, FFN_DIM // 2
    tok = lambda width: pl.BlockSpec((tm, width), lambda i, e, f, cnt: (i, 0))
    grid_spec = pltpu.PrefetchScalarGridSpec(
        num_scalar_prefetch=1,
        grid=(TOKENS // tm, N_EXPERTS, FFN_DIM // tf),
        in_specs=[tok(D_MODEL), pl.BlockSpec((1, D_MODEL), lambda i, e, f, cnt: (0, 0)), tok(D_MODEL),
                  pl.BlockSpec((8, tm), lambda i, e, f, cnt: (0, i)), tok(LANES), tok(LANES),
                  pl.BlockSpec((1, D_MODEL, tf), lambda i, e, f, cnt: (e, 0, f)),
                  pl.BlockSpec((1, D_MODEL, tf), lambda i, e, f, cnt: (e, 0, f)),
                  pl.BlockSpec((1, tf, D_MODEL), lambda i, e, f, cnt: (e, f, 0))],
        out_specs=tok(D_MODEL),
        scratch_shapes=[pltpu.VMEM((tm, D_MODEL), BF16), pltpu.VMEM((tm, D_MODEL), F32)])
    return pl.pallas_call(
        _moe_kernel,
        grid_spec=grid_spec,
        out_shape=jax.ShapeDtypeStruct((TOKENS, D_MODEL), F32),
        compiler_params=_params("parallel", "arbitrary", "arbitrary"),
        name="moe_ffn",
    )(counts, x, final_g, h, pt, pc, comb, wg, wu, wd)


def kernel(x, positions, attn_norm, w_in, b_forget, mla_q_norm, w_q_up, mla_kv_norm, w_kv_up, sinks, w_out, ffn_norm, dense_w_gate, dense_w_up, dense_w_down, router, moe_w_gate, moe_w_up, moe_w_down, final_norm):
    assert x.shape == (BATCH, SEQ, D_MODEL) and positions.shape == (BATCH, SEQ)
    p_idx, p_sgn = _proj_columns()
    q_idx, q_sgn = _mla_q_columns()
    kv_idx, kv_sgn = _mla_kv_columns()
    mix_rows = _mix_rows()
    cos, sin = _rope_tables(positions)
    xt = x.reshape(TOKENS, D_MODEL).astype(F32)
    assert DEPTH == 2 and moe_w_gate.shape[0] == 1
    to_round = [moe_w_gate[0], moe_w_up[0], moe_w_down[0]]
    moe_w = []
    for layer in range(DEPTH):
        w = _take_columns(w_in[layer].astype(BF16), p_idx, p_sgn)
        wq = _take_columns(w_q_up[layer].astype(BF16), q_idx, q_sgn)
        wkv = _take_columns(w_kv_up[layer].astype(BF16), kv_idx, kv_sgn)
        fq, fk, fv, fl, pb, qc, kc, vc, pd = _project(
            xt, attn_norm[layer].reshape(1, D_MODEL), w, mla_q_norm[layer].reshape(1, MLA_Q_LORA), wq,
            mla_kv_norm[layer].reshape(1, MLA_KV_LORA), wkv, cos, sin)
        y_a, rounded = _flash(fq, _fox_keys(fl, b_forget[layer], fk), fv, "fox_attention",
                              to_round.pop(0) if to_round else None, decay_skip=True)
        moe_w += [rounded] if rounded is not None else []
        y_b = _swa(pb, sinks[layer])
        y_c, rounded = _flash(qc, kc, vc, "mla_attention", to_round.pop(0) if to_round else None)
        moe_w += [rounded] if rounded is not None else []
        y_d = _dilated(pd)
        cuts = [0] + [r for r in range(1, len(mix_rows)) if mix_rows[r] != mix_rows[r - 1] + 1] + [len(mix_rows)]
        w_o16 = w_out[layer].astype(BF16)
        w_o = jnp.concatenate([w_o16[int(mix_rows[a]):int(mix_rows[a]) + (b - a)]
                               for a, b in zip(cuts[:-1], cuts[1:])], axis=0)
        ys = (y_a, y_b, y_c, y_d)
        j = layer // 2
        g = ffn_norm[layer].reshape(1, D_MODEL)
        if layer % 2 == 0:
            xt = _ffn(xt, ys, w_o, g, dense_w_gate, dense_w_up, dense_w_down, j)
        else:
            assert layer == DEPTH - 1
            router_pad = jnp.zeros((D_MODEL, LANES), F32).at[:, :N_EXPERTS].set(router[j])
            xt, h, comb, pc, pt, cnt = _route(xt, ys, w_o, g, router_pad)
            counts = jnp.concatenate([cnt[0::8, :N_EXPERTS].reshape(-1), cnt[1::8, :N_EXPERTS].reshape(-1)])
            counts = counts.astype(jnp.int32)
            assert len(moe_w) == 3
            xt = _moe(xt, final_norm.reshape(1, D_MODEL), h, pt, pc, comb, counts, *moe_w)
    return xt.reshape(BATCH, SEQ, D_MODEL)
```

```python
import functools

import numpy as np
import jax
import jax.numpy as jnp
from jax import lax
from jax.experimental import pallas as pl
from jax.experimental.pallas import tpu as pltpu

D_MODEL = 1024
BATCH = 2
SEQ = 8192
DEPTH = 2
TOKENS = BATCH * SEQ
HEAD_DIM = 64
BAND = 128
NORM_EPS = 1e-6
FOX_HEADS = 4
SWA_Q_HEADS = 4
SWA_KV_HEADS = 2
SWA_WINDOW = 128
MLA_HEADS = 4
MLA_Q_LORA = 256
MLA_KV_LORA = 128
MLA_NOPE_DIM = 64
MLA_ROPE_DIM = 32
MLA_V_DIM = 64
ROPE_THETA = 10000.0
DIL_HEADS = 4
DIL_PATTERNS = ((128, 1), (512, 4), (2048, 16))
DIL_BLOCK = BAND * max(d for _, d in DIL_PATTERNS)
FFN_DIM = 3584
N_EXPERTS = 8
LANES = 128
NEG = -1e30
VMEM_LIMIT = 56 * 1024 * 1024

_OFF = np.cumsum([0, 256, 256, 256, 4, 256, 128, 128, 256, 128, 32, 256, 256, 256])
(_A_Q, _A_K, _A_V, _A_F, _B_Q, _B_K, _B_V, _C_Q, _C_KV, _C_KR, _D_Q, _D_K, _D_V) = _OFF[:13].tolist()
N_PROJ_BLOCKS = 22
LOG2E = 1.4426950408889634

BF16 = jnp.bfloat16
F32 = jnp.float32


def _alibi_slopes():
    n = SWA_Q_HEADS + DIL_HEADS
    return [2.0 ** (-8.0 * i / n) for i in range(1, n + 1)]


def _proj_columns():
    idx = np.zeros((N_PROJ_BLOCKS * LANES,), np.int32)
    sgn = np.zeros((N_PROJ_BLOCKS * LANES,), np.float32)

    def put(dst, src, n, sign=1.0):
        idx[dst:dst + n] = np.arange(src, src + n)
        sgn[dst:dst + n] = sign

    put(0, _A_Q, 256); put(256, _A_K, 256); put(512, _A_V, 256)
    for blk, heads in ((6, (0, 2)), (7, (1, 3))):
        for half, h in enumerate(heads):
            put(blk * LANES + half * HEAD_DIM, _B_Q + h * HEAD_DIM, HEAD_DIM)
    put(8 * LANES, _B_K, 128); put(9 * LANES, _B_V, 128)
    put(10 * LANES, _C_Q, 256); put(12 * LANES, _C_KV, 128)
    half = MLA_ROPE_DIM // 2
    put(13 * LANES + MLA_NOPE_DIM, _C_KR, MLA_ROPE_DIM)
    put(14 * LANES + MLA_NOPE_DIM, _C_KR + half, half, -1.0)
    put(14 * LANES + MLA_NOPE_DIM + half, _C_KR, half)
    put(15 * LANES, _D_Q, 256); put(17 * LANES, _D_K, 256); put(19 * LANES, _D_V, 256)
    put(21 * LANES, _A_F, FOX_HEADS)
    return idx, sgn


def _bias_lane0(h):
    return HEAD_DIM if h % 2 == 0 else 0


def _bias_placement():
    place = np.zeros((3 * LANES, FOX_HEADS * LANES), np.float32)
    for piece in range(3):
        for h in range(FOX_HEADS):
            place[piece * LANES + h, h * LANES + _bias_lane0(h) + piece] = 1.0
    return place


def _mla_q_columns():
    idx = np.zeros((8 * LANES,), np.int32)
    sgn = np.zeros((8 * LANES,), np.float32)
    half = MLA_ROPE_DIM // 2
    dq = MLA_NOPE_DIM + MLA_ROPE_DIM
    for h in range(MLA_HEADS):
        a = h * LANES
        idx[a:a + dq] = np.arange(h * dq, (h + 1) * dq); sgn[a:a + dq] = 1.0
        b = (MLA_HEADS + h) * LANES + MLA_NOPE_DIM
        r = h * dq + MLA_NOPE_DIM
        idx[b:b + half] = np.arange(r + half, r + 2 * half); sgn[b:b + half] = -1.0
        idx[b + half:b + 2 * half] = np.arange(r, r + half); sgn[b + half:b + 2 * half] = 1.0
    return idx, sgn


def _mla_kv_columns():
    idx = np.zeros((6 * LANES,), np.int32)
    sgn = np.zeros((6 * LANES,), np.float32)
    dkv = MLA_NOPE_DIM + MLA_V_DIM
    for h in range(MLA_HEADS):
        idx[h * LANES:h * LANES + MLA_NOPE_DIM] = np.arange(h * dkv, h * dkv + MLA_NOPE_DIM)
        sgn[h * LANES:h * LANES + MLA_NOPE_DIM] = 1.0
        b = MLA_HEADS * LANES + h * MLA_V_DIM
        idx[b:b + MLA_V_DIM] = np.arange(h * dkv + MLA_NOPE_DIM, (h + 1) * dkv)
        sgn[b:b + MLA_V_DIM] = 1.0
    return idx, sgn


def _take_columns(w, idx, sgn):
    parts = []
    a = 0
    while a < len(idx):
        b = a + 1
        while b < len(idx) and sgn[b] == sgn[a] and (sgn[a] == 0 or idx[b] == idx[b - 1] + 1):
            b += 1
        if sgn[a] == 0:
            parts.append(jnp.zeros((w.shape[0], b - a), w.dtype))
        else:
            piece = w[:, int(idx[a]):int(idx[a]) + (b - a)]
            parts.append(piece if sgn[a] > 0 else -piece)
        a = b
    return jnp.concatenate(parts, axis=1)


def _mix_rows():
    rows = np.arange(4 * 256)
    b = 256
    perm = np.concatenate([np.arange(b + h * HEAD_DIM, b + (h + 1) * HEAD_DIM) for h in (0, 2, 1, 3)])
    rows[b:b + 256] = perm
    return rows


def _rms(x, g):
    return x * lax.rsqrt(jnp.mean(x * x, axis=-1, keepdims=True) + NORM_EPS) * g


def _dot_nt(a, b):
    return lax.dot_general(a, b, (((1,), (1,)), ((), ())), preferred_element_type=F32)


def _lane_tile(x, width):
    return x if width == LANES else jnp.concatenate([x] * (width // LANES), axis=1)


def _params(*sem):
    return pltpu.CompilerParams(dimension_semantics=sem, vmem_limit_bytes=VMEM_LIMIT)


def _rope_table_kernel(pos_ref, invf_ref, cos_ref, sin_ref):
    ang = pos_ref[...].astype(F32) * invf_ref[...]
    cos_ref[...] = jnp.cos(ang)
    sin_ref[...] = jnp.sin(ang)


def _rope_tables(positions):
    tm = 2048
    half = MLA_ROPE_DIM // 2
    invf = np.zeros((1, LANES), np.float32)
    f = (ROPE_THETA ** (-np.arange(half, dtype=np.float32) / np.float32(half))).astype(np.float32)
    invf[0, MLA_NOPE_DIM:MLA_NOPE_DIM + half] = f
    invf[0, MLA_NOPE_DIM + half:MLA_NOPE_DIM + 2 * half] = f
    return pl.pallas_call(
        _rope_table_kernel,
        grid=(TOKENS // tm,),
        in_specs=[pl.BlockSpec((tm, 1), lambda i: (i, 0)), pl.BlockSpec((1, LANES), lambda i: (0, 0))],
        out_specs=[pl.BlockSpec((tm, LANES), lambda i: (i, 0))] * 2,
        out_shape=[jax.ShapeDtypeStruct((TOKENS, LANES), F32)] * 2,
        compiler_params=_params("parallel"),
        name="rope_tables",
    )(positions.reshape(TOKENS, 1), jnp.asarray(invf))


def _proj_kernel(x_ref, g_ref, w_ref, qn_ref, wq_ref, kvn_ref, wkv_ref, cos_ref, sin_ref,
                 fq_ref, fk_ref, fv_ref, fl_ref, pb_ref, qc_ref, kc_ref, vc_ref, pd_ref):
    hb = _rms(x_ref[...], g_ref[...]).astype(BF16)
    res = jnp.dot(hb, w_ref[...], preferred_element_type=F32)
    lane = lax.broadcasted_iota(jnp.int32, (x_ref.shape[0], LANES), 1)
    low = lane < HEAD_DIM

    def blk(j, n=1):
        return res[:, j * LANES:(j + n) * LANES]

    def own(h):
        return low if h % 2 == 0 else ~low

    qscale = HEAD_DIM ** -0.5 * LOG2E
    for h in range(FOX_HEADS):
        ones = (lane >= _bias_lane0(h)) & (lane < _bias_lane0(h) + 3)
        fq_ref[h] = jnp.where(own(h), blk(h // 2) * qscale, jnp.where(ones, 1.0, 0.0)).astype(BF16)
        fv_ref[h] = jnp.where(own(h), blk(4 + h // 2), 1.0).astype(BF16)
    for j in range(2):
        fk_ref[j] = blk(2 + j).astype(BF16)
    fl_ref[...] = blk(21)
    for j in range(2):
        pb_ref[j] = (blk(6 + j) * qscale).astype(BF16)
    pb_ref[2] = blk(8).astype(BF16)
    pb_ref[3] = blk(9).astype(BF16)

    cos = cos_ref[...]
    sin = sin_ref[...]
    cq = _rms(blk(10, 2), qn_ref[...]).astype(BF16)
    qab = jnp.dot(cq, wq_ref[...], preferred_element_type=F32)
    mla_scale = (MLA_NOPE_DIM + MLA_ROPE_DIM) ** -0.5
    for h in range(MLA_HEADS):
        qa = qab[:, h * LANES:(h + 1) * LANES]
        qb = qab[:, (MLA_HEADS + h) * LANES:(MLA_HEADS + h + 1) * LANES]
        qc_ref[h] = ((qa * cos + qb * sin) * (mla_scale * LOG2E)).astype(BF16)
    ckv = _rms(blk(12), kvn_ref[...]).astype(BF16)
    kv = jnp.dot(ckv, wkv_ref[...], preferred_element_type=F32)
    k_rot = blk(13) * cos + blk(14) * sin
    for h in range(MLA_HEADS):
        kc_ref[h] = (kv[:, h * LANES:(h + 1) * LANES] + k_rot).astype(BF16)
        v_pair = kv[:, (MLA_HEADS + h // 2) * LANES:(MLA_HEADS + h // 2 + 1) * LANES]
        vc_ref[h] = jnp.where(own(h), v_pair, 1.0).astype(BF16)

    for j in range(2):
        pd_ref[j] = blk(15 + j) * qscale
    for j in range(2, 6):
        pd_ref[j] = blk(15 + j)


def _project(x, g, w, qn, wq, kvn, wkv, cos, sin):
    tm = 512
    full = lambda shape: pl.BlockSpec(shape, lambda i: (0,) * len(shape))
    out_blk = lambda n: pl.BlockSpec((n, tm, LANES), lambda i: (0, i, 0))
    out_sds = lambda n, dt: jax.ShapeDtypeStruct((n, TOKENS, LANES), dt)
    tok_blk = pl.BlockSpec((tm, LANES), lambda i: (i, 0))
    return pl.pallas_call(
        _proj_kernel,
        grid=(TOKENS // tm,),
        in_specs=[pl.BlockSpec((tm, D_MODEL), lambda i: (i, 0)), full((1, D_MODEL)),
                  full((D_MODEL, N_PROJ_BLOCKS * LANES)),
                  full((1, MLA_Q_LORA)), full((MLA_Q_LORA, 8 * LANES)),
                  full((1, MLA_KV_LORA)), full((MLA_KV_LORA, 6 * LANES)), tok_blk, tok_blk],
        out_specs=[out_blk(4), out_blk(2), out_blk(4), tok_blk, out_blk(4), out_blk(4), out_blk(4), out_blk(4),
                   out_blk(6)],
        out_shape=[out_sds(4, BF16), out_sds(2, BF16), out_sds(4, BF16), jax.ShapeDtypeStruct((TOKENS, LANES), F32),
                   out_sds(4, BF16), out_sds(4, BF16), out_sds(4, BF16), out_sds(4, BF16), out_sds(6, F32)],
        compiler_params=_params("parallel"),
        name="in_proj",
    )(x, g, w, qn, wq, kvn, wkv, cos, sin)


def _split3(x):
    hi = x.astype(BF16)
    r1 = x - hi.astype(F32)
    mid = r1.astype(BF16)
    return hi, mid, (r1 - mid.astype(F32)).astype(BF16)


def _fox_keys_kernel(fl_ref, b_ref, k_ref, place_ref, kf_ref, carry_ref, *, cs):
    @pl.when(pl.program_id(1) == 0)
    def _():
        carry_ref[...] = jnp.zeros_like(carry_ref)

    lf = jax.nn.log_sigmoid(fl_ref[...] + b_ref[...])
    row = lax.broadcasted_iota(jnp.int32, (cs, cs), 0)
    col = lax.broadcasted_iota(jnp.int32, (cs, cs), 1)
    tri = jnp.where(col <= row, 1.0, 0.0).astype(BF16)
    cum = carry_ref[0:1, :] + sum(jnp.dot(tri, piece, preferred_element_type=F32) for piece in _split3(lf))
    carry_ref[0:1, :] = cum[cs - 1:cs, :]
    pieces = jnp.concatenate(_split3(cum * (-LOG2E)), axis=1)
    placed = jnp.dot(pieces, place_ref[...], preferred_element_type=F32).astype(BF16)
    low = lax.broadcasted_iota(jnp.int32, (cs, LANES), 1) < HEAD_DIM
    for h in range(FOX_HEADS):
        kf_ref[h] = jnp.where(low if h % 2 == 0 else ~low, k_ref[h // 2], placed[:, h * LANES:(h + 1) * LANES])


def _fox_keys(fl, b_forget, fk):
    cs = 512
    nc = SEQ // cs
    b_row = jnp.zeros((1, LANES), F32).at[0, :FOX_HEADS].set(b_forget.astype(F32))
    return pl.pallas_call(
        functools.partial(_fox_keys_kernel, cs=cs),
        grid=(BATCH, nc),
        in_specs=[pl.BlockSpec((cs, LANES), lambda b, i: (b * nc + i, 0)),
                  pl.BlockSpec((1, LANES), lambda b, i: (0, 0)),
                  pl.BlockSpec((2, cs, LANES), lambda b, i: (0, b * nc + i, 0)),
                  pl.BlockSpec((3 * LANES, FOX_HEADS * LANES), lambda b, i: (0, 0))],
        out_specs=pl.BlockSpec((FOX_HEADS, cs, LANES), lambda b, i: (0, b * nc + i, 0)),
        out_shape=jax.ShapeDtypeStruct((FOX_HEADS, TOKENS, LANES), BF16),
        scratch_shapes=[pltpu.VMEM((8, LANES), F32)],
        compiler_params=_params("parallel", "arbitrary"),
        name="fox_keys",
    )(fl, b_row, fk, jnp.asarray(_bias_placement(), BF16))


FLASH_UNDERFLOW = -160.0


def _flash_kernel(*refs, tq, tk, n_cast, decay_skip):
    q_ref, k_ref, v_ref = refs[:3]
    o_ref = refs[3 + n_cast]
    acc_ref, m_ref, kmax_ref = refs[-3:]
    for src_ref, dst_ref in zip(refs[3:3 + n_cast], refs[4 + n_cast:4 + 2 * n_cast]):
        dst_ref[...] = src_ref[...].astype(dst_ref.dtype)
    i = pl.program_id(2)
    m_ref[...] = jnp.full(m_ref.shape, NEG, F32)
    acc_ref[...] = jnp.zeros(acc_ref.shape, F32)

    def sq_norms(x, h):
        lane_in = lax.broadcasted_iota(jnp.int32, (LANES, LANES), 0)
        ones = jnp.where((lane_in < HEAD_DIM) == (h == 0), 1.0, 0.0).astype(BF16)
        xf = x.astype(F32)
        return jnp.dot((xf * xf).astype(BF16), ones, preferred_element_type=F32)

    if decay_skip:
        @pl.when(i == 0)
        def _():
            for h in range(2):
                kmax_ref[h] = jnp.broadcast_to(jnp.max(sq_norms(k_ref[h], h), axis=0, keepdims=True), (8, LANES))

    def step(kb, row0, diagonal):
        ks = pl.multiple_of(kb * tk, tk)
        rows = pl.ds(row0, tq - row0)
        for h in range(2):
            s = _dot_nt(q_ref[h, rows, :], k_ref[h, pl.ds(ks, tk), :])
            if diagonal:
                r = lax.broadcasted_iota(jnp.int32, s.shape, 0)
                c = lax.broadcasted_iota(jnp.int32, s.shape, 1)
                s = jnp.where(c <= r, s, NEG)
            m_old = m_ref[h, rows, :]
            m_new = jnp.maximum(m_old, jnp.max(s, axis=-1, keepdims=True))
            p = jnp.exp2(s - _lane_tile(m_new, tk)).astype(BF16)
            acc_ref[h, rows, :] = (jnp.exp2(m_old - m_new) * acc_ref[h, rows, :]
                                   + jnp.dot(p, v_ref[h, pl.ds(ks, tk), :], preferred_element_type=F32))
            m_ref[h, rows, :] = m_new

    per_q = tq // tk
    if not decay_skip:
        def body(kb, carry):
            step(kb, 0, False)
            return carry

        lax.fori_loop(0, i * per_q, body, 0)
        for u in range(per_q):
            step(i * per_q + u, u * tk, True)
    else:
        for u in range(per_q):
            step(i * per_q + u, u * tk, True)
        slack = []
        for h in range(2):
            qk_bound = jnp.sqrt(sq_norms(q_ref[h], h) * kmax_ref[h, 0:1, :]) * 1.02
            slack.append(jnp.max(qk_bound - m_ref[h]))
        sub = lax.broadcasted_iota(jnp.int32, (16, LANES), 0)
        lane = lax.broadcasted_iota(jnp.int32, (16, LANES), 1)

        def last_bias(kb, h):
            rows = k_ref[h, pl.ds(pl.multiple_of((kb + 1) * tk - 16, 16), 16), :].astype(F32)
            pick = (sub == 15) & (lane >= _bias_lane0(h)) & (lane < _bias_lane0(h) + 3)
            return jnp.sum(jnp.where(pick, rows, 0.0))

        def visible(kb):
            kb0 = jnp.maximum(kb, 0)
            return (kb >= 0) & ((slack[0] + last_bias(kb0, 0) > FLASH_UNDERFLOW)
                                | (slack[1] + last_bias(kb0, 1) > FLASH_UNDERFLOW))

        def body(state):
            kb, _ = state
            step(kb, 0, False)
            return kb - 1, visible(kb - 1)

        lax.while_loop(lambda state: state[1], body, (i * per_q - 1, visible(i * per_q - 1)))
    outs = [acc_ref[h] / pltpu.roll(acc_ref[h], HEAD_DIM, axis=1) for h in range(2)]
    low = lax.broadcasted_iota(jnp.int32, (tq, LANES), 1) < HEAD_DIM
    o_ref[0] = jnp.where(low, outs[0], outs[1]).astype(o_ref.dtype)


def _flash(q, k, v, name, w_f32=None, decay_skip=False):
    tq, tk = 2048, 512
    nq = SEQ // tq
    w_args, w_specs, w_shapes = [], [], []
    if w_f32 is not None:
        per_expert = BATCH * 2 * nq // N_EXPERTS
        assert per_expert * N_EXPERTS == BATCH * 2 * nq and w_f32.shape[1] % (16 * per_expert) == 0
        slab = lambda b, j, i: divmod((b * 2 + j) * nq + i, per_expert) + (0,)
        w_args = [w_f32]
        w_specs = [pl.BlockSpec((1, w_f32.shape[1] // per_expert, w_f32.shape[2]), slab)]
        w_shapes = [jax.ShapeDtypeStruct(w_f32.shape, BF16)]
    kv_spec = pl.BlockSpec((2, SEQ, LANES), lambda b, j, i: (j, b, 0))
    outs = pl.pallas_call(
        functools.partial(_flash_kernel, tq=tq, tk=tk, n_cast=len(w_args), decay_skip=decay_skip),
        grid=(BATCH, 2, nq),
        in_specs=[pl.BlockSpec((2, tq, LANES), lambda b, j, i: (j, b * nq + i, 0)), kv_spec, kv_spec] + w_specs,
        out_specs=[pl.BlockSpec((1, tq, LANES), lambda b, j, i: (j, b * nq + i, 0))] + w_specs,
        out_shape=[jax.ShapeDtypeStruct((2, TOKENS, LANES), BF16)] + w_shapes,
        scratch_shapes=[pltpu.VMEM((2, tq, LANES), F32)] * 2 + [pltpu.VMEM((2, 8, LANES), F32)],
        compiler_params=_params("parallel", "parallel", "arbitrary"),
        name=name,
    )(q, k, v, *w_args)
    return outs[0], (outs[1] if w_args else None)


def _band_mask_bias(span, slope_step, first):
    qi = lax.broadcasted_iota(jnp.int32, (BAND, 2 * BAND), 0)
    kj = lax.broadcasted_iota(jnp.int32, (BAND, 2 * BAND), 1)
    dist = BAND + qi - kj
    ok = (dist >= 0) & (dist <= span) & (kj >= jnp.where(first, BAND, 0))
    return jnp.where(ok, dist.astype(F32) * (-slope_step), NEG)


def _band_pair(q, kk, vv, mask_bias):
    low = lax.broadcasted_iota(jnp.int32, (BAND, LANES), 1) < HEAD_DIM
    low_kv = lax.broadcasted_iota(jnp.int32, (2 * BAND, LANES), 1) < HEAD_DIM
    accs, ms = [], []
    for half in range(2):
        qm = jnp.where(low if half == 0 else ~low, q, jnp.zeros_like(q))
        vh = jnp.where(low_kv if half == 0 else ~low_kv, vv, jnp.ones_like(vv))
        s = _dot_nt(qm, kk) + mask_bias[half]
        m = jnp.max(s, axis=-1, keepdims=True)
        accs.append(jnp.dot(jnp.exp2(s - m).astype(BF16), vh, preferred_element_type=F32))
        ms.append(m)
    l = pltpu.roll(jnp.where(low, accs[1], accs[0]), HEAD_DIM, axis=1)
    return jnp.where(low, accs[0], accs[1]) / l, jnp.where(low, ms[0], ms[1]) + jnp.log2(l)


def _swa_kernel(q_ref, k_ref, kp_ref, v_ref, vp_ref, sink_ref, o_ref, kk_ref, vv_ref, *, tb, slopes):
    n = pl.program_id(1)
    kk_ref[0:BAND] = kp_ref[0]
    kk_ref[BAND:] = k_ref[0]
    vv_ref[0:BAND] = vp_ref[0]
    vv_ref[BAND:] = v_ref[0]
    mask_bias = [[_band_mask_bias(SWA_WINDOW - 1, slope * LOG2E, first) for slope in slopes]
                 for first in (n == 0, False)]
    for c in range(tb // BAND):
        kk = kk_ref[c * BAND:(c + 2) * BAND]
        vv = vv_ref[c * BAND:(c + 2) * BAND]
        mb = mask_bias[0 if c == 0 else 1]
        for jb in range(2):
            o, lse2 = _band_pair(q_ref[jb, c * BAND:(c + 1) * BAND, :], kk, vv, (mb[jb], mb[jb + 2]))
            o = o / (1.0 + jnp.exp2(sink_ref[jb:jb + 1, :] - lse2))
            o_ref[jb, c * BAND:(c + 1) * BAND, :] = o.astype(o_ref.dtype)


def _swa(pb, sinks):
    tb = 512
    nb = SEQ // tb
    r = tb // BAND
    s = sinks.astype(F32) * LOG2E
    sink_lanes = jnp.stack([jnp.concatenate([jnp.full((HEAD_DIM,), s[jb]), jnp.full((HEAD_DIM,), s[jb + 2])])
                            for jb in range(2)])
    cur = lambda blk: pl.BlockSpec((1, tb, LANES), lambda b, n: (blk, b * nb + n, 0))
    prev = lambda blk: pl.BlockSpec((1, BAND, LANES), lambda b, n: (blk, jnp.maximum((b * nb + n) * r - 1, 0), 0))
    return pl.pallas_call(
        functools.partial(_swa_kernel, tb=tb, slopes=_alibi_slopes()[:SWA_Q_HEADS]),
        grid=(BATCH, nb),
        in_specs=[pl.BlockSpec((2, tb, LANES), lambda b, n: (0, b * nb + n, 0)), cur(2), prev(2), cur(3), prev(3),
                  pl.BlockSpec((2, LANES), lambda b, n: (0, 0))],
        out_specs=pl.BlockSpec((2, tb, LANES), lambda b, n: (0, b * nb + n, 0)),
        out_shape=jax.ShapeDtypeStruct((2, TOKENS, LANES), BF16),
        scratch_shapes=[pltpu.VMEM((tb + BAND, LANES), BF16)] * 2,
        compiler_params=_params("parallel", "parallel"),
        name="swa_attention",
    )(pb, pb, pb, pb, pb, sink_lanes)


def _dil_kernel(q_ref, k_ref, kp_ref, v_ref, vp_ref, o_ref, kk_ref, vv_ref, po_ref, pl_ref, *, slopes):
    pair = pl.program_id(1)
    n = pl.program_id(2)
    kk_ref[0:DIL_BLOCK] = kp_ref[0]
    kk_ref[DIL_BLOCK:] = k_ref[0]
    vv_ref[0:DIL_BLOCK] = vp_ref[0]
    vv_ref[DIL_BLOCK:] = v_ref[0]
    units = DIL_BLOCK // BAND
    for p, (window, dil) in enumerate(DIL_PATTERNS):
        steps = [jnp.where(pair == 0, slopes[half], slopes[2 + half]) * (dil * LOG2E) for half in range(2)]
        mask_bias = [[_band_mask_bias(window // dil, step, first) for step in steps] for first in (n == 0, False)]

        def unit(u, carry, p=p, dil=dil, mb=None):
            start = (u // dil) * (BAND * dil) + u % dil
            if dil == 1:
                rows = pl.ds(pl.multiple_of(start, BAND), BAND)
                krows = pl.ds(pl.multiple_of(DIL_BLOCK + start - BAND, BAND), 2 * BAND)
            else:
                rows = pl.ds(start, BAND, stride=dil)
                krows = pl.ds(DIL_BLOCK + start - BAND * dil, 2 * BAND, stride=dil)
            o, lse2 = _band_pair(q_ref[0, rows, :].astype(BF16), kk_ref[krows, :].astype(BF16),
                                 vv_ref[krows, :].astype(BF16), mb)
            po_ref[p, rows, :] = o
            pl_ref[p, rows, :] = lse2
            return carry

        lax.fori_loop(0, dil, functools.partial(unit, mb=mask_bias[0]), 0, unroll=min(dil, 8))
        if dil < units:
            lax.fori_loop(dil, units, functools.partial(unit, mb=mask_bias[1]), 0, unroll=min(units - dil, 8))
    chunk = 256

    def merge(t, carry):
        rows = pl.ds(pl.multiple_of(t * chunk, chunk), chunk)
        lse = pl_ref[:, rows, :]
        w = jnp.exp2(lse - jnp.max(lse, axis=0, keepdims=True))
        o_ref[0, rows, :] = (jnp.sum(w * po_ref[:, rows, :], axis=0) / jnp.sum(w, axis=0)).astype(o_ref.dtype)
        return carry

    lax.fori_loop(0, DIL_BLOCK // chunk, merge, 0)


def _dilated(pd):
    nb = SEQ // DIL_BLOCK
    cur = lambda off: pl.BlockSpec((1, DIL_BLOCK, LANES), lambda b, j, n: (off + j, b * nb + n, 0))
    prev = lambda off: pl.BlockSpec((1, DIL_BLOCK, LANES),
                                    lambda b, j, n: (off + j, b * nb + jnp.maximum(n - 1, 0), 0))
    return pl.pallas_call(
        functools.partial(_dil_kernel, slopes=_alibi_slopes()[SWA_Q_HEADS:]),
        grid=(BATCH, 2, nb),
        in_specs=[cur(0), cur(2), prev(2), cur(4), prev(4)],
        out_specs=pl.BlockSpec((1, DIL_BLOCK, LANES), lambda b, j, n: (j, b * nb + n, 0)),
        out_shape=jax.ShapeDtypeStruct((2, TOKENS, LANES), BF16),
        scratch_shapes=[pltpu.VMEM((2 * DIL_BLOCK, LANES), F32)] * 2
                       + [pltpu.VMEM((len(DIL_PATTERNS), DIL_BLOCK, LANES), F32)] * 2,
        compiler_params=_params("parallel", "parallel", "parallel"),
        name="dilated_attention",
    )(pd, pd, pd, pd, pd)


def _mix_residual(x_ref, y_refs, w_ref):
    mixed = jnp.concatenate([y[j] for y in y_refs for j in range(2)], axis=1)
    return x_ref[...] + jnp.dot(mixed, w_ref[...], preferred_element_type=F32)


def _mix_specs(tm):
    return ([pl.BlockSpec((tm, D_MODEL), lambda i, *_: (i, 0))]
            + [pl.BlockSpec((2, tm, LANES), lambda i, *_: (0, i, 0))] * 4
            + [pl.BlockSpec((D_MODEL, D_MODEL), lambda i, *_: (0, 0))])


def _swiglu(h, wg, wu, wd):
    gate = jnp.dot(h, wg, preferred_element_type=F32)
    up = jnp.dot(h, wu, preferred_element_type=F32)
    act = (gate * jax.nn.sigmoid(gate) * up).astype(BF16)
    return jnp.dot(act, wd, preferred_element_type=F32)


def _ffn_kernel(x_ref, ya_ref, yb_ref, yc_ref, yd_ref, wo_ref, g_ref, wg_ref, wu_ref, wd_ref, o_ref, h_ref, acc_ref):
    f = pl.program_id(1)

    @pl.when(f == 0)
    def _():
        x = _mix_residual(x_ref, (ya_ref, yb_ref, yc_ref, yd_ref), wo_ref)
        h_ref[...] = _rms(x, g_ref[...]).astype(BF16)
        acc_ref[...] = x

    acc_ref[...] += _swiglu(h_ref[...], wg_ref[...].astype(BF16), wu_ref[...].astype(BF16), wd_ref[...].astype(BF16))

    @pl.when(f == pl.num_programs(1) - 1)
    def _():
        o_ref[...] = acc_ref[...]


def _ffn(x, ys, w_out, g, wg, wu, wd, j):
    tm, tf = 1024, 512
    return pl.pallas_call(
        _ffn_kernel,
        grid=(TOKENS // tm, FFN_DIM // tf),
        in_specs=_mix_specs(tm) + [
                  pl.BlockSpec((1, D_MODEL), lambda i, f: (0, 0)),
                  pl.BlockSpec((None, D_MODEL, tf), lambda i, f: (j, 0, f)),
                  pl.BlockSpec((None, D_MODEL, tf), lambda i, f: (j, 0, f)),
                  pl.BlockSpec((None, tf, D_MODEL), lambda i, f: (j, f, 0))],
        out_specs=pl.BlockSpec((tm, D_MODEL), lambda i, f: (i, 0)),
        out_shape=jax.ShapeDtypeStruct((TOKENS, D_MODEL), F32),
        scratch_shapes=[pltpu.VMEM((tm, D_MODEL), BF16), pltpu.VMEM((tm, D_MODEL), F32)],
        compiler_params=_params("parallel", "arbitrary"),
        name="dense_ffn",
    )(x, *ys, w_out, g, wg, wu, wd)


MOE_TM = 1024
MOE_STATIC_ROWS = 288
MOE_CHUNK = 32
MOE_HALF_MIN = 96
MOE_HALF_MAX = 160


def _route_kernel(x_ref, ya_ref, yb_ref, yc_ref, yd_ref, wo_ref, g_ref, router_ref,
                  x1_ref, h_ref, comb_ref, pc_ref, pt_ref, cnt_ref):
    tm = x_ref.shape[0]
    x1 = _mix_residual(x_ref, (ya_ref, yb_ref, yc_ref, yd_ref), wo_ref)
    x1_ref[...] = x1
    h = _rms(x1, g_ref[...])
    h_ref[...] = h.astype(BF16)
    lane = lax.broadcasted_iota(jnp.int32, (tm, LANES), 1).astype(F32)
    h_hi, h_lo, _ = _split3(h)
    r_hi, r_lo, _ = _split3(router_ref[...])
    logits = (jnp.dot(h_hi, r_hi, preferred_element_type=F32) + jnp.dot(h_hi, r_lo, preferred_element_type=F32)
              + jnp.dot(h_lo, r_hi, preferred_element_type=F32))
    logits = jnp.where(lane < N_EXPERTS, logits, NEG)
    m1 = jnp.max(logits, axis=-1, keepdims=True)
    i1 = jnp.min(jnp.where(logits == m1, lane, float(LANES)), axis=-1, keepdims=True)
    rest = jnp.where(lane == i1, NEG, logits)
    m2 = jnp.max(rest, axis=-1, keepdims=True)
    i2 = jnp.min(jnp.where(rest == m2, lane, float(LANES)), axis=-1, keepdims=True)
    t = jnp.exp(m2 - m1)
    comb_ref[...] = jnp.where(lane == i1, 1.0 / (1.0 + t), 0.0) + jnp.where(lane == i2, t / (1.0 + t), 0.0)
    sel = jnp.where(lane == i1, 1.0, jnp.where(lane == i2, 1.0, 0.0))
    selb = sel.astype(BF16)
    r = lax.broadcasted_iota(jnp.int32, (tm, tm), 0)
    c = lax.broadcasted_iota(jnp.int32, (tm, tm), 1)
    rank = jnp.dot(jnp.where(c < r, 1.0, 0.0).astype(BF16), selb, preferred_element_type=F32)
    pc_ref[...] = jnp.where(sel > 0.0, rank, -1.0)
    eye = jnp.where(lax.broadcasted_iota(jnp.int32, (8, LANES), 0) == lax.broadcasted_iota(jnp.int32, (8, LANES), 1),
                    1.0, 0.0).astype(BF16)
    sel_t = _dot_nt(eye, selb)
    rank_t = jnp.dot(sel_t.astype(BF16), jnp.where(r < c, 1.0, 0.0).astype(BF16), preferred_element_type=F32)
    pt_ref[...] = jnp.where(sel_t > 0.0, rank_t, -1.0)
    row8 = lax.broadcasted_iota(jnp.int32, (8, LANES), 0)
    cnt_ref[...] = jnp.where(row8 == 0, jnp.sum(sel, axis=0, keepdims=True),
                             jnp.where(row8 == 1, rank[tm // 2:tm // 2 + 1, :], 0.0))


def _route(x, ys, w_out, g, router_pad):
    tm = MOE_TM
    nt = TOKENS // tm
    tok = lambda width: pl.BlockSpec((tm, width), lambda i: (i, 0))
    return pl.pallas_call(
        _route_kernel,
        grid=(nt,),
        in_specs=_mix_specs(tm) + [pl.BlockSpec((1, D_MODEL), lambda i: (0, 0)),
                                   pl.BlockSpec((D_MODEL, LANES), lambda i: (0, 0))],
        out_specs=[tok(D_MODEL), tok(D_MODEL), tok(LANES), tok(LANES), pl.BlockSpec((8, tm), lambda i: (0, i)),
                   pl.BlockSpec((8, LANES), lambda i: (i, 0))],
        out_shape=[jax.ShapeDtypeStruct((TOKENS, D_MODEL), F32), jax.ShapeDtypeStruct((TOKENS, D_MODEL), BF16),
                   jax.ShapeDtypeStruct((TOKENS, LANES), F32), jax.ShapeDtypeStruct((TOKENS, LANES), F32),
                   jax.ShapeDtypeStruct((8, TOKENS), F32), jax.ShapeDtypeStruct((nt * 8, LANES), F32)],
        compiler_params=_params("parallel"),
        name="moe_route",
    )(x, *ys, w_out, g, router_pad)


def _moe_kernel(cnt_ref, x_ref, fg_ref, h_ref, pt_ref, pc_ref, comb_ref, wg_ref, wu_ref, wd_ref,
                acc_ref, xg_ref, yacc_ref):
    tm = h_ref.shape[0]
    sm, oc = MOE_STATIC_ROWS, MOE_CHUNK
    i = pl.program_id(0)
    e = pl.program_id(1)
    f = pl.program_id(2)
    n_over = jnp.maximum(cnt_ref[i * N_EXPERTS + e] - sm + oc - 1, 0) // oc

    def overflow(body):
        def step(c, carry):
            body(pl.multiple_of(sm + c * oc, oc), oc)
            return carry
        lax.fori_loop(0, n_over, step, 0)

    @pl.when((e == 0) & (f == 0))
    def _():
        acc_ref[...] = jnp.zeros(acc_ref.shape, F32)

    @pl.when(f == 0)
    def _():
        pt = pt_ref[pl.ds(e, 1), :]

        def gather(row0, nrows):
            ridx = lax.broadcasted_iota(jnp.int32, (nrows, tm), 0) + row0
            onehot = jnp.where(ridx.astype(F32) == pt, 1.0, 0.0).astype(BF16)
            xg_ref[pl.ds(row0, nrows), :] = jnp.dot(onehot, h_ref[...], preferred_element_type=F32).astype(BF16)
            yacc_ref[pl.ds(row0, nrows), :] = jnp.zeros((nrows, D_MODEL), F32)

        gather(0, sm)
        overflow(gather)

    def ffn(row0, nrows):
        rows = pl.ds(row0, nrows)
        yacc_ref[rows, :] += _swiglu(xg_ref[rows, :], wg_ref[0], wu_ref[0], wd_ref[0])

    ffn(0, sm)
    overflow(ffn)

    @pl.when(f == pl.num_programs(2) - 1)
    def _():
        lane = lax.broadcasted_iota(jnp.int32, (tm, LANES), 1)
        mine = lane == e
        pc = jnp.sum(jnp.where(mine, pc_ref[...], 0.0), axis=-1, keepdims=True)
        gate = jnp.sum(jnp.where(mine, comb_ref[...], 0.0), axis=-1, keepdims=True)

        def scatter(row0, nrows, tok0=0, ntok=tm):
            cidx = lax.broadcasted_iota(jnp.int32, (ntok, nrows), 1) + row0
            onehot = jnp.where(cidx.astype(F32) == pc[tok0:tok0 + ntok], 1.0, 0.0).astype(BF16)
            y = yacc_ref[pl.ds(row0, nrows), :].astype(BF16)
            acc_ref[tok0:tok0 + ntok, :] += jnp.dot(onehot, y, preferred_element_type=F32) * gate[tok0:tok0 + ntok]

        half = tm // 2
        n_half = cnt_ref[pl.num_programs(0) * N_EXPERTS + i * N_EXPERTS + e]
        windowed = (n_half >= MOE_HALF_MIN) & (n_half <= MOE_HALF_MAX)

        @pl.when(windowed)
        def _():
            scatter(0, MOE_HALF_MAX, 0, half)
            scatter(MOE_HALF_MIN, sm - MOE_HALF_MIN, half, half)

        @pl.when(jnp.logical_not(windowed))
        def _():
            scatter(0, sm)

        overflow(scatter)

    @pl.when((e == pl.num_programs(1) - 1) & (f == pl.num_programs(2) - 1))
    def _():
        acc_ref[...] = _rms(x_ref[...] + acc_ref[...], fg_ref[...])


def _moe(x, final_g, h, pt, pc, comb, counts, wg, wu, wd):
    tm, tf = MOE_TM, FFN_DIM // 2
    tok = lambda width: pl.BlockSpec((tm, width), lambda i, e, f, cnt: (i, 0))
    grid_spec = pltpu.PrefetchScalarGridSpec(
        num_scalar_prefetch=1,
        grid=(TOKENS // tm, N_EXPERTS, FFN_DIM // tf),
        in_specs=[tok(D_MODEL), pl.BlockSpec((1, D_MODEL), lambda i, e, f, cnt: (0, 0)), tok(D_MODEL),
                  pl.BlockSpec((8, tm), lambda i, e, f, cnt: (0, i)), tok(LANES), tok(LANES),
                  pl.BlockSpec((1, D_MODEL, tf), lambda i, e, f, cnt: (e, 0, f)),
                  pl.BlockSpec((1, D_MODEL, tf), lambda i, e, f, cnt: (e, 0, f)),
                  pl.BlockSpec((1, tf, D_MODEL), lambda i, e, f, cnt: (e, f, 0))],
        out_specs=tok(D_MODEL),
        scratch_shapes=[pltpu.VMEM((tm, D_MODEL), BF16), pltpu.VMEM((tm, D_MODEL), F32)])
    return pl.pallas_call(
        _moe_kernel,
        grid_spec=grid_spec,
        out_shape=jax.ShapeDtypeStruct((TOKENS, D_MODEL), F32),
        compiler_params=_params("parallel", "arbitrary", "arbitrary"),
        name="moe_ffn",
    )(counts, x, final_g, h, pt, pc, comb, wg, wu, wd)


def kernel(x, positions, attn_norm, w_in, b_forget, mla_q_norm, w_q_up, mla_kv_norm, w_kv_up, sinks, w_out, ffn_norm, dense_w_gate, dense_w_up, dense_w_down, router, moe_w_gate, moe_w_up, moe_w_down, final_norm):
    assert x.shape == (BATCH, SEQ, D_MODEL) and positions.shape == (BATCH, SEQ)
    p_idx, p_sgn = _proj_columns()
    sel = np.zeros((w_in.shape[-1], len(p_idx)), np.float32)
    sel[p_idx, np.arange(len(p_idx))] = p_sgn
    p_sel = jnp.asarray(sel, BF16)
    q_idx, q_sgn = _mla_q_columns()
    kv_idx, kv_sgn = _mla_kv_columns()
    mix_rows = _mix_rows()
    cos, sin = _rope_tables(positions)
    xt = x.reshape(TOKENS, D_MODEL).astype(F32)
    assert DEPTH == 2 and moe_w_gate.shape[0] == 1
    to_round = [moe_w_gate[0], moe_w_up[0], moe_w_down[0]]
    moe_w = []
    for layer in range(DEPTH):
        w = jnp.dot(w_in[layer].astype(BF16), p_sel, preferred_element_type=BF16)
        wq = _take_columns(w_q_up[layer].astype(BF16), q_idx, q_sgn)
        wkv = _take_columns(w_kv_up[layer].astype(BF16), kv_idx, kv_sgn)
        fq, fk, fv, fl, pb, qc, kc, vc, pd = _project(
            xt, attn_norm[layer].reshape(1, D_MODEL), w, mla_q_norm[layer].reshape(1, MLA_Q_LORA), wq,
            mla_kv_norm[layer].reshape(1, MLA_KV_LORA), wkv, cos, sin)
        y_a, rounded = _flash(fq, _fox_keys(fl, b_forget[layer], fk), fv, "fox_attention",
                              to_round.pop(0) if to_round else None, decay_skip=True)
        moe_w += [rounded] if rounded is not None else []
        y_b = _swa(pb, sinks[layer])
        y_c, rounded = _flash(qc, kc, vc, "mla_attention", to_round.pop(0) if to_round else None)
        moe_w += [rounded] if rounded is not None else []
        y_d = _dilated(pd)
        cuts = [0] + [r for r in range(1, len(mix_rows)) if mix_rows[r] != mix_rows[r - 1] + 1] + [len(mix_rows)]
        w_o16 = w_out[layer].astype(BF16)
        w_o = jnp.concatenate([w_o16[int(mix_rows[a]):int(mix_rows[a]) + (b - a)]
                               for a, b in zip(cuts[:-1], cuts[1:])], axis=0)
        ys = (y_a, y_b, y_c, y_d)
        j = layer // 2
        g = ffn_norm[layer].reshape(1, D_MODEL)
        if layer % 2 == 0:
            xt = _ffn(xt, ys, w_o, g, dense_w_gate, dense_w_up, dense_w_down, j)
        else:
            assert layer == DEPTH - 1
            router_pad = jnp.zeros((D_MODEL, LANES), F32).at[:, :N_EXPERTS].set(router[j])
            xt, h, comb, pc, pt, cnt = _route(xt, ys, w_o, g, router_pad)
            counts = jnp.concatenate([cnt[0::8, :N_EXPERTS].reshape(-1), cnt[1::8, :N_EXPERTS].reshape(-1)])
            counts = counts.astype(jnp.int32)
            assert len(moe_w) == 3
            xt = _moe(xt, final_norm.reshape(1, D_MODEL), h, pt, pc, comb, counts, *moe_w)
    return xt.reshape(BATCH, SEQ, D_MODEL)
```
